```python
import jax
import jax.numpy as jnp
from jax import lax
import numpy as np

D_MODEL = 2048
BATCH = 4
SEQ = 2048
DEPTH = 1
DEC_BATCH = 128
DEC_SEQ = 1
PAST_LEN = 16384
PAGE_SIZE = 128

CHUNK = 128
A_WIDTH = D_MODEL
A_GROUPS = 8
A_GROUP_DIM = A_WIDTH // A_GROUPS
B_WIDTH = D_MODEL
CONV_WIDTH = 31
MOE_GROUPS = 4
EXPERTS_PER_GROUP = 8
N_EXPERTS = MOE_GROUPS * EXPERTS_PER_GROUP
TOP_K = 2
D_EXPERT = D_MODEL // 2
MOE_BLOCK = 128
PLE_DIM = 256
EPS = 1e-6
SPLITS = [A_WIDTH, 2 * A_WIDTH, 2 * A_WIDTH + B_WIDTH, 2 * A_WIDTH + 2 * B_WIDTH,
          2 * A_WIDTH + 2 * B_WIDTH + D_MODEL]
N_IN = 2 * A_WIDTH + 2 * B_WIDTH + 2 * D_MODEL

kernel_name = 'hybrid_gmlp_conformer_hiermoe_step'


def rms_norm(x, g):
    xf = x.astype(jnp.float32)
    y = xf * lax.rsqrt(jnp.mean(xf * xf, axis=-1, keepdims=True) + EPS)
    return (y * g).astype(x.dtype)


def layer_norm(x, g, b):
    xf = x.astype(jnp.float32)
    mu = jnp.mean(xf, axis=-1, keepdims=True)
    xc = xf - mu
    y = xc * lax.rsqrt(jnp.mean(xc * xc, axis=-1, keepdims=True) + EPS)
    return (y * g + b).astype(x.dtype)


def chunk_spatial_mix(v, w_s, b_s):
    bsz, length = v.shape[0], v.shape[1]
    c = min(length, CHUNK)
    padded = -(-length // c) * c
    v = jnp.pad(v, ((0, 0), (0, padded - length), (0, 0), (0, 0)))
    v = v.reshape(bsz, padded // c, c, A_GROUPS, A_GROUP_DIM)
    w = jnp.tril(w_s[:, :c, :c])
    out = jnp.einsum('gts,bnsgd->bntgd', w, v) + b_s[:, :c].T[None, None, :, :, None]
    return out.reshape(bsz, padded, A_GROUPS, A_GROUP_DIM)[:, :length]


def hier_moe(x, w_rg, b_rg, w_re, b_re, w_gate, w_up, w_down):
    n_tok = x.shape[0]
    lg = (x @ w_rg).astype(jnp.float32) + b_rg.astype(jnp.float32)
    pg = jax.nn.softmax(lg, axis=-1)
    gi = jnp.argmax(lg, axis=-1)
    gw = jnp.take_along_axis(pg, gi[:, None], axis=-1)
    le = ((x @ w_re).astype(jnp.float32) + b_re.astype(jnp.float32)).reshape(
        n_tok, MOE_GROUPS, EXPERTS_PER_GROUP)
    le = jnp.take_along_axis(le, gi[:, None, None], axis=1)[:, 0]
    tv, ti = lax.top_k(jax.nn.softmax(le, axis=-1), TOP_K)
    ew = gw * tv / jnp.sum(tv, axis=-1, keepdims=True)
    eid = gi[:, None] * EXPERTS_PER_GROUP + ti
    n_rows = n_tok * TOP_K
    flat_e = eid.reshape(-1).astype(jnp.int32)
    flat_tok = jnp.repeat(jnp.arange(n_tok, dtype=jnp.int32), TOP_K)
    flat_w = ew.reshape(-1)
    order = jnp.argsort(flat_e)
    se, stok, sw = flat_e[order], flat_tok[order], flat_w[order]
    counts = jax.ops.segment_sum(jnp.ones_like(flat_e), flat_e, num_segments=N_EXPERTS)
    starts = jnp.cumsum(counts) - counts
    pcounts = (counts + MOE_BLOCK - 1) // MOE_BLOCK * MOE_BLOCK
    pends = jnp.cumsum(pcounts)
    pstarts = pends - pcounts
    dest = pstarts[se] + jnp.arange(n_rows, dtype=jnp.int32) - starts[se]
    n_blocks = -(-n_rows // MOE_BLOCK) + N_EXPERTS
    buf_tok = jnp.full((n_blocks * MOE_BLOCK,), n_tok, jnp.int32).at[dest].set(stok)
    buf_w = jnp.zeros((n_blocks * MOE_BLOCK,), jnp.float32).at[dest].set(sw)
    blk_e = jnp.minimum(jnp.searchsorted(pends, jnp.arange(n_blocks, dtype=jnp.int32) * MOE_BLOCK,
                                         side='right'), N_EXPERTS - 1)
    x_pad = jnp.concatenate([x, jnp.zeros((1, x.shape[1]), x.dtype)], axis=0)
    xb = x_pad[buf_tok].reshape(n_blocks, MOE_BLOCK, x.shape[1])

    def expert_block(args):
        xe, e = args
        h = jax.nn.silu(xe @ w_gate[e]) * (xe @ w_up[e])
        return h @ w_down[e]

    yb = lax.map(expert_block, (xb, blk_e)).reshape(n_blocks * MOE_BLOCK, x.shape[1])
    yb = (yb * buf_w[:, None]).astype(x.dtype)
    return jax.ops.segment_sum(yb, buf_tok, num_segments=n_tok + 1)[:n_tok]


def trunk_layer(x, p, conv_prefix, norm_mix, w_in, ln_v_g, ln_v_b, w_spatial, b_spatial, w_proj_a,
                conv_w, conv_b, ln_c_g, ln_c_b, w_proj_b, w_out, norm_ffn, w_rg, b_rg, w_re, b_re,
                w_eg, w_eu, w_ed, norm_ple, w_ple_gate, w_ple_proj):
    bsz, length, _ = x.shape
    xn = rms_norm(x, norm_mix)
    z = xn @ w_in
    u, v, c_val, c_gate, g_a, g_b = jnp.split(z, SPLITS, axis=-1)
    u = jax.nn.gelu(u)
    v = layer_norm(jax.nn.gelu(v), ln_v_g, ln_v_b)
    s = chunk_spatial_mix(v.reshape(bsz, length, A_GROUPS, A_GROUP_DIM), w_spatial, b_spatial)
    branch_a = (u * s.reshape(bsz, length, A_WIDTH)) @ w_proj_a
    glu = c_val * jax.nn.sigmoid(c_gate)
    conv_in = jnp.concatenate([conv_prefix, glu], axis=1)
    conv = lax.conv_general_dilated(conv_in, conv_w[:, None, :], window_strides=(1,), padding='VALID',
                                    dimension_numbers=('NWC', 'WIO', 'NWC'),
                                    feature_group_count=B_WIDTH) + conv_b
    branch_b = jax.nn.silu(layer_norm(conv, ln_c_g, ln_c_b)) @ w_proj_b
    h = x + (jax.nn.sigmoid(g_a) * branch_a + jax.nn.sigmoid(g_b) * branch_b) @ w_out
    hn = rms_norm(h, norm_ffn).reshape(bsz * length, D_MODEL)
    h = h + hier_moe(hn, w_rg, b_rg, w_re, b_re, w_eg, w_eu, w_ed).reshape(bsz, length, D_MODEL)
    h = h + jax.nn.sigmoid(rms_norm(h, norm_ple) @ w_ple_gate) * (p @ w_ple_proj)
    return h, conv_in[:, -(CONV_WIDTH - 1):], v


def setup_inputs(seed: int = 0) -> dict:
    key = jax.random.key(seed)
    ks = iter(jax.random.split(key, 40))

    def nrm(shape, scale):
        return jax.random.normal(next(ks), shape, jnp.float32) * scale

    def gain(shape):
        return 1.0 + nrm(shape, 0.02)

    return {
        'x_prompt': nrm((BATCH, SEQ, D_MODEL), 1.0),
        'x_sample': nrm((DEC_BATCH, DEC_SEQ, D_MODEL), 1.0),
        'state_conv': nrm((DEPTH, DEC_BATCH, CONV_WIDTH - 1, B_WIDTH), 0.5),
        'p_prompt': nrm((DEPTH, BATCH, SEQ, PLE_DIM), 1.0),
        'p_sample': nrm((DEPTH, DEC_BATCH, DEC_SEQ, PLE_DIM), 1.0),
        'norm_mix': gain((DEPTH, D_MODEL)),
        'w_in': nrm((DEPTH, D_MODEL, N_IN), D_MODEL ** -0.5),
        'ln_v_g': gain((DEPTH, A_WIDTH)),
        'ln_v_b': nrm((DEPTH, A_WIDTH), 0.02),
        'w_spatial': nrm((DEPTH, A_GROUPS, CHUNK, CHUNK), CHUNK ** -0.5),
        'b_spatial': 1.0 + nrm((DEPTH, A_GROUPS, CHUNK), 0.1),
        'w_proj_a': nrm((DEPTH, A_WIDTH, D_MODEL), A_WIDTH ** -0.5),
        'conv_w': nrm((DEPTH, CONV_WIDTH, B_WIDTH), CONV_WIDTH ** -0.5),
        'conv_b': nrm((DEPTH, B_WIDTH), 0.02),
        'ln_c_g': gain((DEPTH, B_WIDTH)),
        'ln_c_b': nrm((DEPTH, B_WIDTH), 0.02),
        'w_proj_b': nrm((DEPTH, B_WIDTH, D_MODEL), B_WIDTH ** -0.5),
        'w_out': nrm((DEPTH, D_MODEL, D_MODEL), D_MODEL ** -0.5),
        'norm_ffn': gain((DEPTH, D_MODEL)),
        'w_router_group': nrm((DEPTH, D_MODEL, MOE_GROUPS), D_MODEL ** -0.5),
        'b_router_group': nrm((DEPTH, MOE_GROUPS), 0.01),
        'w_router_expert': nrm((DEPTH, D_MODEL, N_EXPERTS), D_MODEL ** -0.5),
        'b_router_expert': nrm((DEPTH, N_EXPERTS), 0.01),
        'w_exp_gate': nrm((DEPTH, N_EXPERTS, D_MODEL, D_EXPERT), D_MODEL ** -0.5),
        'w_exp_up': nrm((DEPTH, N_EXPERTS, D_MODEL, D_EXPERT), D_MODEL ** -0.5),
        'w_exp_down': nrm((DEPTH, N_EXPERTS, D_EXPERT, D_MODEL), D_EXPERT ** -0.5),
        'norm_ple': gain((DEPTH, D_MODEL)),
        'w_ple_gate': nrm((DEPTH, D_MODEL, D_MODEL), D_MODEL ** -0.5),
        'w_ple_proj': nrm((DEPTH, PLE_DIM, D_MODEL), PLE_DIM ** -0.5),
        'final_norm': gain((D_MODEL,)),
    }


def reference(x_prompt, x_sample, state_conv, p_prompt, p_sample, norm_mix, w_in, ln_v_g, ln_v_b,
              w_spatial, b_spatial, w_proj_a, conv_w, conv_b, ln_c_g, ln_c_b, w_proj_b, w_out,
              norm_ffn, w_router_group, b_router_group, w_router_expert, b_router_expert,
              w_exp_gate, w_exp_up, w_exp_down, norm_ple, w_ple_gate, w_ple_proj, final_norm):
    h_p, h_s = x_prompt, x_sample
    conv_p, conv_s, chunk_v_s = [], [], []
    for i in range(DEPTH):
        lp = (norm_mix[i], w_in[i], ln_v_g[i], ln_v_b[i], w_spatial[i], b_spatial[i], w_proj_a[i],
              conv_w[i], conv_b[i], ln_c_g[i], ln_c_b[i], w_proj_b[i], w_out[i], norm_ffn[i],
              w_router_group[i], b_router_group[i], w_router_expert[i], b_router_expert[i],
              w_exp_gate[i], w_exp_up[i], w_exp_down[i], norm_ple[i], w_ple_gate[i], w_ple_proj[i])
        prefix = jnp.zeros((h_p.shape[0], CONV_WIDTH - 1, B_WIDTH), h_p.dtype)
        h_p, cp, _ = trunk_layer(h_p, p_prompt[i], prefix, *lp)
        h_s, cs, vs = trunk_layer(h_s, p_sample[i], state_conv[i], *lp)
        conv_p.append(cp)
        conv_s.append(cs)
        chunk_v_s.append(vs)
    y_prompt = rms_norm(h_p, final_norm)
    y_sample = rms_norm(h_s, final_norm)
    return (y_prompt, y_sample, jnp.stack(conv_p), jnp.stack(conv_s), jnp.stack(chunk_v_s))
```

```python
import functools

import jax
import jax.numpy as jnp
from jax import lax
from jax.experimental import pallas as pl
from jax.experimental.pallas import tpu as pltpu

F32 = jnp.float32
BF16 = jnp.bfloat16

D_MODEL = 2048
N_PROMPT_SEQ = 4
SEQ = 2048
N_PROMPT = N_PROMPT_SEQ * SEQ
N_SAMPLE = 128
N_ROWS = N_PROMPT + N_SAMPLE
CHUNK = 128
N_CHUNKS = N_ROWS // CHUNK
CHUNKS_PER_SEQ = SEQ // CHUNK
A_GROUPS = 8
A_GROUP_DIM = D_MODEL // A_GROUPS
CONV_WIDTH = 31
CONV_HIST = CONV_WIDTH - 1
HIST_BLOCK = 32
MOE_GROUPS = 4
EXPERTS_PER_GROUP = 8
N_EXPERTS = MOE_GROUPS * EXPERTS_PER_GROUP
TOP_K = 2
D_EXPERT = D_MODEL // 2
PLE_DIM = 256
EPS = 1e-6
LANES = 128

IN_TM = 832
IN_TN = 512
IN_COL_TILES = D_MODEL // IN_TN
MIX_TM = 208
FIN_TM = 320
ROW_BLOCK = 128
N_PAIRS = N_ROWS * TOP_K
MAX_BLOCKS = N_PAIRS // ROW_BLOCK + N_EXPERTS
ITEM_BLOCKS = 8
ITEM_ROWS = ITEM_BLOCKS * ROW_BLOCK
MAX_ITEMS = (MAX_BLOCKS + (ITEM_BLOCKS - 1) * N_EXPERTS) // ITEM_BLOCKS
N_SLOTS = MAX_BLOCKS * ROW_BLOCK
SLOT_TABLE = N_SLOTS + ITEM_ROWS
EXPERT_SLICES = 4
EXPERT_TN = D_EXPERT // EXPERT_SLICES
DUMP_ROW0 = TOP_K * N_ROWS
PAIR_ROWS = DUMP_ROW0 + ROW_BLOCK

VMEM_LIMIT = 56 * 1024 * 1024


def _rms(x, g):
    return x * lax.rsqrt(jnp.mean(x * x, axis=-1, keepdims=True) + EPS) * g


def _layer_norm(x, g, b):
    mu = jnp.mean(x, axis=-1, keepdims=True)
    xc = x - mu
    return xc * lax.rsqrt(jnp.mean(xc * xc, axis=-1, keepdims=True) + EPS) * g + b


def _sigmoid(x):
    return 1.0 / (1.0 + jnp.exp(-x))


def _silu(x):
    return x * _sigmoid(x)


def _gelu(x):
    return jax.nn.gelu(x, approximate=True)


def _bdot(a, b):
    return jnp.dot(a, b, preferred_element_type=F32)


def _cast_kernel(w_ref, o_ref):
    o_ref[...] = w_ref[...].astype(BF16)


def _cast_bf16(w, rows):
    k, n = w.shape
    return pl.pallas_call(
        _cast_kernel,
        out_shape=jax.ShapeDtypeStruct((k, n), BF16),
        grid=(k // rows,),
        in_specs=[pl.BlockSpec((rows, n), lambda i: (i, 0))],
        out_specs=pl.BlockSpec((rows, n), lambda i: (i, 0)),
        name="cast_bf16",
    )(w)


def _in_w_col(j):
    t = IN_COL_TILES
    jj = j - 2 * t
    glu_col = 2 * t + (jj % 2) * t + jj // 2
    return jnp.where((j >= 2 * t) & (j < 4 * t), glu_col, j)


def _in_proj_kernel(x_ref, nm_ref, w_ref, gu_ref, gv_ref, glu_ref, sga_ref, sgb_ref, xn_ref, val_ref):
    j = pl.program_id(1)
    t = IN_COL_TILES

    @pl.when(j == 0)
    def _():
        xn_ref[...] = _rms(x_ref[...], nm_ref[...]).astype(BF16)

    def z():
        return _bdot(xn_ref[...], w_ref[...].astype(BF16))

    @pl.when(j < t)
    def _():
        gu_ref[...] = _gelu(z())

    @pl.when((j >= t) & (j < 2 * t))
    def _():
        gv_ref[...] = _gelu(z())

    @pl.when((j >= 2 * t) & (j < 4 * t) & (j % 2 == 0))
    def _():
        val_ref[...] = z()

    @pl.when((j >= 2 * t) & (j < 4 * t) & (j % 2 == 1))
    def _():
        glu_ref[...] = val_ref[...] * _sigmoid(z())

    @pl.when((j >= 4 * t) & (j < 5 * t))
    def _():
        sga_ref[...] = _sigmoid(z())

    @pl.when(j >= 5 * t)
    def _():
        sgb_ref[...] = _sigmoid(z())


def _in_proj(x_all, norm_mix, w_in):
    t = IN_COL_TILES

    def out_spec(lo, div=1):
        return pl.BlockSpec((IN_TM, IN_TN), lambda i, j: (i, jnp.clip((j - lo) // div, 0, t - 1)))

    return pl.pallas_call(
        _in_proj_kernel,
        out_shape=(
            jax.ShapeDtypeStruct((N_ROWS, D_MODEL), F32),
            jax.ShapeDtypeStruct((N_ROWS, D_MODEL), F32),
            jax.ShapeDtypeStruct((N_ROWS, D_MODEL), F32),
            jax.ShapeDtypeStruct((N_ROWS, D_MODEL), F32),
            jax.ShapeDtypeStruct((N_ROWS, D_MODEL), F32),
        ),
        grid=(N_ROWS // IN_TM, 6 * t),
        in_specs=[
            pl.BlockSpec((IN_TM, D_MODEL), lambda i, j: (i, 0)),
            pl.BlockSpec((1, D_MODEL), lambda i, j: (0, 0)),
            pl.BlockSpec((D_MODEL, IN_TN), lambda i, j: (0, _in_w_col(j))),
        ],
        out_specs=(out_spec(0), out_spec(t), out_spec(2 * t, 2), out_spec(4 * t), out_spec(5 * t)),
        scratch_shapes=[pltpu.VMEM((IN_TM, D_MODEL), BF16), pltpu.VMEM((IN_TM, IN_TN), F32)],
        compiler_params=pltpu.CompilerParams(
            dimension_semantics=("arbitrary", "arbitrary"), vmem_limit_bytes=VMEM_LIMIT),
        name="in_proj",
    )(x_all, norm_mix, w_in)


SCONV_B = 8


def _sconv_kernel(st_ref, glu_ref, w_ref, wl_ref, cb_ref, new_ref, conv_ref):
    st = st_ref[...]
    glu = glu_ref[...]
    new_ref[:, 0:CONV_HIST - 1, :] = st[:, 1:CONV_HIST, :]
    new_ref[:, CONV_HIST - 1, :] = glu
    acc = st[:, 0, :] * w_ref[0:1, :]
    for k in range(1, CONV_HIST):
        acc = acc + st[:, k, :] * w_ref[k:k + 1, :]
    conv_ref[...] = acc + glu * wl_ref[...] + cb_ref[...]


def _sconv(state, glu_s, conv_w, conv_b):
    return pl.pallas_call(
        _sconv_kernel,
        out_shape=(
            jax.ShapeDtypeStruct((N_SAMPLE, CONV_HIST, D_MODEL), F32),
            jax.ShapeDtypeStruct((N_SAMPLE, D_MODEL), F32),
        ),
        grid=(N_SAMPLE // SCONV_B,),
        in_specs=[
            pl.BlockSpec((SCONV_B, CONV_HIST, D_MODEL), lambda i: (i, 0, 0)),
            pl.BlockSpec((SCONV_B, D_MODEL), lambda i: (i, 0)),
            pl.BlockSpec((CONV_HIST, D_MODEL), lambda i: (0, 0)),
            pl.BlockSpec((1, D_MODEL), lambda i: (0, 0)),
            pl.BlockSpec((1, D_MODEL), lambda i: (0, 0)),
        ],
        out_specs=(
            pl.BlockSpec((SCONV_B, CONV_HIST, D_MODEL), lambda i: (i, 0, 0)),
            pl.BlockSpec((SCONV_B, D_MODEL), lambda i: (i, 0)),
        ),
        name="sconv",
    )(state, glu_s, conv_w[:CONV_HIST], conv_w[CONV_HIST:], conv_b)


CONV_STRIP = 256


def _seqmix_kernel(gv_ref, gu_ref, cur_ref, prev_ref, convs_ref, wmix_ref, bmix_ref, w00_ref, b00_ref,
                   lvg_ref, lvb_ref, cw_ref, cb_ref, lcg_ref, lcb_ref, a_ref, b_ref, vns_ref, win_ref, conv_ref):
    c = pl.program_id(0)
    is_sample = c == N_CHUNKS - 1

    vn = _layer_norm(gv_ref[...], lvg_ref[...], lvb_ref[...])

    @pl.when(is_sample)
    def _():
        vns_ref[...] = vn

    vnb = vn.astype(BF16)
    for g in range(A_GROUPS):
        sl = slice(g * A_GROUP_DIM, (g + 1) * A_GROUP_DIM)
        s_chunk = _bdot(wmix_ref[g], vnb[:, sl]) + bmix_ref[:, sl]
        s_first = vn[:, sl] * w00_ref[:, sl] + b00_ref[:, sl]
        a_ref[:, sl] = (gu_ref[:, sl] * jnp.where(is_sample, s_first, s_chunk)).astype(BF16)

    fresh = (c % CHUNKS_PER_SEQ) == 0
    win_ref[0:HIST_BLOCK, :] = jnp.where(fresh, 0.0, prev_ref[...]).astype(BF16).astype(F32)
    win_ref[HIST_BLOCK:, :] = cur_ref[...].astype(BF16).astype(F32)
    off = HIST_BLOCK - CONV_HIST

    def tap(k, cs):
        return win_ref[off + k:off + k + CHUNK, cs] * cw_ref[k:k + 1, cs].astype(BF16).astype(F32)

    for s0 in range(0, D_MODEL, CONV_STRIP):
        cs = slice(s0, s0 + CONV_STRIP)
        acc = tap(0, cs)
        for k in range(1, CONV_WIDTH):
            acc = acc + tap(k, cs)
        conv_ref[:, cs] = acc + cb_ref[:, cs]
    conv = jnp.where(is_sample, convs_ref[...], conv_ref[...])
    b_ref[...] = _silu(_layer_norm(conv, lcg_ref[...], lcb_ref[...])).astype(BF16)


def _seqmix(gv, gu, glu, conv_s, wmix, bmix, w00, b00, ln_v_g, ln_v_b, conv_w, conv_b, ln_c_g, ln_c_b):
    row = lambda c: (c, 0)
    const2 = lambda c: (0, 0)
    vec = pl.BlockSpec((1, D_MODEL), const2)
    hist_per_chunk = CHUNK // HIST_BLOCK
    return pl.pallas_call(
        _seqmix_kernel,
        out_shape=(
            jax.ShapeDtypeStruct((N_ROWS, D_MODEL), BF16),
            jax.ShapeDtypeStruct((N_ROWS, D_MODEL), BF16),
            jax.ShapeDtypeStruct((N_SAMPLE, D_MODEL), F32),
        ),
        grid=(N_CHUNKS,),
        in_specs=[
            pl.BlockSpec((CHUNK, D_MODEL), row),
            pl.BlockSpec((CHUNK, D_MODEL), row),
            pl.BlockSpec((CHUNK, D_MODEL), row),
            pl.BlockSpec((HIST_BLOCK, D_MODEL), lambda c: (jnp.maximum(c * hist_per_chunk - 1, 0), 0)),
            pl.BlockSpec((N_SAMPLE, D_MODEL), const2),
            pl.BlockSpec((A_GROUPS, CHUNK, CHUNK), lambda c: (0, 0, 0)),
            pl.BlockSpec((CHUNK, D_MODEL), const2),
            vec, vec, vec, vec,
            pl.BlockSpec((CONV_WIDTH + 1, D_MODEL), const2),
            vec, vec, vec,
        ],
        out_specs=(
            pl.BlockSpec((CHUNK, D_MODEL), row),
            pl.BlockSpec((CHUNK, D_MODEL), row),
            pl.BlockSpec((N_SAMPLE, D_MODEL), const2),
        ),
        scratch_shapes=[pltpu.VMEM((HIST_BLOCK + CHUNK, D_MODEL), F32), pltpu.VMEM((CHUNK, D_MODEL), F32)],
        compiler_params=pltpu.CompilerParams(
            dimension_semantics=("arbitrary",), vmem_limit_bytes=VMEM_LIMIT),
        name="seqmix",
    )(gv, gu, glu, glu, conv_s, wmix, bmix, w00, b00, ln_v_g, ln_v_b, conv_w, conv_b, ln_c_g, ln_c_b)


def _route(logits):
    col = lax.broadcasted_iota(jnp.int32, logits.shape, 1).astype(F32)
    neg = jnp.float32(-jnp.inf)
    big = jnp.float32(1e9)
    lg = jnp.where(col < MOE_GROUPS, logits, neg)
    gmax = jnp.max(lg, axis=-1, keepdims=True)
    gi = jnp.min(jnp.where(lg == gmax, col, big), axis=-1, keepdims=True)
    gw = 1.0 / jnp.sum(jnp.exp(lg - gmax), axis=-1, keepdims=True)
    lo = MOE_GROUPS + gi * EXPERTS_PER_GROUP
    le = jnp.where((col >= lo) & (col < lo + EXPERTS_PER_GROUP), logits, neg)
    m1 = jnp.max(le, axis=-1, keepdims=True)
    i1 = jnp.min(jnp.where(le == m1, col, big), axis=-1, keepdims=True)
    le2 = jnp.where(col == i1, neg, le)
    m2 = jnp.max(le2, axis=-1, keepdims=True)
    i2 = jnp.min(jnp.where(le2 == m2, col, big), axis=-1, keepdims=True)
    e = jnp.exp(m2 - m1)
    w1 = gw / (1.0 + e)
    w2 = gw * e / (1.0 + e)
    eid = jnp.where(col == 0, i1 - MOE_GROUPS, jnp.where(col == 1, i2 - MOE_GROUPS, 0.0)).astype(jnp.int32)
    ew = jnp.where(col == 0, w1, jnp.where(col == 1, w2, 0.0))
    return eid, ew


def _mixout_kernel(a_ref, b_ref, sga_ref, sgb_ref, x_ref, wa_hbm, wb_hbm, wo_hbm, nf_ref, wr_ref, br_ref,
                   h_ref, eid_ref, ew_ref, wa_ref, wb_ref, wo_ref, sem):
    @pl.when(pl.program_id(0) == 0)
    def _():
        copies = [pltpu.make_async_copy(src, dst, sem.at[k])
                  for k, (src, dst) in enumerate(((wa_hbm, wa_ref), (wb_hbm, wb_ref), (wo_hbm, wo_ref)))]
        for cp in copies:
            cp.start()
        for cp in copies:
            cp.wait()

    pa = _bdot(a_ref[...], wa_ref[...])
    pb = _bdot(b_ref[...], wb_ref[...])
    m = (sga_ref[...] * pa + sgb_ref[...] * pb).astype(BF16)
    h = x_ref[...] + _bdot(m, wo_ref[...])
    h_ref[...] = h
    hn = _rms(h, nf_ref[...])
    logits = _bdot(hn.astype(BF16), wr_ref[...]) + br_ref[...]
    eid, ew = _route(logits)
    eid_ref[...] = eid
    ew_ref[...] = ew


def _mixout(a_in, b_in, sga, sgb, x_all, wa, wb, wo, norm_ffn, w_route, b_route):
    row = lambda i: (i, 0)
    const2 = lambda i: (0, 0)
    act = pl.BlockSpec((MIX_TM, D_MODEL), row)
    hbm = pl.BlockSpec(memory_space=pltpu.MemorySpace.HBM)
    return pl.pallas_call(
        _mixout_kernel,
        out_shape=(
            jax.ShapeDtypeStruct((N_ROWS, D_MODEL), F32),
            jax.ShapeDtypeStruct((N_ROWS, LANES), jnp.int32),
            jax.ShapeDtypeStruct((N_ROWS, LANES), F32),
        ),
        grid=(N_ROWS // MIX_TM,),
        in_specs=[act, act, act, act, act, hbm, hbm, hbm,
                  pl.BlockSpec((1, D_MODEL), const2),
                  pl.BlockSpec((D_MODEL, LANES), const2),
                  pl.BlockSpec((1, LANES), const2)],
        out_specs=(act, pl.BlockSpec((MIX_TM, LANES), row), pl.BlockSpec((MIX_TM, LANES), row)),
        scratch_shapes=[pltpu.VMEM((D_MODEL, D_MODEL), BF16)] * 3 + [pltpu.SemaphoreType.DMA((3,))],
        compiler_params=pltpu.CompilerParams(
            dimension_semantics=("arbitrary",), vmem_limit_bytes=VMEM_LIMIT),
        name="mixout",
    )(a_in, b_in, sga, sgb, x_all, wa, wb, wo, norm_ffn, w_route, b_route)


def _experts_kernel(item_e, item_blk0, item_nb, src_hbm, dst_hbm, h_hbm, nf_ref, wg_ref, wu_ref, wd_ref,
                    y_hbm, src_s, dst_s, xg_ref, xs_ref, acc_ref, wgb_ref, wub_ref, wdb_ref,
                    idx_sem, g_sem, s_sem):
    w = pl.program_id(0)
    s = pl.program_id(1)
    nb = item_nb[w]
    slot0 = pl.multiple_of(item_blk0[w] * ROW_BLOCK, ROW_BLOCK)

    def row_copy_in(r, blk):
        return pltpu.make_async_copy(h_hbm.at[pl.ds(src_s[r], 1)], xg_ref.at[pl.ds(r, 1)], g_sem.at[blk])

    def row_copy_out(r, blk):
        return pltpu.make_async_copy(xg_ref.at[pl.ds(r, 1)], y_hbm.at[pl.ds(dst_s[r], 1)], s_sem.at[blk])

    def block_copy_wait(sem, blk):
        rows = pl.ds(pl.multiple_of(blk * ROW_BLOCK, ROW_BLOCK), ROW_BLOCK)
        pltpu.make_async_copy(xg_ref.at[rows], xg_ref.at[rows], sem.at[blk]).wait()

    @pl.when((w == 0) & (s == 0))
    def _():
        pad_rows = pl.ds(0, ROW_BLOCK)
        xg_ref[pad_rows, :] = jnp.zeros((ROW_BLOCK, D_MODEL), F32)
        cz = pltpu.make_async_copy(xg_ref.at[pad_rows], y_hbm.at[pl.ds(DUMP_ROW0, ROW_BLOCK)], idx_sem.at[0])
        cz.start()
        cz.wait()

    @pl.when((s == 0) & (nb > 0))
    def _():
        ci = pltpu.make_async_copy(src_hbm.at[pl.ds(slot0, ITEM_ROWS)], src_s, idx_sem.at[0])
        co = pltpu.make_async_copy(dst_hbm.at[pl.ds(slot0, ITEM_ROWS)], dst_s, idx_sem.at[1])
        ci.start()
        co.start()
        ci.wait()
        co.wait()

        def issue(blk, carry):
            def one(i, carry2):
                row_copy_in(blk * ROW_BLOCK + i, blk).start()
                return carry2
            return lax.fori_loop(0, ROW_BLOCK, one, carry, unroll=8)
        lax.fori_loop(0, nb, issue, 0)

        def land(blk, carry):
            block_copy_wait(g_sem, blk)
            rows = pl.ds(pl.multiple_of(blk * ROW_BLOCK, ROW_BLOCK), ROW_BLOCK)
            xs_ref[rows, :] = _rms(xg_ref[rows, :], nf_ref[...]).astype(BF16)
            return carry
        lax.fori_loop(0, nb, land, 0)

    @pl.when(nb > 0)
    def _():
        wgb_ref[...] = wg_ref[...].astype(BF16)
        wub_ref[...] = wu_ref[...].astype(BF16)
        wdb_ref[...] = wd_ref[...].astype(BF16)

        def block(blk, carry):
            rows = pl.ds(pl.multiple_of(blk * ROW_BLOCK, ROW_BLOCK), ROW_BLOCK)
            x = xs_ref[rows, :]
            hm = (_silu(_bdot(x, wgb_ref[...])) * _bdot(x, wub_ref[...])).astype(BF16)
            y = _bdot(hm, wdb_ref[...])

            @pl.when(s == 0)
            def _():
                acc_ref[rows, :] = y

            @pl.when(s != 0)
            def _():
                acc_ref[rows, :] += y
            return carry
        lax.fori_loop(0, nb, block, 0)

    @pl.when((s == EXPERT_SLICES - 1) & (nb > 0))
    def _():
        def send(blk, carry):
            rows = pl.ds(pl.multiple_of(blk * ROW_BLOCK, ROW_BLOCK), ROW_BLOCK)
            xg_ref[rows, :] = acc_ref[rows, :]

            def one(i, carry2):
                row_copy_out(blk * ROW_BLOCK + i, blk).start()
                return carry2
            return lax.fori_loop(0, ROW_BLOCK, one, carry, unroll=8)
        lax.fori_loop(0, nb, send, 0)

        def drain(blk, carry):
            block_copy_wait(s_sem, blk)
            return carry
        lax.fori_loop(0, nb, drain, 0)


def _experts(item_e, item_blk0, item_nb, src_tok, dst_row, h_all, norm_ffn, w_gate, w_up, w_down):
    hbm = pl.BlockSpec(memory_space=pltpu.MemorySpace.HBM)

    def w_slice(w, s, inb):
        return jnp.where(inb[w] > 0, s, EXPERT_SLICES - 1)

    grid_spec = pltpu.PrefetchScalarGridSpec(
        num_scalar_prefetch=3,
        grid=(MAX_ITEMS, EXPERT_SLICES),
        in_specs=[
            hbm, hbm, hbm,
            pl.BlockSpec((1, D_MODEL), lambda w, s, ie, ib, inb: (0, 0)),
            pl.BlockSpec((None, D_MODEL, EXPERT_TN), lambda w, s, ie, ib, inb: (ie[w], 0, w_slice(w, s, inb))),
            pl.BlockSpec((None, D_MODEL, EXPERT_TN), lambda w, s, ie, ib, inb: (ie[w], 0, w_slice(w, s, inb))),
            pl.BlockSpec((None, EXPERT_TN, D_MODEL), lambda w, s, ie, ib, inb: (ie[w], w_slice(w, s, inb), 0)),
        ],
        out_specs=hbm,
        scratch_shapes=[
            pltpu.SMEM((ITEM_ROWS,), jnp.int32),
            pltpu.SMEM((ITEM_ROWS,), jnp.int32),
            pltpu.VMEM((ITEM_ROWS, D_MODEL), F32),
            pltpu.VMEM((ITEM_ROWS, D_MODEL), BF16),
            pltpu.VMEM((ITEM_ROWS, D_MODEL), F32),
            pltpu.VMEM((D_MODEL, EXPERT_TN), BF16),
            pltpu.VMEM((D_MODEL, EXPERT_TN), BF16),
            pltpu.VMEM((EXPERT_TN, D_MODEL), BF16),
            pltpu.SemaphoreType.DMA((2,)),
            pltpu.SemaphoreType.DMA((ITEM_BLOCKS,)),
            pltpu.SemaphoreType.DMA((ITEM_BLOCKS,)),
        ],
    )
    return pl.pallas_call(
        _experts_kernel,
        out_shape=jax.ShapeDtypeStruct((PAIR_ROWS, D_MODEL), F32),
        grid_spec=grid_spec,
        compiler_params=pltpu.CompilerParams(
            dimension_semantics=("arbitrary", "arbitrary"), vmem_limit_bytes=VMEM_LIMIT),
        name="experts",
    )(item_e, item_blk0, item_nb, src_tok, dst_row, h_all, norm_ffn, w_gate, w_up, w_down)


def _dispatch_tables(eid, ):
    flat_e = eid.reshape(-1)
    onehot = (flat_e[:, None] == jnp.arange(N_EXPERTS, dtype=jnp.int32)[None, :]).astype(jnp.int32)
    rank = jnp.take_along_axis(jnp.cumsum(onehot, axis=0) - onehot, flat_e[:, None], axis=1)[:, 0]
    counts = jnp.sum(onehot, axis=0)
    nblk = (counts + ROW_BLOCK - 1) // ROW_BLOCK
    blk0 = jnp.cumsum(nblk) - nblk
    slot = blk0[flat_e] * ROW_BLOCK + rank
    pair = jnp.arange(N_PAIRS, dtype=jnp.int32)
    src_tok = jnp.zeros((SLOT_TABLE,), jnp.int32).at[slot].set(pair // TOP_K)
    pad_dst = DUMP_ROW0 + jnp.arange(SLOT_TABLE, dtype=jnp.int32) % ROW_BLOCK
    dst_row = pad_dst.at[slot].set((pair % TOP_K) * N_ROWS + pair // TOP_K)
    nitem = (nblk + ITEM_BLOCKS - 1) // ITEM_BLOCKS
    item_end = jnp.cumsum(nitem)
    total = item_end[-1]
    idx = jnp.arange(MAX_ITEMS, dtype=jnp.int32)
    live = idx < total
    e_of = jnp.minimum(jnp.searchsorted(item_end, jnp.minimum(idx, total - 1), side='right'),
                       N_EXPERTS - 1).astype(jnp.int32)
    local = jnp.minimum(idx, total - 1) - (item_end - nitem)[e_of]
    item_blk0 = (blk0[e_of] + local * ITEM_BLOCKS).astype(jnp.int32)
    item_nb = jnp.where(live, jnp.clip(nblk[e_of] - local * ITEM_BLOCKS, 0, ITEM_BLOCKS), 0).astype(jnp.int32)
    return e_of, item_blk0, item_nb, src_tok, dst_row


def _final_kernel(h_ref, y0_ref, y1_ref, ew_ref, p_ref, wg_hbm, wp_hbm, np_ref, fn_ref, o_ref,
                  wg_ref, wp_ref, sem):
    @pl.when(pl.program_id(0) == 0)
    def _():
        c0 = pltpu.make_async_copy(wg_hbm, wg_ref, sem.at[0])
        c1 = pltpu.make_async_copy(wp_hbm, wp_ref, sem.at[1])
        c0.start()
        c1.start()
        c0.wait()
        c1.wait()

    ew = ew_ref[...]
    h = h_ref[...] + (ew[:, 0:1] * y0_ref[...] + ew[:, 1:2] * y1_ref[...])
    gate = _sigmoid(_bdot(_rms(h, np_ref[...]).astype(BF16), wg_ref[...]))
    h = h + gate * _bdot(p_ref[...].astype(BF16), wp_ref[...])
    o_ref[...] = _rms(h, fn_ref[...])


def _final(h_all, ypairs, ew, p_all, w_gate, w_proj, norm_ple, final_norm):
    row = lambda i: (i, 0)
    const2 = lambda i: (0, 0)
    act = pl.BlockSpec((FIN_TM, D_MODEL), row)
    hbm = pl.BlockSpec(memory_space=pltpu.MemorySpace.HBM)
    steps = N_ROWS // FIN_TM
    return pl.pallas_call(
        _final_kernel,
        out_shape=jax.ShapeDtypeStruct((N_ROWS, D_MODEL), F32),
        grid=(steps,),
        in_specs=[act, act,
                  pl.BlockSpec((FIN_TM, D_MODEL), lambda i: (i + steps, 0)),
                  pl.BlockSpec((FIN_TM, LANES), row),
                  pl.BlockSpec((FIN_TM, PLE_DIM), row),
                  hbm, hbm,
                  pl.BlockSpec((1, D_MODEL), const2),
                  pl.BlockSpec((1, D_MODEL), const2)],
        out_specs=act,
        scratch_shapes=[pltpu.VMEM((D_MODEL, D_MODEL), BF16), pltpu.VMEM((PLE_DIM, D_MODEL), BF16),
                        pltpu.SemaphoreType.DMA((2,))],
        compiler_params=pltpu.CompilerParams(
            dimension_semantics=("arbitrary",), vmem_limit_bytes=VMEM_LIMIT),
        name="final",
    )(h_all, ypairs, ypairs, ew, p_all, w_gate, w_proj, norm_ple, final_norm)


def kernel(x_prompt, x_sample, state_conv, p_prompt, p_sample, norm_mix, w_in, ln_v_g, ln_v_b, w_spatial,
           b_spatial, w_proj_a, conv_w, conv_b, ln_c_g, ln_c_b, w_proj_b, w_out, norm_ffn, w_router_group,
           b_router_group, w_router_expert, b_router_expert, w_exp_gate, w_exp_up, w_exp_down, norm_ple,
           w_ple_gate, w_ple_proj, final_norm):
    assert w_in.shape[0] == 1, "single layer"
    x_all = jnp.concatenate([x_prompt.reshape(N_PROMPT, D_MODEL), x_sample.reshape(N_SAMPLE, D_MODEL)], axis=0)
    p_all = jnp.concatenate([p_prompt[0].reshape(N_PROMPT, PLE_DIM), p_sample[0].reshape(N_SAMPLE, PLE_DIM)],
                            axis=0)
    vec = lambda v: v.reshape(1, -1)

    w_s, b_s = w_spatial[0], b_spatial[0]
    wmix = jnp.tril(w_s).astype(BF16)
    bmix = jnp.repeat(b_s.T, A_GROUP_DIM, axis=1)
    w00 = jnp.repeat(w_s[:, 0, 0], A_GROUP_DIM).reshape(1, D_MODEL)
    b00 = jnp.repeat(b_s[:, 0], A_GROUP_DIM).reshape(1, D_MODEL)

    gu, gv, glu, sga, sgb = _in_proj(x_all, vec(norm_mix[0]), w_in[0])

    cw = conv_w[0]
    new_conv_sample, conv_s = _sconv(state_conv[0], glu[N_PROMPT:], cw, vec(conv_b[0]))
    cw_pad = jnp.concatenate([cw, jnp.zeros((1, D_MODEL), F32)], axis=0)
    a_in, b_in, vn_s = _seqmix(gv, gu, glu, conv_s, wmix, bmix, w00, b00, vec(ln_v_g[0]), vec(ln_v_b[0]), cw_pad,
                               vec(conv_b[0]), vec(ln_c_g[0]), vec(ln_c_b[0]))

    w_route = jnp.zeros((D_MODEL, LANES), F32)
    w_route = w_route.at[:, :MOE_GROUPS].set(w_router_group[0])
    w_route = w_route.at[:, MOE_GROUPS:MOE_GROUPS + N_EXPERTS].set(w_router_expert[0])
    b_route = jnp.zeros((1, LANES), F32)
    b_route = b_route.at[0, :MOE_GROUPS].set(b_router_group[0])
    b_route = b_route.at[0, MOE_GROUPS:MOE_GROUPS + N_EXPERTS].set(b_router_expert[0])

    wa = _cast_bf16(w_proj_a[0], 512)
    wb = _cast_bf16(w_proj_b[0], 512)
    wo = _cast_bf16(w_out[0], 512)
    h_all, eid, ew = _mixout(a_in, b_in, sga, sgb, x_all, wa, wb, wo, vec(norm_ffn[0]),
                             w_route.astype(BF16), b_route)

    tables = _dispatch_tables(eid[:, :TOP_K])
    ypairs = _experts(*tables, h_all, vec(norm_ffn[0]), w_exp_gate[0], w_exp_up[0], w_exp_down[0])

    wpg = _cast_bf16(w_ple_gate[0], 512)
    wpp = _cast_bf16(w_ple_proj[0], PLE_DIM)
    y_all = _final(h_all, ypairs, ew, p_all, wpg, wpp, vec(norm_ple[0]), vec(final_norm))

    y_prompt = y_all[:N_PROMPT].reshape(N_PROMPT_SEQ, SEQ, D_MODEL)
    y_sample = y_all[N_PROMPT:].reshape(N_SAMPLE, 1, D_MODEL)
    new_conv_prompt = glu[:N_PROMPT].reshape(N_PROMPT_SEQ, SEQ, D_MODEL)[:, SEQ - CONV_HIST:][None]
    return (y_prompt, y_sample, new_conv_prompt, new_conv_sample[None], vn_s.reshape(1, N_SAMPLE, 1, D_MODEL))
```

```python
import jax
import jax.numpy as jnp
from jax import lax
from jax.experimental import pallas as pl
from jax.experimental.pallas import tpu as pltpu

F32 = jnp.float32
BF16 = jnp.bfloat16

D_MODEL = 2048
N_PROMPT_SEQ = 4
SEQ = 2048
N_PROMPT = N_PROMPT_SEQ * SEQ
N_SAMPLE = 128
N_ROWS = N_PROMPT + N_SAMPLE
CHUNK = 128
N_CHUNKS = N_ROWS // CHUNK
CHUNKS_PER_SEQ = SEQ // CHUNK
A_GROUPS = 8
A_GROUP_DIM = D_MODEL // A_GROUPS
CONV_WIDTH = 31
CONV_HIST = CONV_WIDTH - 1
HIST_BLOCK = 32
MOE_GROUPS = 4
EXPERTS_PER_GROUP = 8
N_EXPERTS = MOE_GROUPS * EXPERTS_PER_GROUP
TOP_K = 2
D_EXPERT = D_MODEL // 2
PLE_DIM = 256
EPS = 1e-6
LANES = 128

IN_TM = 832
IN_TN = 512
IN_COL_TILES = D_MODEL // IN_TN
MIX_TM = 208
FIN_TM = 256
FIN_PROMPT_STEPS = N_PROMPT // FIN_TM
ROW_BLOCK = 128
N_PAIRS = N_ROWS * TOP_K
MAX_BLOCKS = N_PAIRS // ROW_BLOCK + N_EXPERTS
ITEM_BLOCKS = 8
ITEM_ROWS = ITEM_BLOCKS * ROW_BLOCK
MAX_ITEMS = (MAX_BLOCKS + (ITEM_BLOCKS - 1) * N_EXPERTS) // ITEM_BLOCKS
N_SLOTS = MAX_BLOCKS * ROW_BLOCK
SLOT_TABLE = N_SLOTS + ITEM_ROWS
EXPERT_SLICES = 4
EXPERT_TN = D_EXPERT // EXPERT_SLICES
PAIR_REGION = 8448
DUMP_ROW0 = PAIR_REGION + N_ROWS
PAIR_ROWS = DUMP_ROW0 + ROW_BLOCK

VMEM_LIMIT = 56 * 1024 * 1024


def _rms(x, g):
    return x * lax.rsqrt(jnp.mean(x * x, axis=-1, keepdims=True) + EPS) * g


def _layer_norm(x, g, b):
    mu = jnp.mean(x, axis=-1, keepdims=True)
    xc = x - mu
    return xc * lax.rsqrt(jnp.mean(xc * xc, axis=-1, keepdims=True) + EPS) * g + b


def _sigmoid(x):
    return 1.0 / (1.0 + jnp.exp(-x))


def _silu(x):
    return x * _sigmoid(x)


def _gelu(x):
    return jax.nn.gelu(x, approximate=True)


def _bdot(a, b):
    return jnp.dot(a, b, preferred_element_type=F32)


def _cast_kernel(w_ref, o_ref):
    o_ref[...] = w_ref[...].astype(BF16)


def _cast_bf16(w, rows):
    k, n = w.shape
    return pl.pallas_call(
        _cast_kernel,
        out_shape=jax.ShapeDtypeStruct((k, n), BF16),
        grid=(k // rows,),
        in_specs=[pl.BlockSpec((rows, n), lambda i: (i, 0))],
        out_specs=pl.BlockSpec((rows, n), lambda i: (i, 0)),
        name="cast_bf16",
    )(w)


def _in_w_col(j):
    t = IN_COL_TILES
    jj = j - 2 * t
    glu_col = 2 * t + (jj % 2) * t + jj // 2
    return jnp.where((j >= 2 * t) & (j < 4 * t), glu_col, j)


def _in_proj_kernel(x_ref, nm_ref, w_ref, gu_ref, gv_ref, glu_ref, sga_ref, sgb_ref, xn_ref, val_ref):
    j = pl.program_id(1)
    t = IN_COL_TILES

    @pl.when(j == 0)
    def _():
        xn_ref[...] = _rms(x_ref[...], nm_ref[...]).astype(BF16)

    def z():
        return _bdot(xn_ref[...], w_ref[...].astype(BF16))

    @pl.when(j < t)
    def _():
        gu_ref[...] = _gelu(z())

    @pl.when((j >= t) & (j < 2 * t))
    def _():
        gv_ref[...] = _gelu(z())

    @pl.when((j >= 2 * t) & (j < 4 * t) & (j % 2 == 0))
    def _():
        val_ref[...] = z()

    @pl.when((j >= 2 * t) & (j < 4 * t) & (j % 2 == 1))
    def _():
        glu_ref[...] = val_ref[...] * _sigmoid(z())

    @pl.when((j >= 4 * t) & (j < 5 * t))
    def _():
        sga_ref[...] = _sigmoid(z())

    @pl.when(j >= 5 * t)
    def _():
        sgb_ref[...] = _sigmoid(z())


def _in_proj(x_all, norm_mix, w_in):
    t = IN_COL_TILES

    def out_spec(lo, div=1):
        return pl.BlockSpec((IN_TM, IN_TN), lambda i, j: (i, jnp.clip((j - lo) // div, 0, t - 1)))

    return pl.pallas_call(
        _in_proj_kernel,
        out_shape=(
            jax.ShapeDtypeStruct((N_ROWS, D_MODEL), F32),
            jax.ShapeDtypeStruct((N_ROWS, D_MODEL), F32),
            jax.ShapeDtypeStruct((N_ROWS, D_MODEL), F32),
            jax.ShapeDtypeStruct((N_ROWS, D_MODEL), F32),
            jax.ShapeDtypeStruct((N_ROWS, D_MODEL), F32),
        ),
        grid=(N_ROWS // IN_TM, 6 * t),
        in_specs=[
            pl.BlockSpec((IN_TM, D_MODEL), lambda i, j: (i, 0)),
            pl.BlockSpec((1, D_MODEL), lambda i, j: (0, 0)),
            pl.BlockSpec((D_MODEL, IN_TN), lambda i, j: (0, _in_w_col(j))),
        ],
        out_specs=(out_spec(0), out_spec(t), out_spec(2 * t, 2), out_spec(4 * t), out_spec(5 * t)),
        scratch_shapes=[pltpu.VMEM((IN_TM, D_MODEL), BF16), pltpu.VMEM((IN_TM, IN_TN), F32)],
        compiler_params=pltpu.CompilerParams(
            dimension_semantics=("arbitrary", "arbitrary"), vmem_limit_bytes=VMEM_LIMIT),
        name="in_proj",
    )(x_all, norm_mix, w_in)


SCONV_B = 8


def _sconv_kernel(st_ref, glu_ref, w_ref, wl_ref, cb_ref, new_ref, conv_ref):
    st = st_ref[...]
    glu = glu_ref[...]
    new_ref[:, 0:CONV_HIST - 1, :] = st[:, 1:CONV_HIST, :]
    new_ref[:, CONV_HIST - 1, :] = glu
    acc = st[:, 0, :] * w_ref[0:1, :]
    for k in range(1, CONV_HIST):
        acc = acc + st[:, k, :] * w_ref[k:k + 1, :]
    conv_ref[...] = acc + glu * wl_ref[...] + cb_ref[...]


def _sconv(state, glu_s, conv_w, conv_b):
    return pl.pallas_call(
        _sconv_kernel,
        out_shape=(
            jax.ShapeDtypeStruct((N_SAMPLE, CONV_HIST, D_MODEL), F32),
            jax.ShapeDtypeStruct((N_SAMPLE, D_MODEL), F32),
        ),
        grid=(N_SAMPLE // SCONV_B,),
        in_specs=[
            pl.BlockSpec((SCONV_B, CONV_HIST, D_MODEL), lambda i: (i, 0, 0)),
            pl.BlockSpec((SCONV_B, D_MODEL), lambda i: (i, 0)),
            pl.BlockSpec((CONV_HIST, D_MODEL), lambda i: (0, 0)),
            pl.BlockSpec((1, D_MODEL), lambda i: (0, 0)),
            pl.BlockSpec((1, D_MODEL), lambda i: (0, 0)),
        ],
        out_specs=(
            pl.BlockSpec((SCONV_B, CONV_HIST, D_MODEL), lambda i: (i, 0, 0)),
            pl.BlockSpec((SCONV_B, D_MODEL), lambda i: (i, 0)),
        ),
        name="sconv",
    )(state, glu_s, conv_w[:CONV_HIST], conv_w[CONV_HIST:], conv_b)


CONV_STRIP = 256


def _seqmix_kernel(gv_ref, gu_ref, cur_ref, prev_ref, convs_ref, wmix_ref, bmix_ref, w00_ref, b00_ref,
                   lvg_ref, lvb_ref, cw_ref, cb_ref, lcg_ref, lcb_ref, a_ref, b_ref, vns_ref, win_ref, conv_ref):
    c = pl.program_id(0)
    is_sample = c == N_CHUNKS - 1

    vn = _layer_norm(gv_ref[...], lvg_ref[...], lvb_ref[...])

    @pl.when(is_sample)
    def _():
        vns_ref[...] = vn

    vnb = vn.astype(BF16)
    for g in range(A_GROUPS):
        sl = slice(g * A_GROUP_DIM, (g + 1) * A_GROUP_DIM)
        s_chunk = _bdot(wmix_ref[g], vnb[:, sl]) + bmix_ref[:, sl]
        s_first = vn[:, sl] * w00_ref[:, sl] + b00_ref[:, sl]
        a_ref[:, sl] = (gu_ref[:, sl] * jnp.where(is_sample, s_first, s_chunk)).astype(BF16)

    fresh = (c % CHUNKS_PER_SEQ) == 0
    win_ref[0:HIST_BLOCK, :] = jnp.where(fresh, 0.0, prev_ref[...]).astype(BF16).astype(F32)
    win_ref[HIST_BLOCK:, :] = cur_ref[...].astype(BF16).astype(F32)
    off = HIST_BLOCK - CONV_HIST

    def tap(k, cs):
        return win_ref[off + k:off + k + CHUNK, cs] * cw_ref[k:k + 1, cs].astype(BF16).astype(F32)

    for s0 in range(0, D_MODEL, CONV_STRIP):
        cs = slice(s0, s0 + CONV_STRIP)
        acc = tap(0, cs)
        for k in range(1, CONV_WIDTH):
            acc = acc + tap(k, cs)
        conv_ref[:, cs] = acc + cb_ref[:, cs]
    conv = jnp.where(is_sample, convs_ref[...], conv_ref[...])
    b_ref[...] = _silu(_layer_norm(conv, lcg_ref[...], lcb_ref[...])).astype(BF16)


def _seqmix(gv, gu, glu, conv_s, wmix, bmix, w00, b00, ln_v_g, ln_v_b, conv_w, conv_b, ln_c_g, ln_c_b):
    row = lambda c: (c, 0)
    const2 = lambda c: (0, 0)
    vec = pl.BlockSpec((1, D_MODEL), const2)
    hist_per_chunk = CHUNK // HIST_BLOCK
    return pl.pallas_call(
        _seqmix_kernel,
        out_shape=(
            jax.ShapeDtypeStruct((N_ROWS, D_MODEL), BF16),
            jax.ShapeDtypeStruct((N_ROWS, D_MODEL), BF16),
            jax.ShapeDtypeStruct((N_SAMPLE, D_MODEL), F32),
        ),
        grid=(N_CHUNKS,),
        in_specs=[
            pl.BlockSpec((CHUNK, D_MODEL), row),
            pl.BlockSpec((CHUNK, D_MODEL), row),
            pl.BlockSpec((CHUNK, D_MODEL), row),
            pl.BlockSpec((HIST_BLOCK, D_MODEL), lambda c: (jnp.maximum(c * hist_per_chunk - 1, 0), 0)),
            pl.BlockSpec((N_SAMPLE, D_MODEL), const2),
            pl.BlockSpec((A_GROUPS, CHUNK, CHUNK), lambda c: (0, 0, 0)),
            pl.BlockSpec((CHUNK, D_MODEL), const2),
            vec, vec, vec, vec,
            pl.BlockSpec((CONV_WIDTH + 1, D_MODEL), const2),
            vec, vec, vec,
        ],
        out_specs=(
            pl.BlockSpec((CHUNK, D_MODEL), row),
            pl.BlockSpec((CHUNK, D_MODEL), row),
            pl.BlockSpec((N_SAMPLE, D_MODEL), const2),
        ),
        scratch_shapes=[pltpu.VMEM((HIST_BLOCK + CHUNK, D_MODEL), F32), pltpu.VMEM((CHUNK, D_MODEL), F32)],
        compiler_params=pltpu.CompilerParams(
            dimension_semantics=("arbitrary",), vmem_limit_bytes=VMEM_LIMIT),
        name="seqmix",
    )(gv, gu, glu, glu, conv_s, wmix, bmix, w00, b00, ln_v_g, ln_v_b, conv_w, conv_b, ln_c_g, ln_c_b)


def _route(logits):
    col = lax.broadcasted_iota(jnp.int32, logits.shape, 1).astype(F32)
    neg = jnp.float32(-jnp.inf)
    big = jnp.float32(1e9)
    lg = jnp.where(col < MOE_GROUPS, logits, neg)
    gmax = jnp.max(lg, axis=-1, keepdims=True)
    gi = jnp.min(jnp.where(lg == gmax, col, big), axis=-1, keepdims=True)
    gw = 1.0 / jnp.sum(jnp.exp(lg - gmax), axis=-1, keepdims=True)
    lo = MOE_GROUPS + gi * EXPERTS_PER_GROUP
    le = jnp.where((col >= lo) & (col < lo + EXPERTS_PER_GROUP), logits, neg)
    m1 = jnp.max(le, axis=-1, keepdims=True)
    i1 = jnp.min(jnp.where(le == m1, col, big), axis=-1, keepdims=True)
    le2 = jnp.where(col == i1, neg, le)
    m2 = jnp.max(le2, axis=-1, keepdims=True)
    i2 = jnp.min(jnp.where(le2 == m2, col, big), axis=-1, keepdims=True)
    e = jnp.exp(m2 - m1)
    w1 = gw / (1.0 + e)
    w2 = gw * e / (1.0 + e)
    eid = jnp.where(col == 0, i1 - MOE_GROUPS, jnp.where(col == 1, i2 - MOE_GROUPS, 0.0)).astype(jnp.int32)
    ew = jnp.where(col == 0, w1, jnp.where(col == 1, w2, 0.0))
    return eid, ew


def _mixout_kernel(a_ref, b_ref, sga_ref, sgb_ref, x_ref, wa_hbm, wb_hbm, wo_hbm, nf_ref, wr_ref, br_ref,
                   h_ref, eid_ref, ew_ref, wa_ref, wb_ref, wo_ref, sem):
    @pl.when(pl.program_id(0) == 0)
    def _():
        copies = [pltpu.make_async_copy(src, dst, sem.at[k])
                  for k, (src, dst) in enumerate(((wa_hbm, wa_ref), (wb_hbm, wb_ref), (wo_hbm, wo_ref)))]
        for cp in copies:
            cp.start()
        for cp in copies:
            cp.wait()

    pa = _bdot(a_ref[...], wa_ref[...])
    pb = _bdot(b_ref[...], wb_ref[...])
    m = (sga_ref[...] * pa + sgb_ref[...] * pb).astype(BF16)
    h = x_ref[...] + _bdot(m, wo_ref[...])
    h_ref[...] = h
    hn = _rms(h, nf_ref[...])
    logits = _bdot(hn.astype(BF16), wr_ref[...]) + br_ref[...]
    eid, ew = _route(logits)
    eid_ref[...] = eid
    ew_ref[...] = ew


def _mixout(a_in, b_in, sga, sgb, x_all, wa, wb, wo, norm_ffn, w_route, b_route):
    row = lambda i: (i, 0)
    const2 = lambda i: (0, 0)
    act = pl.BlockSpec((MIX_TM, D_MODEL), row)
    hbm = pl.BlockSpec(memory_space=pltpu.MemorySpace.HBM)
    return pl.pallas_call(
        _mixout_kernel,
        out_shape=(
            jax.ShapeDtypeStruct((N_ROWS, D_MODEL), F32),
            jax.ShapeDtypeStruct((N_ROWS, LANES), jnp.int32),
            jax.ShapeDtypeStruct((N_ROWS, LANES), F32),
        ),
        grid=(N_ROWS // MIX_TM,),
        in_specs=[act, act, act, act, act, hbm, hbm, hbm,
                  pl.BlockSpec((1, D_MODEL), const2),
                  pl.BlockSpec((D_MODEL, LANES), const2),
                  pl.BlockSpec((1, LANES), const2)],
        out_specs=(act, pl.BlockSpec((MIX_TM, LANES), row), pl.BlockSpec((MIX_TM, LANES), row)),
        scratch_shapes=[pltpu.VMEM((D_MODEL, D_MODEL), BF16)] * 3 + [pltpu.SemaphoreType.DMA((3,))],
        compiler_params=pltpu.CompilerParams(
            dimension_semantics=("arbitrary",), vmem_limit_bytes=VMEM_LIMIT),
        name="mixout",
    )(a_in, b_in, sga, sgb, x_all, wa, wb, wo, norm_ffn, w_route, b_route)


def _experts_kernel(item_e, item_blk0, item_nb, tbl_hbm, h_hbm, nf_ref, wg_ref, wu_ref, wd_ref,
                    y_hbm, tbl_s, xg_ref, xs_ref, acc_ref, wgb_ref, wub_ref, wdb_ref,
                    idx_sem, g_sem, s_sem):
    w = pl.program_id(0)
    s = pl.program_id(1)
    nb = item_nb[w]
    slot0 = pl.multiple_of(item_blk0[w] * ROW_BLOCK, ROW_BLOCK)

    def src_row(r):
        return jnp.minimum(tbl_s[r] >> 1, N_ROWS - 1)

    def dst_row(r):
        p = tbl_s[r]
        return jnp.where(p < N_PAIRS, (p & 1) * PAIR_REGION + (p >> 1), DUMP_ROW0 + (r & (ROW_BLOCK - 1)))

    def row_copy_in(r, blk):
        return pltpu.make_async_copy(h_hbm.at[pl.ds(src_row(r), 1)], xg_ref.at[pl.ds(r, 1)], g_sem.at[blk])

    def row_copy_out(r, blk):
        return pltpu.make_async_copy(xg_ref.at[pl.ds(r, 1)], y_hbm.at[pl.ds(dst_row(r), 1)], s_sem.at[blk])

    def block_rows(blk):
        return pl.ds(pl.multiple_of(blk * ROW_BLOCK, ROW_BLOCK), ROW_BLOCK)

    def block_copy_wait(sem, blk):
        pltpu.make_async_copy(xg_ref.at[block_rows(blk)], xg_ref.at[block_rows(blk)], sem.at[blk]).wait()

    @pl.when((w == 0) & (s == 0))
    def _():
        pad_rows = pl.ds(0, ROW_BLOCK)
        xg_ref[pad_rows, :] = jnp.zeros((ROW_BLOCK, D_MODEL), F32)
        for row0 in (N_ROWS, DUMP_ROW0):
            cz = pltpu.make_async_copy(xg_ref.at[pad_rows], y_hbm.at[pl.ds(row0, ROW_BLOCK)], idx_sem.at[0])
            cz.start()
            cz.wait()

    @pl.when((s == 0) & (nb > 0))
    def _():
        ci = pltpu.make_async_copy(tbl_hbm.at[pl.ds(slot0, ITEM_ROWS)], tbl_s, idx_sem.at[0])
        ci.start()
        ci.wait()

        def issue(blk, carry):
            def one(i, carry2):
                row_copy_in(blk * ROW_BLOCK + i, blk).start()
                return carry2
            return lax.fori_loop(0, ROW_BLOCK, one, carry, unroll=8)
        lax.fori_loop(0, nb, issue, 0)

        def land(blk, carry):
            block_copy_wait(g_sem, blk)
            xs_ref[block_rows(blk), :] = _rms(xg_ref[block_rows(blk), :], nf_ref[...]).astype(BF16)
            return carry
        lax.fori_loop(0, nb, land, 0)

    @pl.when(nb > 0)
    def _():
        wgb_ref[...] = wg_ref[...].astype(BF16)
        wub_ref[...] = wu_ref[...].astype(BF16)
        wdb_ref[...] = wd_ref[...].astype(BF16)

        def compute(row0, nrows):
            rows = pl.ds(row0, nrows)
            x = xs_ref[rows, :]
            hm = (_silu(_bdot(x, wgb_ref[...])) * _bdot(x, wub_ref[...])).astype(BF16)
            y = _bdot(hm, wdb_ref[...])

            @pl.when(s == 0)
            def _():
                acc_ref[rows, :] = y

            @pl.when(s != 0)
            def _():
                acc_ref[rows, :] += y

        def two_blocks(j, carry):
            compute(pl.multiple_of(j * (2 * ROW_BLOCK), 2 * ROW_BLOCK), 2 * ROW_BLOCK)
            return carry
        lax.fori_loop(0, nb >> 1, two_blocks, 0)

        @pl.when((nb & 1) == 1)
        def _():
            compute(pl.multiple_of((nb - 1) * ROW_BLOCK, ROW_BLOCK), ROW_BLOCK)

    @pl.when((s == EXPERT_SLICES - 1) & (nb > 0))
    def _():
        def send(blk, carry):
            xg_ref[block_rows(blk), :] = acc_ref[block_rows(blk), :]

            def one(i, carry2):
                row_copy_out(blk * ROW_BLOCK + i, blk).start()
                return carry2
            return lax.fori_loop(0, ROW_BLOCK, one, carry, unroll=8)
        lax.fori_loop(0, nb, send, 0)

        def drain(blk, carry):
            block_copy_wait(s_sem, blk)
            return carry
        lax.fori_loop(0, nb, drain, 0)


def _experts(item_e, item_blk0, item_nb, slot_pair, h_all, norm_ffn, w_gate, w_up, w_down):
    hbm = pl.BlockSpec(memory_space=pltpu.MemorySpace.HBM)

    def w_slice(w, s, inb):
        return jnp.where(inb[w] > 0, s, EXPERT_SLICES - 1)

    grid_spec = pltpu.PrefetchScalarGridSpec(
        num_scalar_prefetch=3,
        grid=(MAX_ITEMS, EXPERT_SLICES),
        in_specs=[
            hbm, hbm,
            pl.BlockSpec((1, D_MODEL), lambda w, s, ie, ib, inb: (0, 0)),
            pl.BlockSpec((None, D_MODEL, EXPERT_TN), lambda w, s, ie, ib, inb: (ie[w], 0, w_slice(w, s, inb))),
            pl.BlockSpec((None, D_MODEL, EXPERT_TN), lambda w, s, ie, ib, inb: (ie[w], 0, w_slice(w, s, inb))),
            pl.BlockSpec((None, EXPERT_TN, D_MODEL), lambda w, s, ie, ib, inb: (ie[w], w_slice(w, s, inb), 0)),
        ],
        out_specs=hbm,
        scratch_shapes=[
            pltpu.SMEM((ITEM_ROWS,), jnp.int32),
            pltpu.VMEM((ITEM_ROWS, D_MODEL), F32),
            pltpu.VMEM((ITEM_ROWS, D_MODEL), BF16),
            pltpu.VMEM((ITEM_ROWS, D_MODEL), F32),
            pltpu.VMEM((D_MODEL, EXPERT_TN), BF16),
            pltpu.VMEM((D_MODEL, EXPERT_TN), BF16),
            pltpu.VMEM((EXPERT_TN, D_MODEL), BF16),
            pltpu.SemaphoreType.DMA((1,)),
            pltpu.SemaphoreType.DMA((ITEM_BLOCKS,)),
            pltpu.SemaphoreType.DMA((ITEM_BLOCKS,)),
        ],
    )
    return pl.pallas_call(
        _experts_kernel,
        out_shape=jax.ShapeDtypeStruct((PAIR_ROWS, D_MODEL), F32),
        grid_spec=grid_spec,
        compiler_params=pltpu.CompilerParams(
            dimension_semantics=("arbitrary", "arbitrary"), vmem_limit_bytes=VMEM_LIMIT),
        name="experts",
    )(item_e, item_blk0, item_nb, slot_pair, h_all, norm_ffn, w_gate, w_up, w_down)


def _dispatch_tables(eid):
    flat_e = eid.reshape(-1)
    onehot = (flat_e[:, None] == jnp.arange(N_EXPERTS, dtype=jnp.int32)[None, :]).astype(jnp.int32)
    rank = jnp.take_along_axis(jnp.cumsum(onehot, axis=0) - onehot, flat_e[:, None], axis=1)[:, 0]
    counts = jnp.sum(onehot, axis=0)
    nblk = (counts + ROW_BLOCK - 1) // ROW_BLOCK
    blk0 = jnp.cumsum(nblk) - nblk
    slot = blk0[flat_e] * ROW_BLOCK + rank
    slot_pair = jnp.full((SLOT_TABLE,), N_PAIRS, jnp.int32).at[slot].set(jnp.arange(N_PAIRS, dtype=jnp.int32))
    nitem = (nblk + ITEM_BLOCKS - 1) // ITEM_BLOCKS
    item_end = jnp.cumsum(nitem)
    total = item_end[-1]
    idx = jnp.arange(MAX_ITEMS, dtype=jnp.int32)
    live = idx < total
    e_of = jnp.minimum(jnp.searchsorted(item_end, jnp.minimum(idx, total - 1), side='right'),
                       N_EXPERTS - 1).astype(jnp.int32)
    local = jnp.minimum(idx, total - 1) - (item_end - nitem)[e_of]
    item_blk0 = (blk0[e_of] + local * ITEM_BLOCKS).astype(jnp.int32)
    item_nb = jnp.where(live, jnp.clip(nblk[e_of] - local * ITEM_BLOCKS, 0, ITEM_BLOCKS), 0).astype(jnp.int32)
    return e_of, item_blk0, item_nb, slot_pair


def _final_rows(h, y0, y1, ew, p, wg_ref, wp_ref, np_ref, fn_ref):
    h = h + (ew[:, 0:1] * y0 + ew[:, 1:2] * y1)
    gate = _sigmoid(_bdot(_rms(h, np_ref[...]).astype(BF16), wg_ref[...]))
    h = h + gate * _bdot(p.astype(BF16), wp_ref[...])
    return _rms(h, fn_ref[...])


def _final_kernel(h_ref, y0_ref, y1_ref, ew_ref, p_ref, hs_ref, y0s_ref, y1s_ref, ews_ref, ps_ref,
                  wg_hbm, wp_hbm, np_ref, fn_ref, op_ref, os_ref, wg_ref, wp_ref, sem):
    i = pl.program_id(0)

    @pl.when(i == 0)
    def _():
        c0 = pltpu.make_async_copy(wg_hbm, wg_ref, sem.at[0])
        c1 = pltpu.make_async_copy(wp_hbm, wp_ref, sem.at[1])
        c0.start()
        c1.start()
        c0.wait()
        c1.wait()

    @pl.when(i < FIN_PROMPT_STEPS)
    def _():
        op_ref[...] = _final_rows(h_ref[...], y0_ref[...], y1_ref[...], ew_ref[...], p_ref[...],
                                  wg_ref, wp_ref, np_ref, fn_ref)

    @pl.when(i == FIN_PROMPT_STEPS)
    def _():
        os_ref[...] = _final_rows(hs_ref[...], y0s_ref[...], y1s_ref[...], ews_ref[...], ps_ref[...],
                                  wg_ref, wp_ref, np_ref, fn_ref)


def _final(h_all, ypairs, ew, p_prompt, p_sample, w_gate, w_proj, norm_ple, final_norm):
    last = FIN_PROMPT_STEPS - 1
    prow = lambda i: (jnp.minimum(i, last), 0)
    const2 = lambda i: (0, 0)
    hbm = pl.BlockSpec(memory_space=pltpu.MemorySpace.HBM)
    sblk = N_PROMPT // N_SAMPLE
    pact = pl.BlockSpec((FIN_TM, D_MODEL), prow)
    sact = lambda off: pl.BlockSpec((N_SAMPLE, D_MODEL), lambda i: (off + sblk, 0))
    return pl.pallas_call(
        _final_kernel,
        out_shape=(jax.ShapeDtypeStruct((N_PROMPT, D_MODEL), F32), jax.ShapeDtypeStruct((N_SAMPLE, D_MODEL), F32)),
        grid=(FIN_PROMPT_STEPS + 1,),
        in_specs=[pact, pact,
                  pl.BlockSpec((FIN_TM, D_MODEL), lambda i: (jnp.minimum(i, last) + PAIR_REGION // FIN_TM, 0)),
                  pl.BlockSpec((FIN_TM, LANES), prow),
                  pl.BlockSpec((FIN_TM, PLE_DIM), prow),
                  sact(0), sact(0), sact(PAIR_REGION // N_SAMPLE),
                  pl.BlockSpec((N_SAMPLE, LANES), lambda i: (sblk, 0)),
                  pl.BlockSpec((N_SAMPLE, PLE_DIM), const2),
                  hbm, hbm,
                  pl.BlockSpec((1, D_MODEL), const2),
                  pl.BlockSpec((1, D_MODEL), const2)],
        out_specs=(pact, pl.BlockSpec((N_SAMPLE, D_MODEL), const2)),
        scratch_shapes=[pltpu.VMEM((D_MODEL, D_MODEL), BF16), pltpu.VMEM((PLE_DIM, D_MODEL), BF16),
                        pltpu.SemaphoreType.DMA((2,))],
        compiler_params=pltpu.CompilerParams(
            dimension_semantics=("arbitrary",), vmem_limit_bytes=VMEM_LIMIT),
        name="final",
    )(h_all, ypairs, ypairs, ew, p_prompt, h_all, ypairs, ypairs, ew, p_sample, w_gate, w_proj, norm_ple, final_norm)


def kernel(x_prompt, x_sample, state_conv, p_prompt, p_sample, norm_mix, w_in, ln_v_g, ln_v_b, w_spatial,
           b_spatial, w_proj_a, conv_w, conv_b, ln_c_g, ln_c_b, w_proj_b, w_out, norm_ffn, w_router_group,
           b_router_group, w_router_expert, b_router_expert, w_exp_gate, w_exp_up, w_exp_down, norm_ple,
           w_ple_gate, w_ple_proj, final_norm):
    assert w_in.shape[0] == 1, "single layer"
    x_all = jnp.concatenate([x_prompt.reshape(N_PROMPT, D_MODEL), x_sample.reshape(N_SAMPLE, D_MODEL)], axis=0)
    vec = lambda v: v.reshape(1, -1)

    w_s, b_s = w_spatial[0], b_spatial[0]
    wmix = jnp.tril(w_s).astype(BF16)
    bmix = jnp.repeat(b_s.T, A_GROUP_DIM, axis=1)
    w00 = jnp.repeat(w_s[:, 0, 0], A_GROUP_DIM).reshape(1, D_MODEL)
    b00 = jnp.repeat(b_s[:, 0], A_GROUP_DIM).reshape(1, D_MODEL)

    gu, gv, glu, sga, sgb = _in_proj(x_all, vec(norm_mix[0]), w_in[0])

    cw = conv_w[0]
    new_conv_sample, conv_s = _sconv(state_conv[0], glu[N_PROMPT:], cw, vec(conv_b[0]))
    cw_pad = jnp.concatenate([cw, jnp.zeros((1, D_MODEL), F32)], axis=0)
    a_in, b_in, vn_s = _seqmix(gv, gu, glu, conv_s, wmix, bmix, w00, b00, vec(ln_v_g[0]), vec(ln_v_b[0]), cw_pad,
                               vec(conv_b[0]), vec(ln_c_g[0]), vec(ln_c_b[0]))

    w_route = jnp.zeros((D_MODEL, LANES), F32)
    w_route = w_route.at[:, :MOE_GROUPS].set(w_router_group[0])
    w_route = w_route.at[:, MOE_GROUPS:MOE_GROUPS + N_EXPERTS].set(w_router_expert[0])
    b_route = jnp.zeros((1, LANES), F32)
    b_route = b_route.at[0, :MOE_GROUPS].set(b_router_group[0])
    b_route = b_route.at[0, MOE_GROUPS:MOE_GROUPS + N_EXPERTS].set(b_router_expert[0])

    wa = _cast_bf16(w_proj_a[0], 512)
    wb = _cast_bf16(w_proj_b[0], 512)
    wo = _cast_bf16(w_out[0], 512)
    h_all, eid, ew = _mixout(a_in, b_in, sga, sgb, x_all, wa, wb, wo, vec(norm_ffn[0]),
                             w_route.astype(BF16), b_route)

    tables = _dispatch_tables(eid[:, :TOP_K])
    ypairs = _experts(*tables, h_all, vec(norm_ffn[0]), w_exp_gate[0], w_exp_up[0], w_exp_down[0])

    wpg = _cast_bf16(w_ple_gate[0], 512)
    wpp = _cast_bf16(w_ple_proj[0], PLE_DIM)
    y_prompt, y_sample = _final(h_all, ypairs, ew, p_prompt[0].reshape(N_PROMPT, PLE_DIM),
                                p_sample[0].reshape(N_SAMPLE, PLE_DIM), wpg, wpp, vec(norm_ple[0]), vec(final_norm))
    y_prompt = y_prompt.reshape(N_PROMPT_SEQ, SEQ, D_MODEL)
    y_sample = y_sample.reshape(N_SAMPLE, 1, D_MODEL)
    new_conv_prompt = jnp.stack([glu[(b + 1) * SEQ - CONV_HIST:(b + 1) * SEQ] for b in range(N_PROMPT_SEQ)])[None]
    return (y_prompt, y_sample, new_conv_prompt, new_conv_sample[None], vn_s.reshape(1, N_SAMPLE, 1, D_MODEL))
```

```python
import jax
import jax.numpy as jnp
from jax import lax
from jax.experimental import pallas as pl
from jax.experimental.pallas import tpu as pltpu

F32 = jnp.float32
BF16 = jnp.bfloat16

D_MODEL = 2048
N_PROMPT_SEQ = 4
SEQ = 2048
N_PROMPT = N_PROMPT_SEQ * SEQ
N_SAMPLE = 128
N_ROWS = N_PROMPT + N_SAMPLE
CHUNK = 128
N_CHUNKS = N_ROWS // CHUNK
CHUNKS_PER_SEQ = SEQ // CHUNK
A_GROUPS = 8
A_GROUP_DIM = D_MODEL // A_GROUPS
CONV_WIDTH = 31
CONV_HIST = CONV_WIDTH - 1
HIST_BLOCK = 32
MOE_GROUPS = 4
EXPERTS_PER_GROUP = 8
N_EXPERTS = MOE_GROUPS * EXPERTS_PER_GROUP
TOP_K = 2
D_EXPERT = D_MODEL // 2
PLE_DIM = 256
EPS = 1e-6
LANES = 128

IN_TM = 832
IN_TN = 512
IN_COL_TILES = D_MODEL // IN_TN
MIX_TM = 208
FIN_TM = 256
FIN_PROMPT_STEPS = N_PROMPT // FIN_TM
ROW_BLOCK = 128
N_PAIRS = N_ROWS * TOP_K
MAX_BLOCKS = N_PAIRS // ROW_BLOCK + N_EXPERTS
ITEM_BLOCKS = 8
ITEM_ROWS = ITEM_BLOCKS * ROW_BLOCK
MAX_ITEMS = (MAX_BLOCKS + (ITEM_BLOCKS - 1) * N_EXPERTS) // ITEM_BLOCKS
N_SLOTS = MAX_BLOCKS * ROW_BLOCK
SLOT_TABLE = N_SLOTS + ITEM_ROWS
EXPERT_SLICES = 4
EXPERT_TN = D_EXPERT // EXPERT_SLICES
PAIR_REGION = 8448
DUMP_ROW0 = PAIR_REGION + N_ROWS
PAIR_ROWS = DUMP_ROW0 + ROW_BLOCK

VMEM_LIMIT = 56 * 1024 * 1024


def _rms(x, g):
    return x * lax.rsqrt(jnp.mean(x * x, axis=-1, keepdims=True) + EPS) * g


def _layer_norm(x, g, b):
    mu = jnp.mean(x, axis=-1, keepdims=True)
    xc = x - mu
    return xc * lax.rsqrt(jnp.mean(xc * xc, axis=-1, keepdims=True) + EPS) * g + b


def _sigmoid(x):
    return 1.0 / (1.0 + jnp.exp(-x))


def _silu(x):
    return x * _sigmoid(x)


def _gelu(x):
    return jax.nn.gelu(x, approximate=True)


def _bdot(a, b):
    return jnp.dot(a, b, preferred_element_type=F32)


def _cast_kernel(w_ref, o_ref):
    o_ref[...] = w_ref[...].astype(BF16)


def _cast_bf16(w, rows):
    k, n = w.shape
    return pl.pallas_call(
        _cast_kernel,
        out_shape=jax.ShapeDtypeStruct((k, n), BF16),
        grid=(k // rows,),
        in_specs=[pl.BlockSpec((rows, n), lambda i: (i, 0))],
        out_specs=pl.BlockSpec((rows, n), lambda i: (i, 0)),
        name="cast_bf16",
    )(w)


def _in_w_col(j):
    t = IN_COL_TILES
    jj = j - 2 * t
    glu_col = 2 * t + (jj % 2) * t + jj // 2
    return jnp.where((j >= 2 * t) & (j < 4 * t), glu_col, j)


def _in_proj_kernel(x_ref, nm_ref, w_ref, gu_ref, gv_ref, glu_ref, sga_ref, sgb_ref, xn_ref, val_ref):
    j = pl.program_id(1)
    t = IN_COL_TILES

    @pl.when(j == 0)
    def _():
        xn_ref[...] = _rms(x_ref[...], nm_ref[...]).astype(BF16)

    def z():
        return _bdot(xn_ref[...], w_ref[...].astype(BF16))

    @pl.when(j < t)
    def _():
        gu_ref[...] = _gelu(z())

    @pl.when((j >= t) & (j < 2 * t))
    def _():
        gv_ref[...] = _gelu(z())

    @pl.when((j >= 2 * t) & (j < 4 * t) & (j % 2 == 0))
    def _():
        val_ref[...] = z()

    @pl.when((j >= 2 * t) & (j < 4 * t) & (j % 2 == 1))
    def _():
        glu_ref[...] = val_ref[...] * _sigmoid(z())

    @pl.when((j >= 4 * t) & (j < 5 * t))
    def _():
        sga_ref[...] = _sigmoid(z())

    @pl.when(j >= 5 * t)
    def _():
        sgb_ref[...] = _sigmoid(z())


def _in_proj(x_all, norm_mix, w_in):
    t = IN_COL_TILES

    def out_spec(lo, div=1):
        return pl.BlockSpec((IN_TM, IN_TN), lambda i, j: (i, jnp.clip((j - lo) // div, 0, t - 1)))

    return pl.pallas_call(
        _in_proj_kernel,
        out_shape=(
            jax.ShapeDtypeStruct((N_ROWS, D_MODEL), F32),
            jax.ShapeDtypeStruct((N_ROWS, D_MODEL), F32),
            jax.ShapeDtypeStruct((N_ROWS, D_MODEL), F32),
            jax.ShapeDtypeStruct((N_ROWS, D_MODEL), F32),
            jax.ShapeDtypeStruct((N_ROWS, D_MODEL), F32),
        ),
        grid=(N_ROWS // IN_TM, 6 * t),
        in_specs=[
            pl.BlockSpec((IN_TM, D_MODEL), lambda i, j: (i, 0)),
            pl.BlockSpec((1, D_MODEL), lambda i, j: (0, 0)),
            pl.BlockSpec((D_MODEL, IN_TN), lambda i, j: (0, _in_w_col(j))),
        ],
        out_specs=(out_spec(0), out_spec(t), out_spec(2 * t, 2), out_spec(4 * t), out_spec(5 * t)),
        scratch_shapes=[pltpu.VMEM((IN_TM, D_MODEL), BF16), pltpu.VMEM((IN_TM, IN_TN), F32)],
        compiler_params=pltpu.CompilerParams(
            dimension_semantics=("arbitrary", "arbitrary"), vmem_limit_bytes=VMEM_LIMIT),
        name="in_proj",
    )(x_all, norm_mix, w_in)


SCONV_B = 8


def _sconv_kernel(st_ref, glu_ref, w_ref, wl_ref, cb_ref, new_ref, conv_ref):
    st = st_ref[...]
    glu = glu_ref[...]
    new_ref[:, 0:CONV_HIST - 1, :] = st[:, 1:CONV_HIST, :]
    new_ref[:, CONV_HIST - 1, :] = glu
    acc = st[:, 0, :] * w_ref[0:1, :]
    for k in range(1, CONV_HIST):
        acc = acc + st[:, k, :] * w_ref[k:k + 1, :]
    conv_ref[...] = acc + glu * wl_ref[...] + cb_ref[...]


def _sconv(state, glu_s, conv_w, conv_b):
    return pl.pallas_call(
        _sconv_kernel,
        out_shape=(
            jax.ShapeDtypeStruct((N_SAMPLE, CONV_HIST, D_MODEL), F32),
            jax.ShapeDtypeStruct((N_SAMPLE, D_MODEL), F32),
        ),
        grid=(N_SAMPLE // SCONV_B,),
        in_specs=[
            pl.BlockSpec((SCONV_B, CONV_HIST, D_MODEL), lambda i: (i, 0, 0)),
            pl.BlockSpec((SCONV_B, D_MODEL), lambda i: (i, 0)),
            pl.BlockSpec((CONV_HIST, D_MODEL), lambda i: (0, 0)),
            pl.BlockSpec((1, D_MODEL), lambda i: (0, 0)),
            pl.BlockSpec((1, D_MODEL), lambda i: (0, 0)),
        ],
        out_specs=(
            pl.BlockSpec((SCONV_B, CONV_HIST, D_MODEL), lambda i: (i, 0, 0)),
            pl.BlockSpec((SCONV_B, D_MODEL), lambda i: (i, 0)),
        ),
        name="sconv",
    )(state, glu_s, conv_w[:CONV_HIST], conv_w[CONV_HIST:], conv_b)


CONV_STRIP = 256
SUBLANES = 8
WIN_ROWS = HIST_BLOCK + CHUNK
WIN_PAD = 256


def _seqmix_kernel(gv_ref, gu_ref, cur_ref, prev_ref, convs_ref, wmix_ref, bmix_ref, w00_ref, b00_ref,
                   lvg_ref, lvb_ref, cw_ref, cb_ref, lcg_ref, lcb_ref, sel_ref, a_ref, b_ref, vns_ref,
                   win_ref, shift_ref, conv_ref):
    c = pl.program_id(0)
    is_sample = c == N_CHUNKS - 1

    vn = _layer_norm(gv_ref[...], lvg_ref[...], lvb_ref[...])

    @pl.when(is_sample)
    def _():
        vns_ref[...] = vn

    vnb = vn.astype(BF16)
    for g in range(A_GROUPS):
        sl = slice(g * A_GROUP_DIM, (g + 1) * A_GROUP_DIM)
        s_chunk = _bdot(wmix_ref[g], vnb[:, sl]) + bmix_ref[:, sl]
        s_first = vn[:, sl] * w00_ref[:, sl] + b00_ref[:, sl]
        a_ref[:, sl] = (gu_ref[:, sl] * jnp.where(is_sample, s_first, s_chunk)).astype(BF16)

    fresh = (c % CHUNKS_PER_SEQ) == 0

    @pl.when(c == 0)
    def _():
        win_ref[WIN_ROWS:, :] = jnp.zeros((WIN_PAD - WIN_ROWS, D_MODEL), BF16)

    win_ref[0:HIST_BLOCK, :] = jnp.where(fresh, 0.0, prev_ref[...]).astype(BF16)
    win_ref[HIST_BLOCK:WIN_ROWS, :] = cur_ref[...].astype(BF16)
    shift_ref[0] = win_ref[0:WIN_ROWS, :].astype(F32)
    for r in range(1, SUBLANES):
        shift_ref[r] = _bdot(sel_ref[r - 1], win_ref[...])
    off = HIST_BLOCK - CONV_HIST

    def tap(k, cs):
        q, r = divmod(off + k, SUBLANES)
        return (shift_ref[r, SUBLANES * q:SUBLANES * q + CHUNK, cs]
                * cw_ref[k:k + 1, cs].astype(BF16).astype(F32))

    for s0 in range(0, D_MODEL, CONV_STRIP):
        cs = slice(s0, s0 + CONV_STRIP)
        acc = tap(0, cs)
        for k in range(1, CONV_WIDTH):
            acc = acc + tap(k, cs)
        conv_ref[:, cs] = acc + cb_ref[:, cs]
    conv = jnp.where(is_sample, convs_ref[...], conv_ref[...])
    b_ref[...] = _silu(_layer_norm(conv, lcg_ref[...], lcb_ref[...])).astype(BF16)


def _seqmix(gv, gu, glu, conv_s, wmix, bmix, w00, b00, ln_v_g, ln_v_b, conv_w, conv_b, ln_c_g, ln_c_b, sel):
    row = lambda c: (c, 0)
    const2 = lambda c: (0, 0)
    vec = pl.BlockSpec((1, D_MODEL), const2)
    hist_per_chunk = CHUNK // HIST_BLOCK
    return pl.pallas_call(
        _seqmix_kernel,
        out_shape=(
            jax.ShapeDtypeStruct((N_ROWS, D_MODEL), BF16),
            jax.ShapeDtypeStruct((N_ROWS, D_MODEL), BF16),
            jax.ShapeDtypeStruct((N_SAMPLE, D_MODEL), F32),
        ),
        grid=(N_CHUNKS,),
        in_specs=[
            pl.BlockSpec((CHUNK, D_MODEL), row),
            pl.BlockSpec((CHUNK, D_MODEL), row),
            pl.BlockSpec((CHUNK, D_MODEL), row),
            pl.BlockSpec((HIST_BLOCK, D_MODEL), lambda c: (jnp.maximum(c * hist_per_chunk - 1, 0), 0)),
            pl.BlockSpec((N_SAMPLE, D_MODEL), const2),
            pl.BlockSpec((A_GROUPS, CHUNK, CHUNK), lambda c: (0, 0, 0)),
            pl.BlockSpec((CHUNK, D_MODEL), const2),
            vec, vec, vec, vec,
            pl.BlockSpec((CONV_WIDTH + 1, D_MODEL), const2),
            vec, vec, vec,
            pl.BlockSpec((SUBLANES - 1, WIN_ROWS, WIN_PAD), lambda c: (0, 0, 0)),
        ],
        out_specs=(
            pl.BlockSpec((CHUNK, D_MODEL), row),
            pl.BlockSpec((CHUNK, D_MODEL), row),
            pl.BlockSpec((N_SAMPLE, D_MODEL), const2),
        ),
        scratch_shapes=[pltpu.VMEM((WIN_PAD, D_MODEL), BF16), pltpu.VMEM((SUBLANES, WIN_ROWS, D_MODEL), F32),
                        pltpu.VMEM((CHUNK, D_MODEL), F32)],
        compiler_params=pltpu.CompilerParams(
            dimension_semantics=("arbitrary",), vmem_limit_bytes=VMEM_LIMIT),
        name="seqmix",
    )(gv, gu, glu, glu, conv_s, wmix, bmix, w00, b00, ln_v_g, ln_v_b, conv_w, conv_b, ln_c_g, ln_c_b, sel)


def _route(logits):
    col = lax.broadcasted_iota(jnp.int32, logits.shape, 1).astype(F32)
    neg = jnp.float32(-jnp.inf)
    big = jnp.float32(1e9)
    lg = jnp.where(col < MOE_GROUPS, logits, neg)
    gmax = jnp.max(lg, axis=-1, keepdims=True)
    gi = jnp.min(jnp.where(lg == gmax, col, big), axis=-1, keepdims=True)
    gw = 1.0 / jnp.sum(jnp.exp(lg - gmax), axis=-1, keepdims=True)
    lo = MOE_GROUPS + gi * EXPERTS_PER_GROUP
    le = jnp.where((col >= lo) & (col < lo + EXPERTS_PER_GROUP), logits, neg)
    m1 = jnp.max(le, axis=-1, keepdims=True)
    i1 = jnp.min(jnp.where(le == m1, col, big), axis=-1, keepdims=True)
    le2 = jnp.where(col == i1, neg, le)
    m2 = jnp.max(le2, axis=-1, keepdims=True)
    i2 = jnp.min(jnp.where(le2 == m2, col, big), axis=-1, keepdims=True)
    e = jnp.exp(m2 - m1)
    w1 = gw / (1.0 + e)
    w2 = gw * e / (1.0 + e)
    eid = jnp.where(col == 0, i1 - MOE_GROUPS, jnp.where(col == 1, i2 - MOE_GROUPS, 0.0)).astype(jnp.int32)
    ew = jnp.where(col == 0, w1, jnp.where(col == 1, w2, 0.0))
    return eid, ew


def _mixout_kernel(a_ref, b_ref, sga_ref, sgb_ref, x_ref, wa_hbm, wb_hbm, wo_hbm, nf_ref, wr_ref, br_ref,
                   h_ref, eid_ref, ew_ref, wa_ref, wb_ref, wo_ref, sem):
    @pl.when(pl.program_id(0) == 0)
    def _():
        copies = [pltpu.make_async_copy(src, dst, sem.at[k])
                  for k, (src, dst) in enumerate(((wa_hbm, wa_ref), (wb_hbm, wb_ref), (wo_hbm, wo_ref)))]
        for cp in copies:
            cp.start()
        for cp in copies:
            cp.wait()

    pa = _bdot(a_ref[...], wa_ref[...])
    pb = _bdot(b_ref[...], wb_ref[...])
    m = (sga_ref[...] * pa + sgb_ref[...] * pb).astype(BF16)
    h = x_ref[...] + _bdot(m, wo_ref[...])
    h_ref[...] = h
    hn = _rms(h, nf_ref[...])
    logits = _bdot(hn.astype(BF16), wr_ref[...]) + br_ref[...]
    eid, ew = _route(logits)
    eid_ref[...] = eid
    ew_ref[...] = ew


def _mixout(a_in, b_in, sga, sgb, x_all, wa, wb, wo, norm_ffn, w_route, b_route):
    row = lambda i: (i, 0)
    const2 = lambda i: (0, 0)
    act = pl.BlockSpec((MIX_TM, D_MODEL), row)
    hbm = pl.BlockSpec(memory_space=pltpu.MemorySpace.HBM)
    return pl.pallas_call(
        _mixout_kernel,
        out_shape=(
            jax.ShapeDtypeStruct((N_ROWS, D_MODEL), F32),
            jax.ShapeDtypeStruct((N_ROWS, LANES), jnp.int32),
            jax.ShapeDtypeStruct((N_ROWS, LANES), F32),
        ),
        grid=(N_ROWS // MIX_TM,),
        in_specs=[act, act, act, act, act, hbm, hbm, hbm,
                  pl.BlockSpec((1, D_MODEL), const2),
                  pl.BlockSpec((D_MODEL, LANES), const2),
                  pl.BlockSpec((1, LANES), const2)],
        out_specs=(act, pl.BlockSpec((MIX_TM, LANES), row), pl.BlockSpec((MIX_TM, LANES), row)),
        scratch_shapes=[pltpu.VMEM((D_MODEL, D_MODEL), BF16)] * 3 + [pltpu.SemaphoreType.DMA((3,))],
        compiler_params=pltpu.CompilerParams(
            dimension_semantics=("arbitrary",), vmem_limit_bytes=VMEM_LIMIT),
        name="mixout",
    )(a_in, b_in, sga, sgb, x_all, wa, wb, wo, norm_ffn, w_route, b_route)


def _experts_kernel(item_e, item_blk0, item_nb, src_hbm, dst_hbm, h_hbm, nf_ref, wg_ref, wu_ref, wd_ref,
                    y_hbm, src_s, dst_s, xg_ref, xs_ref, acc_ref, wgb_ref, wub_ref, wdb_ref,
                    idx_sem, g_sem, s_sem):
    w = pl.program_id(0)
    s = pl.program_id(1)
    nb = item_nb[w]
    slot0 = pl.multiple_of(item_blk0[w] * ROW_BLOCK, ROW_BLOCK)

    def row_copy_in(r, blk):
        return pltpu.make_async_copy(h_hbm.at[pl.ds(src_s[r], 1)], xg_ref.at[pl.ds(r, 1)], g_sem.at[blk])

    def row_copy_out(r, blk):
        return pltpu.make_async_copy(xg_ref.at[pl.ds(r, 1)], y_hbm.at[pl.ds(dst_s[r], 1)], s_sem.at[blk])

    def block_rows(blk):
        return pl.ds(pl.multiple_of(blk * ROW_BLOCK, ROW_BLOCK), ROW_BLOCK)

    def block_copy_wait(sem, blk):
        pltpu.make_async_copy(xg_ref.at[block_rows(blk)], xg_ref.at[block_rows(blk)], sem.at[blk]).wait()

    @pl.when((w == 0) & (s == 0))
    def _():
        pad_rows = pl.ds(0, ROW_BLOCK)
        xg_ref[pad_rows, :] = jnp.zeros((ROW_BLOCK, D_MODEL), F32)
        for row0 in (N_ROWS, DUMP_ROW0):
            cz = pltpu.make_async_copy(xg_ref.at[pad_rows], y_hbm.at[pl.ds(row0, ROW_BLOCK)], idx_sem.at[0])
            cz.start()
            cz.wait()

    @pl.when((s == 0) & (nb > 0))
    def _():
        ci = pltpu.make_async_copy(src_hbm.at[pl.ds(slot0, ITEM_ROWS)], src_s, idx_sem.at[0])
        co = pltpu.make_async_copy(dst_hbm.at[pl.ds(slot0, ITEM_ROWS)], dst_s, idx_sem.at[1])
        ci.start()
        co.start()
        ci.wait()
        co.wait()

        def issue(blk, carry):
            def one(i, carry2):
                row_copy_in(blk * ROW_BLOCK + i, blk).start()
                return carry2
            return lax.fori_loop(0, ROW_BLOCK, one, carry, unroll=8)
        lax.fori_loop(0, nb, issue, 0)

        def land(blk, carry):
            block_copy_wait(g_sem, blk)
            xs_ref[block_rows(blk), :] = _rms(xg_ref[block_rows(blk), :], nf_ref[...]).astype(BF16)
            return carry
        lax.fori_loop(0, nb, land, 0)

    @pl.when(nb > 0)
    def _():
        wgb_ref[...] = wg_ref[...].astype(BF16)
        wub_ref[...] = wu_ref[...].astype(BF16)
        wdb_ref[...] = wd_ref[...].astype(BF16)

        def compute(row0, nrows):
            rows = pl.ds(row0, nrows)
            x = xs_ref[rows, :]
            hm = (_silu(_bdot(x, wgb_ref[...])) * _bdot(x, wub_ref[...])).astype(BF16)
            y = _bdot(hm, wdb_ref[...])

            @pl.when(s == 0)
            def _():
                acc_ref[rows, :] = y

            @pl.when(s != 0)
            def _():
                acc_ref[rows, :] += y

        def two_blocks(j, carry):
            compute(pl.multiple_of(j * (2 * ROW_BLOCK), 2 * ROW_BLOCK), 2 * ROW_BLOCK)
            return carry
        lax.fori_loop(0, nb >> 1, two_blocks, 0)

        @pl.when((nb & 1) == 1)
        def _():
            compute(pl.multiple_of((nb - 1) * ROW_BLOCK, ROW_BLOCK), ROW_BLOCK)

    @pl.when((s == EXPERT_SLICES - 1) & (nb > 0))
    def _():
        def send(blk, carry):
            xg_ref[block_rows(blk), :] = acc_ref[block_rows(blk), :]

            def one(i, carry2):
                row_copy_out(blk * ROW_BLOCK + i, blk).start()
                return carry2
            return lax.fori_loop(0, ROW_BLOCK, one, carry, unroll=8)
        lax.fori_loop(0, nb, send, 0)

        def drain(blk, carry):
            block_copy_wait(s_sem, blk)
            return carry
        lax.fori_loop(0, nb, drain, 0)


def _experts(item_e, item_blk0, item_nb, src_row, dst_row, h_all, norm_ffn, w_gate, w_up, w_down):
    hbm = pl.BlockSpec(memory_space=pltpu.MemorySpace.HBM)

    def w_slice(w, s, inb):
        return jnp.where(inb[w] > 0, s, EXPERT_SLICES - 1)

    grid_spec = pltpu.PrefetchScalarGridSpec(
        num_scalar_prefetch=3,
        grid=(MAX_ITEMS, EXPERT_SLICES),
        in_specs=[
            hbm, hbm, hbm,
            pl.BlockSpec((1, D_MODEL), lambda w, s, ie, ib, inb: (0, 0)),
            pl.BlockSpec((None, D_MODEL, EXPERT_TN), lambda w, s, ie, ib, inb: (ie[w], 0, w_slice(w, s, inb))),
            pl.BlockSpec((None, D_MODEL, EXPERT_TN), lambda w, s, ie, ib, inb: (ie[w], 0, w_slice(w, s, inb))),
            pl.BlockSpec((None, EXPERT_TN, D_MODEL), lambda w, s, ie, ib, inb: (ie[w], w_slice(w, s, inb), 0)),
        ],
        out_specs=hbm,
        scratch_shapes=[
            pltpu.SMEM((ITEM_ROWS,), jnp.int32),
            pltpu.SMEM((ITEM_ROWS,), jnp.int32),
            pltpu.VMEM((ITEM_ROWS, D_MODEL), F32),
            pltpu.VMEM((ITEM_ROWS, D_MODEL), BF16),
            pltpu.VMEM((ITEM_ROWS, D_MODEL), F32),
            pltpu.VMEM((D_MODEL, EXPERT_TN), BF16),
            pltpu.VMEM((D_MODEL, EXPERT_TN), BF16),
            pltpu.VMEM((EXPERT_TN, D_MODEL), BF16),
            pltpu.SemaphoreType.DMA((2,)),
            pltpu.SemaphoreType.DMA((ITEM_BLOCKS,)),
            pltpu.SemaphoreType.DMA((ITEM_BLOCKS,)),
        ],
    )
    return pl.pallas_call(
        _experts_kernel,
        out_shape=jax.ShapeDtypeStruct((PAIR_ROWS, D_MODEL), F32),
        grid_spec=grid_spec,
        compiler_params=pltpu.CompilerParams(
            dimension_semantics=("arbitrary", "arbitrary"), vmem_limit_bytes=VMEM_LIMIT),
        name="experts",
    )(item_e, item_blk0, item_nb, src_row, dst_row, h_all, norm_ffn, w_gate, w_up, w_down)


def _dispatch_tables(eid):
    flat_e = eid.reshape(-1)
    onehot = (flat_e[:, None] == jnp.arange(N_EXPERTS, dtype=jnp.int32)[None, :]).astype(jnp.int32)
    rank = jnp.take_along_axis(jnp.cumsum(onehot, axis=0) - onehot, flat_e[:, None], axis=1)[:, 0]
    counts = jnp.sum(onehot, axis=0)
    nblk = (counts + ROW_BLOCK - 1) // ROW_BLOCK
    blk0 = jnp.cumsum(nblk) - nblk
    slot = blk0[flat_e] * ROW_BLOCK + rank
    slot_pair = jnp.full((SLOT_TABLE,), N_PAIRS, jnp.int32).at[slot].set(jnp.arange(N_PAIRS, dtype=jnp.int32))
    slot_id = jnp.arange(SLOT_TABLE, dtype=jnp.int32)
    src_row = jnp.minimum(slot_pair // TOP_K, N_ROWS - 1)
    dst_row = jnp.where(slot_pair < N_PAIRS, (slot_pair % TOP_K) * PAIR_REGION + slot_pair // TOP_K,
                        DUMP_ROW0 + slot_id % ROW_BLOCK)
    nitem = (nblk + ITEM_BLOCKS - 1) // ITEM_BLOCKS
    item_end = jnp.cumsum(nitem)
    total = item_end[-1]
    idx = jnp.arange(MAX_ITEMS, dtype=jnp.int32)
    live = idx < total
    e_of = jnp.minimum(jnp.searchsorted(item_end, jnp.minimum(idx, total - 1), side='right'),
                       N_EXPERTS - 1).astype(jnp.int32)
    local = jnp.minimum(idx, total - 1) - (item_end - nitem)[e_of]
    item_blk0 = (blk0[e_of] + local * ITEM_BLOCKS).astype(jnp.int32)
    item_nb = jnp.where(live, jnp.clip(nblk[e_of] - local * ITEM_BLOCKS, 0, ITEM_BLOCKS), 0).astype(jnp.int32)
    return e_of, item_blk0, item_nb, src_row, dst_row


def _final_rows(h, y0, y1, ew, p, wg_ref, wp_ref, np_ref, fn_ref):
    h = h + (ew[:, 0:1] * y0 + ew[:, 1:2] * y1)
    gate = _sigmoid(_bdot(_rms(h, np_ref[...]).astype(BF16), wg_ref[...]))
    h = h + gate * _bdot(p.astype(BF16), wp_ref[...])
    return _rms(h, fn_ref[...])


def _final_kernel(h_ref, y0_ref, y1_ref, ew_ref, p_ref, hs_ref, y0s_ref, y1s_ref, ews_ref, ps_ref,
                  wg_hbm, wp_hbm, np_ref, fn_ref, op_ref, os_ref, wg_ref, wp_ref, sem):
    i = pl.program_id(0)

    @pl.when(i == 0)
    def _():
        c0 = pltpu.make_async_copy(wg_hbm, wg_ref, sem.at[0])
        c1 = pltpu.make_async_copy(wp_hbm, wp_ref, sem.at[1])
        c0.start()
        c1.start()
        c0.wait()
        c1.wait()

    @pl.when(i < FIN_PROMPT_STEPS)
    def _():
        op_ref[...] = _final_rows(h_ref[...], y0_ref[...], y1_ref[...], ew_ref[...], p_ref[...],
                                  wg_ref, wp_ref, np_ref, fn_ref)

    @pl.when(i == FIN_PROMPT_STEPS)
    def _():
        os_ref[...] = _final_rows(hs_ref[...], y0s_ref[...], y1s_ref[...], ews_ref[...], ps_ref[...],
                                  wg_ref, wp_ref, np_ref, fn_ref)


def _final(h_all, ypairs, ew, p_prompt, p_sample, w_gate, w_proj, norm_ple, final_norm):
    last = FIN_PROMPT_STEPS - 1
    prow = lambda i: (jnp.minimum(i, last), 0)
    const2 = lambda i: (0, 0)
    hbm = pl.BlockSpec(memory_space=pltpu.MemorySpace.HBM)
    sblk = N_PROMPT // N_SAMPLE
    pact = pl.BlockSpec((FIN_TM, D_MODEL), prow)
    sact = lambda off: pl.BlockSpec((N_SAMPLE, D_MODEL), lambda i: (off + sblk, 0))
    return pl.pallas_call(
        _final_kernel,
        out_shape=(jax.ShapeDtypeStruct((N_PROMPT, D_MODEL), F32), jax.ShapeDtypeStruct((N_SAMPLE, D_MODEL), F32)),
        grid=(FIN_PROMPT_STEPS + 1,),
        in_specs=[pact, pact,
                  pl.BlockSpec((FIN_TM, D_MODEL), lambda i: (jnp.minimum(i, last) + PAIR_REGION // FIN_TM, 0)),
                  pl.BlockSpec((FIN_TM, LANES), prow),
                  pl.BlockSpec((FIN_TM, PLE_DIM), prow),
                  sact(0), sact(0), sact(PAIR_REGION // N_SAMPLE),
                  pl.BlockSpec((N_SAMPLE, LANES), lambda i: (sblk, 0)),
                  pl.BlockSpec((N_SAMPLE, PLE_DIM), const2),
                  hbm, hbm,
                  pl.BlockSpec((1, D_MODEL), const2),
                  pl.BlockSpec((1, D_MODEL), const2)],
        out_specs=(pact, pl.BlockSpec((N_SAMPLE, D_MODEL), const2)),
        scratch_shapes=[pltpu.VMEM((D_MODEL, D_MODEL), BF16), pltpu.VMEM((PLE_DIM, D_MODEL), BF16),
                        pltpu.SemaphoreType.DMA((2,))],
        compiler_params=pltpu.CompilerParams(
            dimension_semantics=("arbitrary",), vmem_limit_bytes=VMEM_LIMIT),
        name="final",
    )(h_all, ypairs, ypairs, ew, p_prompt, h_all, ypairs, ypairs, ew, p_sample, w_gate, w_proj, norm_ple, final_norm)


def kernel(x_prompt, x_sample, state_conv, p_prompt, p_sample, norm_mix, w_in, ln_v_g, ln_v_b, w_spatial,
           b_spatial, w_proj_a, conv_w, conv_b, ln_c_g, ln_c_b, w_proj_b, w_out, norm_ffn, w_router_group,
           b_router_group, w_router_expert, b_router_expert, w_exp_gate, w_exp_up, w_exp_down, norm_ple,
           w_ple_gate, w_ple_proj, final_norm):
    assert w_in.shape[0] == 1, "single layer"
    x_all = jnp.concatenate([x_prompt.reshape(N_PROMPT, D_MODEL), x_sample.reshape(N_SAMPLE, D_MODEL)], axis=0)
    vec = lambda v: v.reshape(1, -1)

    w_s, b_s = w_spatial[0], b_spatial[0]
    wmix = jnp.tril(w_s).astype(BF16)
    bmix = jnp.repeat(b_s.T, A_GROUP_DIM, axis=1)
    w00 = jnp.repeat(w_s[:, 0, 0], A_GROUP_DIM).reshape(1, D_MODEL)
    b00 = jnp.repeat(b_s[:, 0], A_GROUP_DIM).reshape(1, D_MODEL)

    gu, gv, glu, sga, sgb = _in_proj(x_all, vec(norm_mix[0]), w_in[0])

    cw = conv_w[0]
    new_conv_sample, conv_s = _sconv(state_conv[0], glu[N_PROMPT:], cw, vec(conv_b[0]))
    cw_pad = jnp.concatenate([cw, jnp.zeros((1, D_MODEL), F32)], axis=0)
    wi = jnp.arange(WIN_ROWS, dtype=jnp.int32)[None, :, None]
    wj = jnp.arange(WIN_PAD, dtype=jnp.int32)[None, None, :]
    wr = jnp.arange(1, SUBLANES, dtype=jnp.int32)[:, None, None]
    sel = (wj == wi + wr).astype(BF16)
    a_in, b_in, vn_s = _seqmix(gv, gu, glu, conv_s, wmix, bmix, w00, b00, vec(ln_v_g[0]), vec(ln_v_b[0]), cw_pad,
                               vec(conv_b[0]), vec(ln_c_g[0]), vec(ln_c_b[0]), sel)

    w_route = jnp.zeros((D_MODEL, LANES), F32)
    w_route = w_route.at[:, :MOE_GROUPS].set(w_router_group[0])
    w_route = w_route.at[:, MOE_GROUPS:MOE_GROUPS + N_EXPERTS].set(w_router_expert[0])
    b_route = jnp.zeros((1, LANES), F32)
    b_route = b_route.at[0, :MOE_GROUPS].set(b_router_group[0])
    b_route = b_route.at[0, MOE_GROUPS:MOE_GROUPS + N_EXPERTS].set(b_router_expert[0])

    wa = _cast_bf16(w_proj_a[0], 512)
    wb = _cast_bf16(w_proj_b[0], 512)
    wo = _cast_bf16(w_out[0], 512)
    h_all, eid, ew = _mixout(a_in, b_in, sga, sgb, x_all, wa, wb, wo, vec(norm_ffn[0]),
                             w_route.astype(BF16), b_route)

    tables = _dispatch_tables(eid[:, :TOP_K])
    ypairs = _experts(*tables, h_all, vec(norm_ffn[0]), w_exp_gate[0], w_exp_up[0], w_exp_down[0])

    wpg = _cast_bf16(w_ple_gate[0], 512)
    wpp = _cast_bf16(w_ple_proj[0], PLE_DIM)
    y_prompt, y_sample = _final(h_all, ypairs, ew, p_prompt[0].reshape(N_PROMPT, PLE_DIM),
                                p_sample[0].reshape(N_SAMPLE, PLE_DIM), wpg, wpp, vec(norm_ple[0]), vec(final_norm))
    y_prompt = y_prompt.reshape(N_PROMPT_SEQ, SEQ, D_MODEL)
    y_sample = y_sample.reshape(N_SAMPLE, 1, D_MODEL)
    new_conv_prompt = jnp.stack([glu[(b + 1) * SEQ - CONV_HIST:(b + 1) * SEQ] for b in range(N_PROMPT_SEQ)])[None]
    return (y_prompt, y_sample, new_conv_prompt, new_conv_sample[None], vn_s.reshape(1, N_SAMPLE, 1, D_MODEL))
```

```python
import jax
import jax.numpy as jnp
from jax import lax
from jax.experimental import pallas as pl
from jax.experimental.pallas import tpu as pltpu

F32 = jnp.float32
BF16 = jnp.bfloat16

D_MODEL = 2048
N_PROMPT_SEQ = 4
SEQ = 2048
N_PROMPT = N_PROMPT_SEQ * SEQ
N_SAMPLE = 128
N_ROWS = N_PROMPT + N_SAMPLE
CHUNK = 128
N_CHUNKS = N_ROWS // CHUNK
CHUNKS_PER_SEQ = SEQ // CHUNK
A_GROUPS = 8
A_GROUP_DIM = D_MODEL // A_GROUPS
CONV_WIDTH = 31
CONV_HIST = CONV_WIDTH - 1
HIST_BLOCK = 32
MOE_GROUPS = 4
EXPERTS_PER_GROUP = 8
N_EXPERTS = MOE_GROUPS * EXPERTS_PER_GROUP
TOP_K = 2
D_EXPERT = D_MODEL // 2
PLE_DIM = 256
EPS = 1e-6
LANES = 128

XN_TM = 512
IN_TM = 1664
IN_TN = 512
IN_COL_TILES = D_MODEL // IN_TN
ACT_BLOCKS = 5
ACT_GU, ACT_GV, ACT_GLU, ACT_SGA, ACT_SGB = range(ACT_BLOCKS)
MIX_TM = 208
FIN_TM = 256
FIN_PROMPT_STEPS = N_PROMPT // FIN_TM
ROW_BLOCK = 128
N_PAIRS = N_ROWS * TOP_K
MAX_BLOCKS = N_PAIRS // ROW_BLOCK + N_EXPERTS
ITEM_BLOCKS = 8
ITEM_ROWS = ITEM_BLOCKS * ROW_BLOCK
MAX_ITEMS = (MAX_BLOCKS + (ITEM_BLOCKS - 1) * N_EXPERTS) // ITEM_BLOCKS
N_SLOTS = MAX_BLOCKS * ROW_BLOCK
SLOT_TABLE = N_SLOTS + ITEM_ROWS
EXPERT_SLICES = 4
EXPERT_TN = D_EXPERT // EXPERT_SLICES
PAIR_REGION = 8448
DUMP_ROW0 = PAIR_REGION + N_ROWS
PAIR_ROWS = DUMP_ROW0 + ROW_BLOCK

VMEM_LIMIT = 56 * 1024 * 1024


def _rms(x, g):
    return x * lax.rsqrt(jnp.mean(x * x, axis=-1, keepdims=True) + EPS) * g


def _layer_norm(x, g, b):
    mu = jnp.mean(x, axis=-1, keepdims=True)
    xc = x - mu
    return xc * lax.rsqrt(jnp.mean(xc * xc, axis=-1, keepdims=True) + EPS) * g + b


def _sigmoid(x):
    return 1.0 / (1.0 + jnp.exp(-x))


def _silu(x):
    return x * _sigmoid(x)


def _gelu(x):
    return jax.nn.gelu(x, approximate=True)


def _bdot(a, b):
    return jnp.dot(a, b, preferred_element_type=F32)


def _cast_kernel(w_ref, o_ref):
    o_ref[...] = w_ref[...].astype(BF16)


def _cast_bf16(w, rows):
    k, n = w.shape
    return pl.pallas_call(
        _cast_kernel,
        out_shape=jax.ShapeDtypeStruct((k, n), BF16),
        grid=(k // rows,),
        in_specs=[pl.BlockSpec((rows, n), lambda i: (i, 0))],
        out_specs=pl.BlockSpec((rows, n), lambda i: (i, 0)),
        name="cast_bf16",
    )(w)


def _xnorm_kernel(xp_ref, xs_ref, nm_ref, xn_ref, xall_ref):
    i = pl.program_id(0)

    @pl.when(i < N_PROMPT // XN_TM)
    def _():
        x = xp_ref[...]
        xall_ref[...] = x
        xn_ref[...] = _rms(x, nm_ref[...]).astype(BF16)

    @pl.when(i == N_PROMPT // XN_TM)
    def _():
        x = xs_ref[...]
        xall_ref[0:N_SAMPLE, :] = x
        xall_ref[N_SAMPLE:, :] = jnp.zeros((XN_TM - N_SAMPLE, D_MODEL), F32)
        xn_ref[0:N_SAMPLE, :] = _rms(x, nm_ref[...]).astype(BF16)
        xn_ref[N_SAMPLE:, :] = jnp.zeros((XN_TM - N_SAMPLE, D_MODEL), BF16)


def _xnorm(x_prompt, x_sample, norm_mix):
    last = N_PROMPT // XN_TM - 1
    row = lambda i: (i, 0)
    return pl.pallas_call(
        _xnorm_kernel,
        out_shape=(jax.ShapeDtypeStruct((N_ROWS, D_MODEL), BF16), jax.ShapeDtypeStruct((N_ROWS, D_MODEL), F32)),
        grid=(N_PROMPT // XN_TM + 1,),
        in_specs=[pl.BlockSpec((XN_TM, D_MODEL), lambda i: (jnp.minimum(i, last), 0)),
                  pl.BlockSpec((N_SAMPLE, D_MODEL), lambda i: (0, 0)),
                  pl.BlockSpec((1, D_MODEL), lambda i: (0, 0))],
        out_specs=(pl.BlockSpec((XN_TM, D_MODEL), row), pl.BlockSpec((XN_TM, D_MODEL), row)),
        name="xnorm",
    )(x_prompt, x_sample, norm_mix)


def _in_w_col(j):
    t = IN_COL_TILES
    jj = j - 2 * t
    glu_col = 2 * t + (jj % 2) * t + jj // 2
    return jnp.where((j >= 2 * t) & (j < 4 * t), glu_col, j)


def _in_out_col(j):
    t = IN_COL_TILES
    return jnp.where(j < 2 * t, j, jnp.where(j < 4 * t, 2 * t + (j - 2 * t) // 2, j - t))


def _in_proj_kernel(xn_ref, w_ref, o_ref, val_ref):
    j = pl.program_id(1)
    t = IN_COL_TILES

    def z():
        return _bdot(xn_ref[...], w_ref[...].astype(BF16))

    @pl.when(j < 2 * t)
    def _():
        o_ref[...] = _gelu(z())

    @pl.when((j >= 2 * t) & (j < 4 * t) & (j % 2 == 0))
    def _():
        val_ref[...] = z()

    @pl.when((j >= 2 * t) & (j < 4 * t) & (j % 2 == 1))
    def _():
        o_ref[...] = val_ref[...] * _sigmoid(z())

    @pl.when(j >= 4 * t)
    def _():
        o_ref[...] = _sigmoid(z())


def _in_proj(xn, w_in):
    return pl.pallas_call(
        _in_proj_kernel,
        out_shape=jax.ShapeDtypeStruct((N_ROWS, ACT_BLOCKS * D_MODEL), F32),
        grid=(N_ROWS // IN_TM, 6 * IN_COL_TILES),
        in_specs=[
            pl.BlockSpec((IN_TM, D_MODEL), lambda i, j: (i, 0)),
            pl.BlockSpec((D_MODEL, IN_TN), lambda i, j: (0, _in_w_col(j))),
        ],
        out_specs=pl.BlockSpec((IN_TM, IN_TN), lambda i, j: (i, _in_out_col(j))),
        scratch_shapes=[pltpu.VMEM((IN_TM, IN_TN), F32)],
        compiler_params=pltpu.CompilerParams(
            dimension_semantics=("arbitrary", "arbitrary"), vmem_limit_bytes=VMEM_LIMIT),
        name="in_proj",
    )(xn, w_in)


SCONV_B = 8


def _sconv_kernel(st_ref, glu_ref, w_ref, wl_ref, cb_ref, new_ref, conv_ref):
    st = st_ref[0]
    glu = glu_ref[...]
    new_ref[0, :, 0:CONV_HIST - 1, :] = st[:, 1:CONV_HIST, :]
    new_ref[0, :, CONV_HIST - 1, :] = glu
    acc = st[:, 0, :] * w_ref[0:1, :]
    for k in range(1, CONV_HIST):
        acc = acc + st[:, k, :] * w_ref[k:k + 1, :]
    conv_ref[...] = acc + glu * wl_ref[...] + cb_ref[...]


def _sconv(state, act, conv_w, conv_b):
    return pl.pallas_call(
        _sconv_kernel,
        out_shape=(
            jax.ShapeDtypeStruct((1, N_SAMPLE, CONV_HIST, D_MODEL), F32),
            jax.ShapeDtypeStruct((N_SAMPLE, D_MODEL), F32),
        ),
        grid=(N_SAMPLE // SCONV_B,),
        in_specs=[
            pl.BlockSpec((1, SCONV_B, CONV_HIST, D_MODEL), lambda i: (0, i, 0, 0)),
            pl.BlockSpec((SCONV_B, D_MODEL), lambda i: (N_PROMPT // SCONV_B + i, ACT_GLU)),
            pl.BlockSpec((CONV_HIST, D_MODEL), lambda i: (0, 0)),
            pl.BlockSpec((1, D_MODEL), lambda i: (0, 0)),
            pl.BlockSpec((1, D_MODEL), lambda i: (0, 0)),
        ],
        out_specs=(
            pl.BlockSpec((1, SCONV_B, CONV_HIST, D_MODEL), lambda i: (0, i, 0, 0)),
            pl.BlockSpec((SCONV_B, D_MODEL), lambda i: (i, 0)),
        ),
        name="sconv",
    )(state, act, conv_w[:CONV_HIST], conv_w[CONV_HIST:], conv_b)


CONV_STRIP = 256
SUBLANES = 8
WIN_ROWS = HIST_BLOCK + CHUNK
WIN_PAD = 256


def _seqmix_kernel(gv_ref, gu_ref, cur_ref, prev_ref, convs_ref, wmix_ref, bmix_ref, w00_ref, b00_ref,
                   lvg_ref, lvb_ref, cw_ref, cb_ref, lcg_ref, lcb_ref, sel_ref, a_ref, b_ref, vns_ref,
                   win_ref, shift_ref, conv_ref):
    c = pl.program_id(0)
    is_sample = c == N_CHUNKS - 1

    vn = _layer_norm(gv_ref[...], lvg_ref[...], lvb_ref[...])

    @pl.when(is_sample)
    def _():
        vns_ref[...] = vn

    vnb = vn.astype(BF16)
    for g in range(A_GROUPS):
        sl = slice(g * A_GROUP_DIM, (g + 1) * A_GROUP_DIM)
        s_chunk = _bdot(wmix_ref[g], vnb[:, sl]) + bmix_ref[:, sl]
        s_first = vn[:, sl] * w00_ref[:, sl] + b00_ref[:, sl]
        a_ref[:, sl] = (gu_ref[:, sl] * jnp.where(is_sample, s_first, s_chunk)).astype(BF16)

    fresh = (c % CHUNKS_PER_SEQ) == 0

    @pl.when(c == 0)
    def _():
        win_ref[WIN_ROWS:, :] = jnp.zeros((WIN_PAD - WIN_ROWS, D_MODEL), BF16)

    win_ref[0:HIST_BLOCK, :] = jnp.where(fresh, 0.0, prev_ref[...]).astype(BF16)
    win_ref[HIST_BLOCK:WIN_ROWS, :] = cur_ref[...].astype(BF16)
    shift_ref[0] = win_ref[0:WIN_ROWS, :].astype(F32)
    for r in range(1, SUBLANES):
        shift_ref[r] = _bdot(sel_ref[r - 1], win_ref[...])
    off = HIST_BLOCK - CONV_HIST

    def tap(k, cs):
        q, r = divmod(off + k, SUBLANES)
        return (shift_ref[r, SUBLANES * q:SUBLANES * q + CHUNK, cs]
                * cw_ref[k:k + 1, cs].astype(BF16).astype(F32))

    for s0 in range(0, D_MODEL, CONV_STRIP):
        cs = slice(s0, s0 + CONV_STRIP)
        acc = tap(0, cs)
        for k in range(1, CONV_WIDTH):
            acc = acc + tap(k, cs)
        conv_ref[:, cs] = acc + cb_ref[:, cs]
    conv = jnp.where(is_sample, convs_ref[...], conv_ref[...])
    b_ref[...] = _silu(_layer_norm(conv, lcg_ref[...], lcb_ref[...])).astype(BF16)


def _seqmix(act, conv_s, wmix, bmix, w00, b00, ln_v_g, ln_v_b, conv_w, conv_b, ln_c_g, ln_c_b, sel):
    row = lambda c: (c, 0)
    const2 = lambda c: (0, 0)
    vec = pl.BlockSpec((1, D_MODEL), const2)
    hist_per_chunk = CHUNK // HIST_BLOCK
    act_block = lambda blk: pl.BlockSpec((CHUNK, D_MODEL), lambda c: (c, blk))
    return pl.pallas_call(
        _seqmix_kernel,
        out_shape=(
            jax.ShapeDtypeStruct((N_ROWS, D_MODEL), BF16),
            jax.ShapeDtypeStruct((N_ROWS, D_MODEL), BF16),
            jax.ShapeDtypeStruct((N_SAMPLE, D_MODEL), F32),
        ),
        grid=(N_CHUNKS,),
        in_specs=[
            act_block(ACT_GV), act_block(ACT_GU), act_block(ACT_GLU),
            pl.BlockSpec((HIST_BLOCK, D_MODEL), lambda c: (jnp.maximum(c * hist_per_chunk - 1, 0), ACT_GLU)),
            pl.BlockSpec((N_SAMPLE, D_MODEL), const2),
            pl.BlockSpec((A_GROUPS, CHUNK, CHUNK), lambda c: (0, 0, 0)),
            pl.BlockSpec((CHUNK, D_MODEL), const2),
            vec, vec, vec, vec,
            pl.BlockSpec((CONV_WIDTH + 1, D_MODEL), const2),
            vec, vec, vec,
            pl.BlockSpec((SUBLANES - 1, WIN_ROWS, WIN_PAD), lambda c: (0, 0, 0)),
        ],
        out_specs=(
            pl.BlockSpec((CHUNK, D_MODEL), row),
            pl.BlockSpec((CHUNK, D_MODEL), row),
            pl.BlockSpec((N_SAMPLE, D_MODEL), const2),
        ),
        scratch_shapes=[pltpu.VMEM((WIN_PAD, D_MODEL), BF16), pltpu.VMEM((SUBLANES, WIN_ROWS, D_MODEL), F32),
                        pltpu.VMEM((CHUNK, D_MODEL), F32)],
        compiler_params=pltpu.CompilerParams(
            dimension_semantics=("arbitrary",), vmem_limit_bytes=VMEM_LIMIT),
        name="seqmix",
    )(act, act, act, act, conv_s, wmix, bmix, w00, b00, ln_v_g, ln_v_b, conv_w, conv_b, ln_c_g, ln_c_b, sel)


def _route(logits):
    col = lax.broadcasted_iota(jnp.int32, logits.shape, 1).astype(F32)
    neg = jnp.float32(-jnp.inf)
    big = jnp.float32(1e9)
    lg = jnp.where(col < MOE_GROUPS, logits, neg)
    gmax = jnp.max(lg, axis=-1, keepdims=True)
    gi = jnp.min(jnp.where(lg == gmax, col, big), axis=-1, keepdims=True)
    gw = 1.0 / jnp.sum(jnp.exp(lg - gmax), axis=-1, keepdims=True)
    lo = MOE_GROUPS + gi * EXPERTS_PER_GROUP
    le = jnp.where((col >= lo) & (col < lo + EXPERTS_PER_GROUP), logits, neg)
    m1 = jnp.max(le, axis=-1, keepdims=True)
    i1 = jnp.min(jnp.where(le == m1, col, big), axis=-1, keepdims=True)
    le2 = jnp.where(col == i1, neg, le)
    m2 = jnp.max(le2, axis=-1, keepdims=True)
    i2 = jnp.min(jnp.where(le2 == m2, col, big), axis=-1, keepdims=True)
    e = jnp.exp(m2 - m1)
    w1 = gw / (1.0 + e)
    w2 = gw * e / (1.0 + e)
    eid = jnp.where(col == 0, i1 - MOE_GROUPS, jnp.where(col == 1, i2 - MOE_GROUPS, 0.0)).astype(jnp.int32)
    ew = jnp.where(col == 0, w1, jnp.where(col == 1, w2, 0.0))
    return eid, ew


def _mixout_kernel(a_ref, b_ref, sga_ref, sgb_ref, x_ref, wa_hbm, wb_hbm, wo_hbm, nf_ref, wr_ref, br_ref,
                   h_ref, eid_ref, ew_ref, wa_ref, wb_ref, wo_ref, sem):
    @pl.when(pl.program_id(0) == 0)
    def _():
        copies = [pltpu.make_async_copy(src, dst, sem.at[k])
                  for k, (src, dst) in enumerate(((wa_hbm, wa_ref), (wb_hbm, wb_ref), (wo_hbm, wo_ref)))]
        for cp in copies:
            cp.start()
        for cp in copies:
            cp.wait()

    pa = _bdot(a_ref[...], wa_ref[...])
    pb = _bdot(b_ref[...], wb_ref[...])
    m = (sga_ref[...] * pa + sgb_ref[...] * pb).astype(BF16)
    h = x_ref[...] + _bdot(m, wo_ref[...])
    h_ref[...] = h
    hn = _rms(h, nf_ref[...])
    logits = _bdot(hn.astype(BF16), wr_ref[...]) + br_ref[...]
    eid, ew = _route(logits)
    eid_ref[...] = eid
    ew_ref[...] = ew


def _mixout(a_in, b_in, act, x_all, wa, wb, wo, norm_ffn, w_route, b_route):
    row = lambda i: (i, 0)
    const2 = lambda i: (0, 0)
    rows = pl.BlockSpec((MIX_TM, D_MODEL), row)
    hbm = pl.BlockSpec(memory_space=pltpu.MemorySpace.HBM)
    return pl.pallas_call(
        _mixout_kernel,
        out_shape=(
            jax.ShapeDtypeStruct((N_ROWS, D_MODEL), F32),
            jax.ShapeDtypeStruct((N_ROWS, LANES), jnp.int32),
            jax.ShapeDtypeStruct((N_ROWS, LANES), F32),
        ),
        grid=(N_ROWS // MIX_TM,),
        in_specs=[rows, rows, pl.BlockSpec((MIX_TM, D_MODEL), lambda i: (i, ACT_SGA)),
                  pl.BlockSpec((MIX_TM, D_MODEL), lambda i: (i, ACT_SGB)), rows, hbm, hbm, hbm,
                  pl.BlockSpec((1, D_MODEL), const2),
                  pl.BlockSpec((D_MODEL, LANES), const2),
                  pl.BlockSpec((1, LANES), const2)],
        out_specs=(rows, pl.BlockSpec((MIX_TM, LANES), row), pl.BlockSpec((MIX_TM, LANES), row)),
        scratch_shapes=[pltpu.VMEM((D_MODEL, D_MODEL), BF16)] * 3 + [pltpu.SemaphoreType.DMA((3,))],
        compiler_params=pltpu.CompilerParams(
            dimension_semantics=("arbitrary",), vmem_limit_bytes=VMEM_LIMIT),
        name="mixout",
    )(a_in, b_in, act, act, x_all, wa, wb, wo, norm_ffn, w_route, b_route)


def _experts_kernel(item_e, item_blk0, item_nb, src_hbm, dst_hbm, h_hbm, nf_ref, wg_ref, wu_ref, wd_ref,
                    y_hbm, src_s, dst_s, xg_ref, xs_ref, acc_ref, wgb_ref, wub_ref, wdb_ref,
                    idx_sem, g_sem, s_sem):
    w = pl.program_id(0)
    s = pl.program_id(1)
    nb = item_nb[w]
    slot0 = pl.multiple_of(item_blk0[w] * ROW_BLOCK, ROW_BLOCK)

    def row_copy_in(r, blk):
        return pltpu.make_async_copy(h_hbm.at[pl.ds(src_s[r], 1)], xg_ref.at[pl.ds(r, 1)], g_sem.at[blk])

    def row_copy_out(r, blk):
        return pltpu.make_async_copy(xg_ref.at[pl.ds(r, 1)], y_hbm.at[pl.ds(dst_s[r], 1)], s_sem.at[blk])

    def block_rows(blk):
        return pl.ds(pl.multiple_of(blk * ROW_BLOCK, ROW_BLOCK), ROW_BLOCK)

    def block_copy_wait(sem, blk):
        pltpu.make_async_copy(xg_ref.at[block_rows(blk)], xg_ref.at[block_rows(blk)], sem.at[blk]).wait()

    @pl.when((w == 0) & (s == 0))
    def _():
        pad_rows = pl.ds(0, ROW_BLOCK)
        xg_ref[pad_rows, :] = jnp.zeros((ROW_BLOCK, D_MODEL), F32)
        for row0 in (N_ROWS, DUMP_ROW0):
            cz = pltpu.make_async_copy(xg_ref.at[pad_rows], y_hbm.at[pl.ds(row0, ROW_BLOCK)], idx_sem.at[0])
            cz.start()
            cz.wait()

    @pl.when((s == 0) & (nb > 0))
    def _():
        ci = pltpu.make_async_copy(src_hbm.at[pl.ds(slot0, ITEM_ROWS)], src_s, idx_sem.at[0])
        co = pltpu.make_async_copy(dst_hbm.at[pl.ds(slot0, ITEM_ROWS)], dst_s, idx_sem.at[1])
        ci.start()
        co.start()
        ci.wait()
        co.wait()

        def issue(blk, carry):
            def one(i, carry2):
                row_copy_in(blk * ROW_BLOCK + i, blk).start()
                return carry2
            return lax.fori_loop(0, ROW_BLOCK, one, carry, unroll=8)
        lax.fori_loop(0, nb, issue, 0)

        def land(blk, carry):
            block_copy_wait(g_sem, blk)
            xs_ref[block_rows(blk), :] = _rms(xg_ref[block_rows(blk), :], nf_ref[...]).astype(BF16)
            return carry
        lax.fori_loop(0, nb, land, 0)

    @pl.when(nb > 0)
    def _():
        wgb_ref[...] = wg_ref[...].astype(BF16)
        wub_ref[...] = wu_ref[...].astype(BF16)
        wdb_ref[...] = wd_ref[...].astype(BF16)

        def compute(row0, nrows):
            rows = pl.ds(row0, nrows)
            x = xs_ref[rows, :]
            hm = (_silu(_bdot(x, wgb_ref[...])) * _bdot(x, wub_ref[...])).astype(BF16)
            y = _bdot(hm, wdb_ref[...])

            @pl.when(s == 0)
            def _():
                acc_ref[rows, :] = y

            @pl.when(s != 0)
            def _():
                acc_ref[rows, :] += y

        def four_blocks(j, carry):
            compute(pl.multiple_of(j * (4 * ROW_BLOCK), 4 * ROW_BLOCK), 4 * ROW_BLOCK)
            return carry
        lax.fori_loop(0, nb >> 2, four_blocks, 0)

        @pl.when((nb & 2) == 2)
        def _():
            compute(pl.multiple_of((nb >> 2) * (4 * ROW_BLOCK), 2 * ROW_BLOCK), 2 * ROW_BLOCK)

        @pl.when((nb & 1) == 1)
        def _():
            compute(pl.multiple_of((nb - 1) * ROW_BLOCK, ROW_BLOCK), ROW_BLOCK)

    @pl.when((s == EXPERT_SLICES - 1) & (nb > 0))
    def _():
        def send(blk, carry):
            xg_ref[block_rows(blk), :] = acc_ref[block_rows(blk), :]

            def one(i, carry2):
                row_copy_out(blk * ROW_BLOCK + i, blk).start()
                return carry2
            return lax.fori_loop(0, ROW_BLOCK, one, carry, unroll=8)
        lax.fori_loop(0, nb, send, 0)

        def drain(blk, carry):
            block_copy_wait(s_sem, blk)
            return carry
        lax.fori_loop(0, nb, drain, 0)


def _experts(item_e, item_blk0, item_nb, src_row, dst_row, h_all, norm_ffn, w_gate, w_up, w_down):
    hbm = pl.BlockSpec(memory_space=pltpu.MemorySpace.HBM)

    def w_slice(w, s, inb):
        return jnp.where(inb[w] > 0, s, EXPERT_SLICES - 1)

    grid_spec = pltpu.PrefetchScalarGridSpec(
        num_scalar_prefetch=3,
        grid=(MAX_ITEMS, EXPERT_SLICES),
        in_specs=[
            hbm, hbm, hbm,
            pl.BlockSpec((1, D_MODEL), lambda w, s, ie, ib, inb: (0, 0)),
            pl.BlockSpec((None, D_MODEL, EXPERT_TN), lambda w, s, ie, ib, inb: (ie[w], 0, w_slice(w, s, inb))),
            pl.BlockSpec((None, D_MODEL, EXPERT_TN), lambda w, s, ie, ib, inb: (ie[w], 0, w_slice(w, s, inb))),
            pl.BlockSpec((None, EXPERT_TN, D_MODEL), lambda w, s, ie, ib, inb: (ie[w], w_slice(w, s, inb), 0)),
        ],
        out_specs=hbm,
        scratch_shapes=[
            pltpu.SMEM((ITEM_ROWS,), jnp.int32),
            pltpu.SMEM((ITEM_ROWS,), jnp.int32),
            pltpu.VMEM((ITEM_ROWS, D_MODEL), F32),
            pltpu.VMEM((ITEM_ROWS, D_MODEL), BF16),
            pltpu.VMEM((ITEM_ROWS, D_MODEL), F32),
            pltpu.VMEM((D_MODEL, EXPERT_TN), BF16),
            pltpu.VMEM((D_MODEL, EXPERT_TN), BF16),
            pltpu.VMEM((EXPERT_TN, D_MODEL), BF16),
            pltpu.SemaphoreType.DMA((2,)),
            pltpu.SemaphoreType.DMA((ITEM_BLOCKS,)),
            pltpu.SemaphoreType.DMA((ITEM_BLOCKS,)),
        ],
    )
    return pl.pallas_call(
        _experts_kernel,
        out_shape=jax.ShapeDtypeStruct((PAIR_ROWS, D_MODEL), F32),
        grid_spec=grid_spec,
        compiler_params=pltpu.CompilerParams(
            dimension_semantics=("arbitrary", "arbitrary"), vmem_limit_bytes=VMEM_LIMIT),
        name="experts",
    )(item_e, item_blk0, item_nb, src_row, dst_row, h_all, norm_ffn, w_gate, w_up, w_down)


def _dispatch_tables(eid):
    flat_e = eid.reshape(-1)
    onehot = (flat_e[:, None] == jnp.arange(N_EXPERTS, dtype=jnp.int32)[None, :]).astype(jnp.int32)
    rank = jnp.take_along_axis(jnp.cumsum(onehot, axis=0) - onehot, flat_e[:, None], axis=1)[:, 0]
    counts = jnp.sum(onehot, axis=0)
    nblk = (counts + ROW_BLOCK - 1) // ROW_BLOCK
    blk0 = jnp.cumsum(nblk) - nblk
    slot = blk0[flat_e] * ROW_BLOCK + rank
    slot_pair = jnp.full((SLOT_TABLE,), N_PAIRS, jnp.int32).at[slot].set(jnp.arange(N_PAIRS, dtype=jnp.int32))
    slot_id = jnp.arange(SLOT_TABLE, dtype=jnp.int32)
    src_row = jnp.minimum(slot_pair // TOP_K, N_ROWS - 1)
    dst_row = jnp.where(slot_pair < N_PAIRS, (slot_pair % TOP_K) * PAIR_REGION + slot_pair // TOP_K,
                        DUMP_ROW0 + slot_id % ROW_BLOCK)
    nitem = (nblk + ITEM_BLOCKS - 1) // ITEM_BLOCKS
    item_end = jnp.cumsum(nitem)
    total = item_end[-1]
    idx = jnp.arange(MAX_ITEMS, dtype=jnp.int32)
    live = idx < total
    e_of = jnp.minimum(jnp.searchsorted(item_end, jnp.minimum(idx, total - 1), side='right'),
                       N_EXPERTS - 1).astype(jnp.int32)
    local = jnp.minimum(idx, total - 1) - (item_end - nitem)[e_of]
    item_blk0 = (blk0[e_of] + local * ITEM_BLOCKS).astype(jnp.int32)
    item_nb = jnp.where(live, jnp.clip(nblk[e_of] - local * ITEM_BLOCKS, 0, ITEM_BLOCKS), 0).astype(jnp.int32)
    return e_of, item_blk0, item_nb, src_row, dst_row


def _final_rows(h, y0, y1, ew, p, wg_ref, wp_ref, np_ref, fn_ref):
    h = h + (ew[:, 0:1] * y0 + ew[:, 1:2] * y1)
    gate = _sigmoid(_bdot(_rms(h, np_ref[...]).astype(BF16), wg_ref[...]))
    h = h + gate * _bdot(p.astype(BF16), wp_ref[...])
    return _rms(h, fn_ref[...])


def _final_kernel(h_ref, y0_ref, y1_ref, ew_ref, p_ref, hs_ref, y0s_ref, y1s_ref, ews_ref, ps_ref,
                  wg_hbm, wp_hbm, np_ref, fn_ref, op_ref, os_ref, wg_ref, wp_ref, sem):
    i = pl.program_id(0)

    @pl.when(i == 0)
    def _():
        c0 = pltpu.make_async_copy(wg_hbm, wg_ref, sem.at[0])
        c1 = pltpu.make_async_copy(wp_hbm, wp_ref, sem.at[1])
        c0.start()
        c1.start()
        c0.wait()
        c1.wait()

    @pl.when(i < FIN_PROMPT_STEPS)
    def _():
        op_ref[...] = _final_rows(h_ref[...], y0_ref[...], y1_ref[...], ew_ref[...], p_ref[...],
                                  wg_ref, wp_ref, np_ref, fn_ref)

    @pl.when(i == FIN_PROMPT_STEPS)
    def _():
        os_ref[...] = _final_rows(hs_ref[...], y0s_ref[...], y1s_ref[...], ews_ref[...], ps_ref[...],
                                  wg_ref, wp_ref, np_ref, fn_ref)


def _final(h_all, ypairs, ew, p_prompt, p_sample, w_gate, w_proj, norm_ple, final_norm):
    last = FIN_PROMPT_STEPS - 1
    prow = lambda i: (jnp.minimum(i, last), 0)
    const2 = lambda i: (0, 0)
    hbm = pl.BlockSpec(memory_space=pltpu.MemorySpace.HBM)
    sblk = N_PROMPT // N_SAMPLE
    pact = pl.BlockSpec((FIN_TM, D_MODEL), prow)
    sact = lambda off: pl.BlockSpec((N_SAMPLE, D_MODEL), lambda i: (off + sblk, 0))
    return pl.pallas_call(
        _final_kernel,
        out_shape=(jax.ShapeDtypeStruct((N_PROMPT, D_MODEL), F32), jax.ShapeDtypeStruct((N_SAMPLE, D_MODEL), F32)),
        grid=(FIN_PROMPT_STEPS + 1,),
        in_specs=[pact, pact,
                  pl.BlockSpec((FIN_TM, D_MODEL), lambda i: (jnp.minimum(i, last) + PAIR_REGION // FIN_TM, 0)),
                  pl.BlockSpec((FIN_TM, LANES), prow),
                  pl.BlockSpec((FIN_TM, PLE_DIM), prow),
                  sact(0), sact(0), sact(PAIR_REGION // N_SAMPLE),
                  pl.BlockSpec((N_SAMPLE, LANES), lambda i: (sblk, 0)),
                  pl.BlockSpec((N_SAMPLE, PLE_DIM), const2),
                  hbm, hbm,
                  pl.BlockSpec((1, D_MODEL), const2),
                  pl.BlockSpec((1, D_MODEL), const2)],
        out_specs=(pact, pl.BlockSpec((N_SAMPLE, D_MODEL), const2)),
        scratch_shapes=[pltpu.VMEM((D_MODEL, D_MODEL), BF16), pltpu.VMEM((PLE_DIM, D_MODEL), BF16),
                        pltpu.SemaphoreType.DMA((2,))],
        compiler_params=pltpu.CompilerParams(
            dimension_semantics=("arbitrary",), vmem_limit_bytes=VMEM_LIMIT),
        name="final",
    )(h_all, ypairs, ypairs, ew, p_prompt, h_all, ypairs, ypairs, ew, p_sample, w_gate, w_proj, norm_ple, final_norm)


def kernel(x_prompt, x_sample, state_conv, p_prompt, p_sample, norm_mix, w_in, ln_v_g, ln_v_b, w_spatial,
           b_spatial, w_proj_a, conv_w, conv_b, ln_c_g, ln_c_b, w_proj_b, w_out, norm_ffn, w_router_group,
           b_router_group, w_router_expert, b_router_expert, w_exp_gate, w_exp_up, w_exp_down, norm_ple,
           w_ple_gate, w_ple_proj, final_norm):
    assert w_in.shape[0] == 1, "single layer"
    vec = lambda v: v.reshape(1, -1)
    xn, x_all = _xnorm(x_prompt.reshape(N_PROMPT, D_MODEL), x_sample.reshape(N_SAMPLE, D_MODEL), vec(norm_mix[0]))

    w_s, b_s = w_spatial[0], b_spatial[0]
    wmix = jnp.tril(w_s).astype(BF16)
    bmix = jnp.repeat(b_s.T, A_GROUP_DIM, axis=1)
    w00 = jnp.repeat(w_s[:, 0, 0], A_GROUP_DIM).reshape(1, D_MODEL)
    b00 = jnp.repeat(b_s[:, 0], A_GROUP_DIM).reshape(1, D_MODEL)

    act = _in_proj(xn, w_in[0])

    cw = conv_w[0]
    new_conv_sample, conv_s = _sconv(state_conv, act, cw, vec(conv_b[0]))
    cw_pad = jnp.concatenate([cw, jnp.zeros((1, D_MODEL), F32)], axis=0)
    wi = jnp.arange(WIN_ROWS, dtype=jnp.int32)[None, :, None]
    wj = jnp.arange(WIN_PAD, dtype=jnp.int32)[None, None, :]
    wr = jnp.arange(1, SUBLANES, dtype=jnp.int32)[:, None, None]
    sel = (wj == wi + wr).astype(BF16)
    a_in, b_in, vn_s = _seqmix(act, conv_s, wmix, bmix, w00, b00, vec(ln_v_g[0]), vec(ln_v_b[0]), cw_pad,
                               vec(conv_b[0]), vec(ln_c_g[0]), vec(ln_c_b[0]), sel)

    w_route = jnp.zeros((D_MODEL, LANES), F32)
    w_route = w_route.at[:, :MOE_GROUPS].set(w_router_group[0])
    w_route = w_route.at[:, MOE_GROUPS:MOE_GROUPS + N_EXPERTS].set(w_router_expert[0])
    b_route = jnp.zeros((1, LANES), F32)
    b_route = b_route.at[0, :MOE_GROUPS].set(b_router_group[0])
    b_route = b_route.at[0, MOE_GROUPS:MOE_GROUPS + N_EXPERTS].set(b_router_expert[0])

    wa = _cast_bf16(w_proj_a[0], 512)
    wb = _cast_bf16(w_proj_b[0], 512)
    wo = _cast_bf16(w_out[0], 512)
    h_all, eid, ew = _mixout(a_in, b_in, act, x_all, wa, wb, wo, vec(norm_ffn[0]),
                             w_route.astype(BF16), b_route)

    tables = _dispatch_tables(eid[:, :TOP_K])
    ypairs = _experts(*tables, h_all, vec(norm_ffn[0]), w_exp_gate[0], w_exp_up[0], w_exp_down[0])

    wpg = _cast_bf16(w_ple_gate[0], 512)
    wpp = _cast_bf16(w_ple_proj[0], PLE_DIM)
    y_prompt, y_sample = _final(h_all, ypairs, ew, p_prompt[0].reshape(N_PROMPT, PLE_DIM),
                                p_sample[0].reshape(N_SAMPLE, PLE_DIM), wpg, wpp, vec(norm_ple[0]), vec(final_norm))
    y_prompt = y_prompt.reshape(N_PROMPT_SEQ, SEQ, D_MODEL)
    y_sample = y_sample.reshape(N_SAMPLE, 1, D_MODEL)
    glu_cols = slice(ACT_GLU * D_MODEL, (ACT_GLU + 1) * D_MODEL)
    new_conv_prompt = jnp.stack([act[(b + 1) * SEQ - CONV_HIST:(b + 1) * SEQ, glu_cols]
                                 for b in range(N_PROMPT_SEQ)])[None]
    return (y_prompt, y_sample, new_conv_prompt, new_conv_sample, vn_s.reshape(1, N_SAMPLE, 1, D_MODEL))
```

```python
import jax
import jax.numpy as jnp
from jax import lax
from jax.experimental import pallas as pl
from jax.experimental.pallas import tpu as pltpu

F32 = jnp.float32
BF16 = jnp.bfloat16

D_MODEL = 2048
N_PROMPT_SEQ = 4
SEQ = 2048
N_PROMPT = N_PROMPT_SEQ * SEQ
N_SAMPLE = 128
N_ROWS = N_PROMPT + N_SAMPLE
CHUNK = 128
N_CHUNKS = N_ROWS // CHUNK
CHUNKS_PER_SEQ = SEQ // CHUNK
A_GROUPS = 8
A_GROUP_DIM = D_MODEL // A_GROUPS
CONV_WIDTH = 31
CONV_HIST = CONV_WIDTH - 1
HIST_BLOCK = 32
MOE_GROUPS = 4
EXPERTS_PER_GROUP = 8
N_EXPERTS = MOE_GROUPS * EXPERTS_PER_GROUP
TOP_K = 2
D_EXPERT = D_MODEL // 2
PLE_DIM = 256
EPS = 1e-6
LANES = 128

XN_TM = 512
IN_TM = 1664
IN_TN = 512
IN_COL_TILES = D_MODEL // IN_TN
ACT_BLOCKS = 5
ACT_GU, ACT_GV, ACT_GLU, ACT_SGA, ACT_SGB = range(ACT_BLOCKS)
MIX_TM = 208
FIN_TM = 256
FIN_PROMPT_STEPS = N_PROMPT // FIN_TM
ROW_BLOCK = 128
N_PAIRS = N_ROWS * TOP_K
MAX_BLOCKS = N_PAIRS // ROW_BLOCK + N_EXPERTS
ITEM_BLOCKS = 8
ITEM_ROWS = ITEM_BLOCKS * ROW_BLOCK
MAX_ITEMS = (MAX_BLOCKS + (ITEM_BLOCKS - 1) * N_EXPERTS) // ITEM_BLOCKS
N_SLOTS = MAX_BLOCKS * ROW_BLOCK
SLOT_TABLE = N_SLOTS + ITEM_ROWS
EXPERT_SLICES = 4
EXPERT_TN = D_EXPERT // EXPERT_SLICES
PAIR_REGION = 8448
DUMP_ROW0 = PAIR_REGION + N_ROWS
PAIR_ROWS = DUMP_ROW0 + ROW_BLOCK

VMEM_LIMIT = 56 * 1024 * 1024


def _rms(x, g):
    return x * lax.rsqrt(jnp.mean(x * x, axis=-1, keepdims=True) + EPS) * g


def _layer_norm(x, g, b):
    mu = jnp.mean(x, axis=-1, keepdims=True)
    xc = x - mu
    return xc * lax.rsqrt(jnp.mean(xc * xc, axis=-1, keepdims=True) + EPS) * g + b


def _sigmoid(x):
    return 1.0 / (1.0 + jnp.exp(-x))


def _silu(x):
    return x * _sigmoid(x)


def _gelu(x):
    return jax.nn.gelu(x, approximate=True)


def _bdot(a, b):
    return jnp.dot(a, b, preferred_element_type=F32)


def _cast_kernel(w_ref, o_ref):
    o_ref[...] = w_ref[...].astype(BF16)


def _cast_bf16(w, rows):
    k, n = w.shape
    return pl.pallas_call(
        _cast_kernel,
        out_shape=jax.ShapeDtypeStruct((k, n), BF16),
        grid=(k // rows,),
        in_specs=[pl.BlockSpec((rows, n), lambda i: (i, 0))],
        out_specs=pl.BlockSpec((rows, n), lambda i: (i, 0)),
        name="cast_bf16",
    )(w)


def _xnorm_kernel(xp_ref, xs_ref, nm_ref, xn_ref, xall_ref):
    i = pl.program_id(0)

    @pl.when(i < N_PROMPT // XN_TM)
    def _():
        x = xp_ref[...]
        xall_ref[...] = x
        xn_ref[...] = _rms(x, nm_ref[...]).astype(BF16)

    @pl.when(i == N_PROMPT // XN_TM)
    def _():
        x = xs_ref[...]
        xall_ref[0:N_SAMPLE, :] = x
        xall_ref[N_SAMPLE:, :] = jnp.zeros((XN_TM - N_SAMPLE, D_MODEL), F32)
        xn_ref[0:N_SAMPLE, :] = _rms(x, nm_ref[...]).astype(BF16)
        xn_ref[N_SAMPLE:, :] = jnp.zeros((XN_TM - N_SAMPLE, D_MODEL), BF16)


def _xnorm(x_prompt, x_sample, norm_mix):
    last = N_PROMPT // XN_TM - 1
    row = lambda i: (i, 0)
    return pl.pallas_call(
        _xnorm_kernel,
        out_shape=(jax.ShapeDtypeStruct((N_ROWS, D_MODEL), BF16), jax.ShapeDtypeStruct((N_ROWS, D_MODEL), F32)),
        grid=(N_PROMPT // XN_TM + 1,),
        in_specs=[pl.BlockSpec((XN_TM, D_MODEL), lambda i: (jnp.minimum(i, last), 0)),
                  pl.BlockSpec((N_SAMPLE, D_MODEL), lambda i: (0, 0)),
                  pl.BlockSpec((1, D_MODEL), lambda i: (0, 0))],
        out_specs=(pl.BlockSpec((XN_TM, D_MODEL), row), pl.BlockSpec((XN_TM, D_MODEL), row)),
        name="xnorm",
    )(x_prompt, x_sample, norm_mix)


def _in_w_col(j):
    t = IN_COL_TILES
    jj = j - 2 * t
    glu_col = 2 * t + (jj % 2) * t + jj // 2
    return jnp.where((j >= 2 * t) & (j < 4 * t), glu_col, j)


def _in_out_col(j):
    t = IN_COL_TILES
    return jnp.where(j < 2 * t, j, jnp.where(j < 4 * t, 2 * t + (j - 2 * t) // 2, j - t))


def _in_proj_kernel(xn_ref, w_ref, o_ref, val_ref):
    j = pl.program_id(1)
    t = IN_COL_TILES

    def z():
        return _bdot(xn_ref[...], w_ref[...].astype(BF16))

    @pl.when(j < 2 * t)
    def _():
        o_ref[...] = _gelu(z())

    @pl.when((j >= 2 * t) & (j < 4 * t) & (j % 2 == 0))
    def _():
        val_ref[...] = z()

    @pl.when((j >= 2 * t) & (j < 4 * t) & (j % 2 == 1))
    def _():
        o_ref[...] = val_ref[...] * _sigmoid(z())

    @pl.when(j >= 4 * t)
    def _():
        o_ref[...] = _sigmoid(z())


def _in_proj(xn, w_in):
    return pl.pallas_call(
        _in_proj_kernel,
        out_shape=jax.ShapeDtypeStruct((N_ROWS, ACT_BLOCKS * D_MODEL), F32),
        grid=(N_ROWS // IN_TM, 6 * IN_COL_TILES),
        in_specs=[
            pl.BlockSpec((IN_TM, D_MODEL), lambda i, j: (i, 0)),
            pl.BlockSpec((D_MODEL, IN_TN), lambda i, j: (0, _in_w_col(j))),
        ],
        out_specs=pl.BlockSpec((IN_TM, IN_TN), lambda i, j: (i, _in_out_col(j))),
        scratch_shapes=[pltpu.VMEM((IN_TM, IN_TN), F32)],
        compiler_params=pltpu.CompilerParams(
            dimension_semantics=("arbitrary", "arbitrary"), vmem_limit_bytes=VMEM_LIMIT),
        name="in_proj",
    )(xn, w_in)


SCONV_B = 8


def _sconv_kernel(st_ref, glu_ref, w_ref, wl_ref, cb_ref, new_ref, conv_ref):
    st = st_ref[0]
    glu = glu_ref[...]
    new_ref[0, :, 0:CONV_HIST - 1, :] = st[:, 1:CONV_HIST, :]
    new_ref[0, :, CONV_HIST - 1, :] = glu
    acc = st[:, 0, :] * w_ref[0:1, :]
    for k in range(1, CONV_HIST):
        acc = acc + st[:, k, :] * w_ref[k:k + 1, :]
    conv_ref[...] = acc + glu * wl_ref[...] + cb_ref[...]


def _sconv(state, act, conv_w, conv_b):
    return pl.pallas_call(
        _sconv_kernel,
        out_shape=(
            jax.ShapeDtypeStruct((1, N_SAMPLE, CONV_HIST, D_MODEL), F32),
            jax.ShapeDtypeStruct((N_SAMPLE, D_MODEL), F32),
        ),
        grid=(N_SAMPLE // SCONV_B,),
        in_specs=[
            pl.BlockSpec((1, SCONV_B, CONV_HIST, D_MODEL), lambda i: (0, i, 0, 0)),
            pl.BlockSpec((SCONV_B, D_MODEL), lambda i: (N_PROMPT // SCONV_B + i, ACT_GLU)),
            pl.BlockSpec((CONV_HIST, D_MODEL), lambda i: (0, 0)),
            pl.BlockSpec((1, D_MODEL), lambda i: (0, 0)),
            pl.BlockSpec((1, D_MODEL), lambda i: (0, 0)),
        ],
        out_specs=(
            pl.BlockSpec((1, SCONV_B, CONV_HIST, D_MODEL), lambda i: (0, i, 0, 0)),
            pl.BlockSpec((SCONV_B, D_MODEL), lambda i: (i, 0)),
        ),
        name="sconv",
    )(state, act, conv_w[:CONV_HIST], conv_w[CONV_HIST:], conv_b)


CONV_STRIP = 256
SUBLANES = 8
WIN_ROWS = HIST_BLOCK + CHUNK
WIN_PAD = 256


def _seqmix_kernel(gv_ref, gu_ref, cur_ref, prev_ref, convs_ref, wmix_ref, bmix_ref, w00_ref, b00_ref,
                   lvg_ref, lvb_ref, cw_ref, cb_ref, lcg_ref, lcb_ref, sel_ref, a_ref, b_ref, vns_ref,
                   win_ref, shift_ref, conv_ref):
    c = pl.program_id(0)
    is_sample = c == N_CHUNKS - 1

    vn = _layer_norm(gv_ref[...], lvg_ref[...], lvb_ref[...])

    @pl.when(is_sample)
    def _():
        vns_ref[...] = vn

    vnb = vn.astype(BF16)
    for g in range(A_GROUPS):
        sl = slice(g * A_GROUP_DIM, (g + 1) * A_GROUP_DIM)
        s_chunk = _bdot(wmix_ref[g], vnb[:, sl]) + bmix_ref[:, sl]
        s_first = vn[:, sl] * w00_ref[:, sl] + b00_ref[:, sl]
        a_ref[:, sl] = (gu_ref[:, sl] * jnp.where(is_sample, s_first, s_chunk)).astype(BF16)

    fresh = (c % CHUNKS_PER_SEQ) == 0

    @pl.when(c == 0)
    def _():
        win_ref[WIN_ROWS:, :] = jnp.zeros((WIN_PAD - WIN_ROWS, D_MODEL), BF16)

    win_ref[0:HIST_BLOCK, :] = jnp.where(fresh, 0.0, prev_ref[...]).astype(BF16)
    win_ref[HIST_BLOCK:WIN_ROWS, :] = cur_ref[...].astype(BF16)
    shift_ref[0] = win_ref[0:WIN_ROWS, :].astype(F32)
    for r in range(1, SUBLANES):
        shift_ref[r] = _bdot(sel_ref[r - 1], win_ref[...])
    off = HIST_BLOCK - CONV_HIST

    def tap(k, cs):
        q, r = divmod(off + k, SUBLANES)
        return (shift_ref[r, SUBLANES * q:SUBLANES * q + CHUNK, cs]
                * cw_ref[k:k + 1, cs].astype(BF16).astype(F32))

    for s0 in range(0, D_MODEL, CONV_STRIP):
        cs = slice(s0, s0 + CONV_STRIP)
        acc = tap(0, cs)
        for k in range(1, CONV_WIDTH):
            acc = acc + tap(k, cs)
        conv_ref[:, cs] = acc + cb_ref[:, cs]
    conv = jnp.where(is_sample, convs_ref[...], conv_ref[...])
    b_ref[...] = _silu(_layer_norm(conv, lcg_ref[...], lcb_ref[...])).astype(BF16)


def _seqmix(act, conv_s, wmix, bmix, w00, b00, ln_v_g, ln_v_b, conv_w, conv_b, ln_c_g, ln_c_b, sel):
    row = lambda c: (c, 0)
    const2 = lambda c: (0, 0)
    vec = pl.BlockSpec((1, D_MODEL), const2)
    hist_per_chunk = CHUNK // HIST_BLOCK
    act_block = lambda blk: pl.BlockSpec((CHUNK, D_MODEL), lambda c: (c, blk))
    return pl.pallas_call(
        _seqmix_kernel,
        out_shape=(
            jax.ShapeDtypeStruct((N_ROWS, D_MODEL), BF16),
            jax.ShapeDtypeStruct((N_ROWS, D_MODEL), BF16),
            jax.ShapeDtypeStruct((N_SAMPLE, D_MODEL), F32),
        ),
        grid=(N_CHUNKS,),
        in_specs=[
            act_block(ACT_GV), act_block(ACT_GU), act_block(ACT_GLU),
            pl.BlockSpec((HIST_BLOCK, D_MODEL), lambda c: (jnp.maximum(c * hist_per_chunk - 1, 0), ACT_GLU)),
            pl.BlockSpec((N_SAMPLE, D_MODEL), const2),
            pl.BlockSpec((A_GROUPS, CHUNK, CHUNK), lambda c: (0, 0, 0)),
            pl.BlockSpec((CHUNK, D_MODEL), const2),
            vec, vec, vec, vec,
            pl.BlockSpec((CONV_WIDTH + 1, D_MODEL), const2),
            vec, vec, vec,
            pl.BlockSpec((SUBLANES - 1, WIN_ROWS, WIN_PAD), lambda c: (0, 0, 0)),
        ],
        out_specs=(
            pl.BlockSpec((CHUNK, D_MODEL), row),
            pl.BlockSpec((CHUNK, D_MODEL), row),
            pl.BlockSpec((N_SAMPLE, D_MODEL), const2),
        ),
        scratch_shapes=[pltpu.VMEM((WIN_PAD, D_MODEL), BF16), pltpu.VMEM((SUBLANES, WIN_ROWS, D_MODEL), F32),
                        pltpu.VMEM((CHUNK, D_MODEL), F32)],
        compiler_params=pltpu.CompilerParams(
            dimension_semantics=("arbitrary",), vmem_limit_bytes=VMEM_LIMIT),
        name="seqmix",
    )(act, act, act, act, conv_s, wmix, bmix, w00, b00, ln_v_g, ln_v_b, conv_w, conv_b, ln_c_g, ln_c_b, sel)


def _route(logits):
    col = lax.broadcasted_iota(jnp.int32, logits.shape, 1).astype(F32)
    neg = jnp.float32(-jnp.inf)
    big = jnp.float32(1e9)
    lg = jnp.where(col < MOE_GROUPS, logits, neg)
    gmax = jnp.max(lg, axis=-1, keepdims=True)
    gi = jnp.min(jnp.where(lg == gmax, col, big), axis=-1, keepdims=True)
    gw = 1.0 / jnp.sum(jnp.exp(lg - gmax), axis=-1, keepdims=True)
    lo = MOE_GROUPS + gi * EXPERTS_PER_GROUP
    le = jnp.where((col >= lo) & (col < lo + EXPERTS_PER_GROUP), logits, neg)
    m1 = jnp.max(le, axis=-1, keepdims=True)
    i1 = jnp.min(jnp.where(le == m1, col, big), axis=-1, keepdims=True)
    le2 = jnp.where(col == i1, neg, le)
    m2 = jnp.max(le2, axis=-1, keepdims=True)
    i2 = jnp.min(jnp.where(le2 == m2, col, big), axis=-1, keepdims=True)
    e = jnp.exp(m2 - m1)
    w1 = gw / (1.0 + e)
    w2 = gw * e / (1.0 + e)
    eid = jnp.where(col == 0, i1 - MOE_GROUPS, jnp.where(col == 1, i2 - MOE_GROUPS, 0.0)).astype(jnp.int32)
    ew = jnp.where(col == 0, w1, jnp.where(col == 1, w2, 0.0))
    return eid, ew


def _mixout_kernel(a_ref, b_ref, sga_ref, sgb_ref, x_ref, wa_hbm, wb_hbm, wo_hbm, nf_ref, wr_ref, br_ref,
                   h_ref, eid_ref, ew_ref, wa_ref, wb_ref, wo_ref, sem):
    @pl.when(pl.program_id(0) == 0)
    def _():
        copies = [pltpu.make_async_copy(src, dst, sem.at[k])
                  for k, (src, dst) in enumerate(((wa_hbm, wa_ref), (wb_hbm, wb_ref), (wo_hbm, wo_ref)))]
        for cp in copies:
            cp.start()
        for cp in copies:
            cp.wait()

    pa = _bdot(a_ref[...], wa_ref[...])
    pb = _bdot(b_ref[...], wb_ref[...])
    m = (sga_ref[...] * pa + sgb_ref[...] * pb).astype(BF16)
    h = x_ref[...] + _bdot(m, wo_ref[...])
    h_ref[...] = h
    hn = _rms(h, nf_ref[...])
    logits = _bdot(hn.astype(BF16), wr_ref[...]) + br_ref[...]
    eid, ew = _route(logits)
    eid_ref[...] = eid
    ew_ref[...] = ew


def _mixout(a_in, b_in, act, x_all, wa, wb, wo, norm_ffn, w_route, b_route):
    row = lambda i: (i, 0)
    const2 = lambda i: (0, 0)
    rows = pl.BlockSpec((MIX_TM, D_MODEL), row)
    hbm = pl.BlockSpec(memory_space=pltpu.MemorySpace.HBM)
    return pl.pallas_call(
        _mixout_kernel,
        out_shape=(
            jax.ShapeDtypeStruct((N_ROWS, D_MODEL), F32),
            jax.ShapeDtypeStruct((N_ROWS, LANES), jnp.int32),
            jax.ShapeDtypeStruct((N_ROWS, LANES), F32),
        ),
        grid=(N_ROWS // MIX_TM,),
        in_specs=[rows, rows, pl.BlockSpec((MIX_TM, D_MODEL), lambda i: (i, ACT_SGA)),
                  pl.BlockSpec((MIX_TM, D_MODEL), lambda i: (i, ACT_SGB)), rows, hbm, hbm, hbm,
                  pl.BlockSpec((1, D_MODEL), const2),
                  pl.BlockSpec((D_MODEL, LANES), const2),
                  pl.BlockSpec((1, LANES), const2)],
        out_specs=(rows, pl.BlockSpec((MIX_TM, LANES), row), pl.BlockSpec((MIX_TM, LANES), row)),
        scratch_shapes=[pltpu.VMEM((D_MODEL, D_MODEL), BF16)] * 3 + [pltpu.SemaphoreType.DMA((3,))],
        compiler_params=pltpu.CompilerParams(
            dimension_semantics=("arbitrary",), vmem_limit_bytes=VMEM_LIMIT),
        name="mixout",
    )(a_in, b_in, act, act, x_all, wa, wb, wo, norm_ffn, w_route, b_route)


def _experts_kernel(item_e, item_blk0, item_nb, src_hbm, dst_hbm, h_hbm, nf_ref, wg_ref, wu_ref, wd_ref,
                    y_hbm, src_s, dst_s, pend_s, xg_ref, yst_ref, xs_ref, acc_ref, wgb_ref, wub_ref, wdb_ref,
                    idx_sem, g_sem, s_sem):
    w = pl.program_id(0)
    s = pl.program_id(1)
    nb = item_nb[w]
    nxt = jnp.minimum(w + 1, MAX_ITEMS - 1)
    nb_next = jnp.where(w + 1 < MAX_ITEMS, item_nb[nxt], 0)

    def table_fetch(tbl_hbm, tbl_s, sem_idx, item):
        slot0 = pl.multiple_of(item_blk0[item] * ROW_BLOCK, ROW_BLOCK)
        return pltpu.make_async_copy(tbl_hbm.at[pl.ds(slot0, ITEM_ROWS)], tbl_s, idx_sem.at[sem_idx])

    def row_copy_in(r, blk):
        return pltpu.make_async_copy(h_hbm.at[pl.ds(src_s[r], 1)], xg_ref.at[pl.ds(r, 1)], g_sem.at[blk])

    def row_copy_out(r, blk):
        return pltpu.make_async_copy(yst_ref.at[pl.ds(r, 1)], y_hbm.at[pl.ds(dst_s[r], 1)], s_sem.at[blk])

    def block_rows(blk):
        return pl.ds(pl.multiple_of(blk * ROW_BLOCK, ROW_BLOCK), ROW_BLOCK)

    def block_copy_wait(sem, blk):
        pltpu.make_async_copy(xg_ref.at[block_rows(blk)], xg_ref.at[block_rows(blk)], sem.at[blk]).wait()

    def issue_gather(n_blocks):
        def issue(blk, carry):
            def one(i, carry2):
                row_copy_in(blk * ROW_BLOCK + i, blk).start()
                return carry2
            return lax.fori_loop(0, ROW_BLOCK, one, carry, unroll=8)
        lax.fori_loop(0, n_blocks, issue, 0)

    def drain_scatter():
        def drain(blk, carry):
            block_copy_wait(s_sem, blk)
            return carry
        lax.fori_loop(0, pend_s[0], drain, 0)
        pend_s[0] = 0

    @pl.when((w == 0) & (s == 0))
    def _():
        pad_rows = pl.ds(0, ROW_BLOCK)
        xg_ref[pad_rows, :] = jnp.zeros((ROW_BLOCK, D_MODEL), F32)
        for row0 in (N_ROWS, DUMP_ROW0):
            cz = pltpu.make_async_copy(xg_ref.at[pad_rows], y_hbm.at[pl.ds(row0, ROW_BLOCK)], idx_sem.at[0])
            cz.start()
            cz.wait()
        pend_s[0] = 0
        ci = table_fetch(src_hbm, src_s, 0, w)
        ci.start()
        ci.wait()
        issue_gather(nb)

    @pl.when((s == 0) & (nb > 0))
    def _():
        table_fetch(dst_hbm, dst_s, 1, w).start()

        def land(blk, carry):
            block_copy_wait(g_sem, blk)
            xs_ref[block_rows(blk), :] = _rms(xg_ref[block_rows(blk), :], nf_ref[...]).astype(BF16)
            return carry
        lax.fori_loop(0, nb, land, 0)

    @pl.when((s == 1) & (nb_next > 0))
    def _():
        table_fetch(src_hbm, src_s, 0, nxt).start()

    @pl.when((s == 2) & (nb_next > 0))
    def _():
        table_fetch(src_hbm, src_s, 0, nxt).wait()
        issue_gather(nb_next)

    @pl.when(nb > 0)
    def _():
        wgb_ref[...] = wg_ref[...].astype(BF16)
        wub_ref[...] = wu_ref[...].astype(BF16)
        wdb_ref[...] = wd_ref[...].astype(BF16)

        def compute(row0, nrows):
            rows = pl.ds(row0, nrows)
            x = xs_ref[rows, :]
            hm = (_silu(_bdot(x, wgb_ref[...])) * _bdot(x, wub_ref[...])).astype(BF16)
            y = _bdot(hm, wdb_ref[...])

            @pl.when(s == 0)
            def _():
                acc_ref[rows, :] = y

            @pl.when(s != 0)
            def _():
                acc_ref[rows, :] += y

        def four_blocks(j, carry):
            compute(pl.multiple_of(j * (4 * ROW_BLOCK), 4 * ROW_BLOCK), 4 * ROW_BLOCK)
            return carry
        lax.fori_loop(0, nb >> 2, four_blocks, 0)

        @pl.when((nb & 2) == 2)
        def _():
            compute(pl.multiple_of((nb >> 2) * (4 * ROW_BLOCK), 2 * ROW_BLOCK), 2 * ROW_BLOCK)

        @pl.when((nb & 1) == 1)
        def _():
            compute(pl.multiple_of((nb - 1) * ROW_BLOCK, ROW_BLOCK), ROW_BLOCK)

    @pl.when((s == EXPERT_SLICES - 1) & (nb > 0))
    def _():
        drain_scatter()
        table_fetch(dst_hbm, dst_s, 1, w).wait()

        def send(blk, carry):
            yst_ref[block_rows(blk), :] = acc_ref[block_rows(blk), :]

            def one(i, carry2):
                row_copy_out(blk * ROW_BLOCK + i, blk).start()
                return carry2
            return lax.fori_loop(0, ROW_BLOCK, one, carry, unroll=8)
        lax.fori_loop(0, nb, send, 0)
        pend_s[0] = nb

    @pl.when((w == MAX_ITEMS - 1) & (s == EXPERT_SLICES - 1))
    def _():
        drain_scatter()


def _experts(item_e, item_blk0, item_nb, src_row, dst_row, h_all, norm_ffn, w_gate, w_up, w_down):
    hbm = pl.BlockSpec(memory_space=pltpu.MemorySpace.HBM)

    def w_slice(w, s, inb):
        return jnp.where(inb[w] > 0, s, EXPERT_SLICES - 1)

    grid_spec = pltpu.PrefetchScalarGridSpec(
        num_scalar_prefetch=3,
        grid=(MAX_ITEMS, EXPERT_SLICES),
        in_specs=[
            hbm, hbm, hbm,
            pl.BlockSpec((1, D_MODEL), lambda w, s, ie, ib, inb: (0, 0)),
            pl.BlockSpec((None, D_MODEL, EXPERT_TN), lambda w, s, ie, ib, inb: (ie[w], 0, w_slice(w, s, inb))),
            pl.BlockSpec((None, D_MODEL, EXPERT_TN), lambda w, s, ie, ib, inb: (ie[w], 0, w_slice(w, s, inb))),
            pl.BlockSpec((None, EXPERT_TN, D_MODEL), lambda w, s, ie, ib, inb: (ie[w], w_slice(w, s, inb), 0)),
        ],
        out_specs=hbm,
        scratch_shapes=[
            pltpu.SMEM((ITEM_ROWS,), jnp.int32),
            pltpu.SMEM((ITEM_ROWS,), jnp.int32),
            pltpu.SMEM((1,), jnp.int32),
            pltpu.VMEM((ITEM_ROWS, D_MODEL), F32),
            pltpu.VMEM((ITEM_ROWS, D_MODEL), F32),
            pltpu.VMEM((ITEM_ROWS, D_MODEL), BF16),
            pltpu.VMEM((ITEM_ROWS, D_MODEL), F32),
            pltpu.VMEM((D_MODEL, EXPERT_TN), BF16),
            pltpu.VMEM((D_MODEL, EXPERT_TN), BF16),
            pltpu.VMEM((EXPERT_TN, D_MODEL), BF16),
            pltpu.SemaphoreType.DMA((2,)),
            pltpu.SemaphoreType.DMA((ITEM_BLOCKS,)),
            pltpu.SemaphoreType.DMA((ITEM_BLOCKS,)),
        ],
    )
    return pl.pallas_call(
        _experts_kernel,
        out_shape=jax.ShapeDtypeStruct((PAIR_ROWS, D_MODEL), F32),
        grid_spec=grid_spec,
        compiler_params=pltpu.CompilerParams(
            dimension_semantics=("arbitrary", "arbitrary"), vmem_limit_bytes=VMEM_LIMIT),
        name="experts",
    )(item_e, item_blk0, item_nb, src_row, dst_row, h_all, norm_ffn, w_gate, w_up, w_down)


def _dispatch_tables(eid):
    flat_e = eid.reshape(-1)
    onehot = (flat_e[:, None] == jnp.arange(N_EXPERTS, dtype=jnp.int32)[None, :]).astype(jnp.int32)
    rank = jnp.take_along_axis(jnp.cumsum(onehot, axis=0) - onehot, flat_e[:, None], axis=1)[:, 0]
    counts = jnp.sum(onehot, axis=0)
    nblk = (counts + ROW_BLOCK - 1) // ROW_BLOCK
    blk0 = jnp.cumsum(nblk) - nblk
    slot = blk0[flat_e] * ROW_BLOCK + rank
    slot_pair = jnp.full((SLOT_TABLE,), N_PAIRS, jnp.int32).at[slot].set(jnp.arange(N_PAIRS, dtype=jnp.int32))
    slot_id = jnp.arange(SLOT_TABLE, dtype=jnp.int32)
    src_row = jnp.minimum(slot_pair // TOP_K, N_ROWS - 1)
    dst_row = jnp.where(slot_pair < N_PAIRS, (slot_pair % TOP_K) * PAIR_REGION + slot_pair // TOP_K,
                        DUMP_ROW0 + slot_id % ROW_BLOCK)
    nitem = (nblk + ITEM_BLOCKS - 1) // ITEM_BLOCKS
    item_end = jnp.cumsum(nitem)
    total = item_end[-1]
    idx = jnp.arange(MAX_ITEMS, dtype=jnp.int32)
    live = idx < total
    e_of = jnp.minimum(jnp.searchsorted(item_end, jnp.minimum(idx, total - 1), side='right'),
                       N_EXPERTS - 1).astype(jnp.int32)
    local = jnp.minimum(idx, total - 1) - (item_end - nitem)[e_of]
    item_blk0 = (blk0[e_of] + local * ITEM_BLOCKS).astype(jnp.int32)
    item_nb = jnp.where(live, jnp.clip(nblk[e_of] - local * ITEM_BLOCKS, 0, ITEM_BLOCKS), 0).astype(jnp.int32)
    return e_of, item_blk0, item_nb, src_row, dst_row


def _final_rows(h, y0, y1, ew, p, wg_ref, wp_ref, np_ref, fn_ref):
    h = h + (ew[:, 0:1] * y0 + ew[:, 1:2] * y1)
    gate = _sigmoid(_bdot(_rms(h, np_ref[...]).astype(BF16), wg_ref[...]))
    h = h + gate * _bdot(p.astype(BF16), wp_ref[...])
    return _rms(h, fn_ref[...])


def _final_kernel(h_ref, y0_ref, y1_ref, ew_ref, p_ref, hs_ref, y0s_ref, y1s_ref, ews_ref, ps_ref,
                  wg_hbm, wp_hbm, np_ref, fn_ref, op_ref, os_ref, wg_ref, wp_ref, sem):
    i = pl.program_id(0)

    @pl.when(i == 0)
    def _():
        c0 = pltpu.make_async_copy(wg_hbm, wg_ref, sem.at[0])
        c1 = pltpu.make_async_copy(wp_hbm, wp_ref, sem.at[1])
        c0.start()
        c1.start()
        c0.wait()
        c1.wait()

    @pl.when(i < FIN_PROMPT_STEPS)
    def _():
        op_ref[...] = _final_rows(h_ref[...], y0_ref[...], y1_ref[...], ew_ref[...], p_ref[...],
                                  wg_ref, wp_ref, np_ref, fn_ref)

    @pl.when(i == FIN_PROMPT_STEPS)
    def _():
        os_ref[...] = _final_rows(hs_ref[...], y0s_ref[...], y1s_ref[...], ews_ref[...], ps_ref[...],
                                  wg_ref, wp_ref, np_ref, fn_ref)


def _final(h_all, ypairs, ew, p_prompt, p_sample, w_gate, w_proj, norm_ple, final_norm):
    last = FIN_PROMPT_STEPS - 1
    prow = lambda i: (jnp.minimum(i, last), 0)
    const2 = lambda i: (0, 0)
    hbm = pl.BlockSpec(memory_space=pltpu.MemorySpace.HBM)
    sblk = N_PROMPT // N_SAMPLE
    pact = pl.BlockSpec((FIN_TM, D_MODEL), prow)
    sact = lambda off: pl.BlockSpec((N_SAMPLE, D_MODEL), lambda i: (off + sblk, 0))
    return pl.pallas_call(
        _final_kernel,
        out_shape=(jax.ShapeDtypeStruct((N_PROMPT, D_MODEL), F32), jax.ShapeDtypeStruct((N_SAMPLE, D_MODEL), F32)),
        grid=(FIN_PROMPT_STEPS + 1,),
        in_specs=[pact, pact,
                  pl.BlockSpec((FIN_TM, D_MODEL), lambda i: (jnp.minimum(i, last) + PAIR_REGION // FIN_TM, 0)),
                  pl.BlockSpec((FIN_TM, LANES), prow),
                  pl.BlockSpec((FIN_TM, PLE_DIM), prow),
                  sact(0), sact(0), sact(PAIR_REGION // N_SAMPLE),
                  pl.BlockSpec((N_SAMPLE, LANES), lambda i: (sblk, 0)),
                  pl.BlockSpec((N_SAMPLE, PLE_DIM), const2),
                  hbm, hbm,
                  pl.BlockSpec((1, D_MODEL), const2),
                  pl.BlockSpec((1, D_MODEL), const2)],
        out_specs=(pact, pl.BlockSpec((N_SAMPLE, D_MODEL), const2)),
        scratch_shapes=[pltpu.VMEM((D_MODEL, D_MODEL), BF16), pltpu.VMEM((PLE_DIM, D_MODEL), BF16),
                        pltpu.SemaphoreType.DMA((2,))],
        compiler_params=pltpu.CompilerParams(
            dimension_semantics=("arbitrary",), vmem_limit_bytes=VMEM_LIMIT),
        name="final",
    )(h_all, ypairs, ypairs, ew, p_prompt, h_all, ypairs, ypairs, ew, p_sample, w_gate, w_proj, norm_ple, final_norm)


def kernel(x_prompt, x_sample, state_conv, p_prompt, p_sample, norm_mix, w_in, ln_v_g, ln_v_b, w_spatial,
           b_spatial, w_proj_a, conv_w, conv_b, ln_c_g, ln_c_b, w_proj_b, w_out, norm_ffn, w_router_group,
           b_router_group, w_router_expert, b_router_expert, w_exp_gate, w_exp_up, w_exp_down, norm_ple,
           w_ple_gate, w_ple_proj, final_norm):
    assert w_in.shape[0] == 1, "single layer"
    vec = lambda v: v.reshape(1, -1)
    xn, x_all = _xnorm(x_prompt.reshape(N_PROMPT, D_MODEL), x_sample.reshape(N_SAMPLE, D_MODEL), vec(norm_mix[0]))

    w_s, b_s = w_spatial[0], b_spatial[0]
    wmix = jnp.tril(w_s).astype(BF16)
    bmix = jnp.repeat(b_s.T, A_GROUP_DIM, axis=1)
    w00 = jnp.repeat(w_s[:, 0, 0], A_GROUP_DIM).reshape(1, D_MODEL)
    b00 = jnp.repeat(b_s[:, 0], A_GROUP_DIM).reshape(1, D_MODEL)

    act = _in_proj(xn, w_in[0])

    cw = conv_w[0]
    new_conv_sample, conv_s = _sconv(state_conv, act, cw, vec(conv_b[0]))
    cw_pad = jnp.concatenate([cw, jnp.zeros((1, D_MODEL), F32)], axis=0)
    wi = jnp.arange(WIN_ROWS, dtype=jnp.int32)[None, :, None]
    wj = jnp.arange(WIN_PAD, dtype=jnp.int32)[None, None, :]
    wr = jnp.arange(1, SUBLANES, dtype=jnp.int32)[:, None, None]
    sel = (wj == wi + wr).astype(BF16)
    a_in, b_in, vn_s = _seqmix(act, conv_s, wmix, bmix, w00, b00, vec(ln_v_g[0]), vec(ln_v_b[0]), cw_pad,
                               vec(conv_b[0]), vec(ln_c_g[0]), vec(ln_c_b[0]), sel)

    w_route = jnp.zeros((D_MODEL, LANES), F32)
    w_route = w_route.at[:, :MOE_GROUPS].set(w_router_group[0])
    w_route = w_route.at[:, MOE_GROUPS:MOE_GROUPS + N_EXPERTS].set(w_router_expert[0])
    b_route = jnp.zeros((1, LANES), F32)
    b_route = b_route.at[0, :MOE_GROUPS].set(b_router_group[0])
    b_route = b_route.at[0, MOE_GROUPS:MOE_GROUPS + N_EXPERTS].set(b_router_expert[0])

    wa = _cast_bf16(w_proj_a[0], 512)
    wb = _cast_bf16(w_proj_b[0], 512)
    wo = _cast_bf16(w_out[0], 512)
    h_all, eid, ew = _mixout(a_in, b_in, act, x_all, wa, wb, wo, vec(norm_ffn[0]),
                             w_route.astype(BF16), b_route)

    tables = _dispatch_tables(eid[:, :TOP_K])
    ypairs = _experts(*tables, h_all, vec(norm_ffn[0]), w_exp_gate[0], w_exp_up[0], w_exp_down[0])

    wpg = _cast_bf16(w_ple_gate[0], 512)
    wpp = _cast_bf16(w_ple_proj[0], PLE_DIM)
    y_prompt, y_sample = _final(h_all, ypairs, ew, p_prompt[0].reshape(N_PROMPT, PLE_DIM),
                                p_sample[0].reshape(N_SAMPLE, PLE_DIM), wpg, wpp, vec(norm_ple[0]), vec(final_norm))
    y_prompt = y_prompt.reshape(N_PROMPT_SEQ, SEQ, D_MODEL)
    y_sample = y_sample.reshape(N_SAMPLE, 1, D_MODEL)
    glu_cols = slice(ACT_GLU * D_MODEL, (ACT_GLU + 1) * D_MODEL)
    new_conv_prompt = jnp.stack([act[(b + 1) * SEQ - CONV_HIST:(b + 1) * SEQ, glu_cols]
                                 for b in range(N_PROMPT_SEQ)])[None]
    return (y_prompt, y_sample, new_conv_prompt, new_conv_sample, vn_s.reshape(1, N_SAMPLE, 1, D_MODEL))
```

```python
import jax
import jax.numpy as jnp
from jax import lax
from jax.experimental import pallas as pl
from jax.experimental.pallas import tpu as pltpu

F32 = jnp.float32
BF16 = jnp.bfloat16

D_MODEL = 2048
N_PROMPT_SEQ = 4
SEQ = 2048
N_PROMPT = N_PROMPT_SEQ * SEQ
N_SAMPLE = 128
N_ROWS = N_PROMPT + N_SAMPLE
CHUNK = 128
N_CHUNKS = N_ROWS // CHUNK
CHUNKS_PER_SEQ = SEQ // CHUNK
A_GROUPS = 8
A_GROUP_DIM = D_MODEL // A_GROUPS
CONV_WIDTH = 31
CONV_HIST = CONV_WIDTH - 1
HIST_BLOCK = 32
MOE_GROUPS = 4
EXPERTS_PER_GROUP = 8
N_EXPERTS = MOE_GROUPS * EXPERTS_PER_GROUP
TOP_K = 2
D_EXPERT = D_MODEL // 2
PLE_DIM = 256
EPS = 1e-6
LANES = 128

XN_TM = 512
IN_TM = 1664
IN_TN = 512
IN_COL_TILES = D_MODEL // IN_TN
ACT_BLOCKS = 5
ACT_GU, ACT_GV, ACT_GLU, ACT_SGA, ACT_SGB = range(ACT_BLOCKS)
MIX_TM = 208
FIN_TM = 256
FIN_PROMPT_STEPS = N_PROMPT // FIN_TM
ROW_BLOCK = 128
N_PAIRS = N_ROWS * TOP_K
MAX_BLOCKS = N_PAIRS // ROW_BLOCK + N_EXPERTS
ITEM_BLOCKS = 8
ITEM_ROWS = ITEM_BLOCKS * ROW_BLOCK
MAX_ITEMS = (MAX_BLOCKS + (ITEM_BLOCKS - 1) * N_EXPERTS) // ITEM_BLOCKS
N_SLOTS = MAX_BLOCKS * ROW_BLOCK
SLOT_TABLE = N_SLOTS + ITEM_ROWS
EXPERT_SLICES = 4
EXPERT_TN = D_EXPERT // EXPERT_SLICES
PAIR_REGION = 8448
DUMP_ROW0 = PAIR_REGION + N_ROWS
PAIR_ROWS = DUMP_ROW0 + ROW_BLOCK

VMEM_LIMIT = 56 * 1024 * 1024


def _rms(x, g):
    return x * lax.rsqrt(jnp.mean(x * x, axis=-1, keepdims=True) + EPS) * g


def _layer_norm(x, g, b):
    mu = jnp.mean(x, axis=-1, keepdims=True)
    xc = x - mu
    return xc * lax.rsqrt(jnp.mean(xc * xc, axis=-1, keepdims=True) + EPS) * g + b


def _sigmoid(x):
    return 1.0 / (1.0 + jnp.exp(-x))


def _silu(x):
    return x * _sigmoid(x)


def _gelu(x):
    return jax.nn.gelu(x, approximate=True)


def _bdot(a, b):
    return jnp.dot(a, b, preferred_element_type=F32)


def _cast_kernel(w_ref, o_ref):
    o_ref[...] = w_ref[...].astype(BF16)


def _cast_bf16(w, rows):
    k, n = w.shape
    return pl.pallas_call(
        _cast_kernel,
        out_shape=jax.ShapeDtypeStruct((k, n), BF16),
        grid=(k // rows,),
        in_specs=[pl.BlockSpec((rows, n), lambda i: (i, 0))],
        out_specs=pl.BlockSpec((rows, n), lambda i: (i, 0)),
        name="cast_bf16",
    )(w)


def _xnorm_kernel(xp_ref, xs_ref, nm_ref, xn_ref, xall_ref):
    i = pl.program_id(0)

    @pl.when(i < N_PROMPT // XN_TM)
    def _():
        x = xp_ref[...]
        xall_ref[...] = x
        xn_ref[...] = _rms(x, nm_ref[...]).astype(BF16)

    @pl.when(i == N_PROMPT // XN_TM)
    def _():
        x = xs_ref[...]
        xall_ref[0:N_SAMPLE, :] = x
        xall_ref[N_SAMPLE:, :] = jnp.zeros((XN_TM - N_SAMPLE, D_MODEL), F32)
        xn_ref[0:N_SAMPLE, :] = _rms(x, nm_ref[...]).astype(BF16)
        xn_ref[N_SAMPLE:, :] = jnp.zeros((XN_TM - N_SAMPLE, D_MODEL), BF16)


def _xnorm(x_prompt, x_sample, norm_mix):
    last = N_PROMPT // XN_TM - 1
    row = lambda i: (i, 0)
    return pl.pallas_call(
        _xnorm_kernel,
        out_shape=(jax.ShapeDtypeStruct((N_ROWS, D_MODEL), BF16), jax.ShapeDtypeStruct((N_ROWS, D_MODEL), F32)),
        grid=(N_PROMPT // XN_TM + 1,),
        in_specs=[pl.BlockSpec((XN_TM, D_MODEL), lambda i: (jnp.minimum(i, last), 0)),
                  pl.BlockSpec((N_SAMPLE, D_MODEL), lambda i: (0, 0)),
                  pl.BlockSpec((1, D_MODEL), lambda i: (0, 0))],
        out_specs=(pl.BlockSpec((XN_TM, D_MODEL), row), pl.BlockSpec((XN_TM, D_MODEL), row)),
        name="xnorm",
    )(x_prompt, x_sample, norm_mix)


def _in_w_col(j):
    t = IN_COL_TILES
    jj = j - 2 * t
    glu_col = 2 * t + (jj % 2) * t + jj // 2
    return jnp.where((j >= 2 * t) & (j < 4 * t), glu_col, j)


def _in_out_col(j):
    t = IN_COL_TILES
    return jnp.where(j < 2 * t, j, jnp.where(j < 4 * t, 2 * t + (j - 2 * t) // 2, j - t))


def _in_proj_kernel(xn_ref, w_ref, o_ref, val_ref):
    j = pl.program_id(1)
    t = IN_COL_TILES

    def z():
        return _bdot(xn_ref[...], w_ref[...].astype(BF16))

    @pl.when(j < 2 * t)
    def _():
        o_ref[...] = _gelu(z())

    @pl.when((j >= 2 * t) & (j < 4 * t) & (j % 2 == 0))
    def _():
        val_ref[...] = z()

    @pl.when((j >= 2 * t) & (j < 4 * t) & (j % 2 == 1))
    def _():
        o_ref[...] = val_ref[...] * _sigmoid(z())

    @pl.when(j >= 4 * t)
    def _():
        o_ref[...] = _sigmoid(z())


def _in_proj(xn, w_in):
    return pl.pallas_call(
        _in_proj_kernel,
        out_shape=jax.ShapeDtypeStruct((N_ROWS, ACT_BLOCKS * D_MODEL), F32),
        grid=(N_ROWS // IN_TM, 6 * IN_COL_TILES),
        in_specs=[
            pl.BlockSpec((IN_TM, D_MODEL), lambda i, j: (i, 0)),
            pl.BlockSpec((D_MODEL, IN_TN), lambda i, j: (0, _in_w_col(j))),
        ],
        out_specs=pl.BlockSpec((IN_TM, IN_TN), lambda i, j: (i, _in_out_col(j))),
        scratch_shapes=[pltpu.VMEM((IN_TM, IN_TN), F32)],
        compiler_params=pltpu.CompilerParams(
            dimension_semantics=("arbitrary", "arbitrary"), vmem_limit_bytes=VMEM_LIMIT),
        name="in_proj",
    )(xn, w_in)


SCONV_B = 16


def _sconv_kernel(st_ref, glu_ref, w_ref, wl_ref, cb_ref, new_ref, conv_ref):
    st = st_ref[0]
    glu = glu_ref[...]
    new_ref[0, :, 0:CONV_HIST - 1, :] = st[:, 1:CONV_HIST, :]
    new_ref[0, :, CONV_HIST - 1, :] = glu
    acc = st[:, 0, :] * w_ref[0:1, :]
    for k in range(1, CONV_HIST):
        acc = acc + st[:, k, :] * w_ref[k:k + 1, :]
    conv_ref[...] = acc + glu * wl_ref[...] + cb_ref[...]


def _sconv(state, act, conv_w, conv_b):
    return pl.pallas_call(
        _sconv_kernel,
        out_shape=(
            jax.ShapeDtypeStruct((1, N_SAMPLE, CONV_HIST, D_MODEL), F32),
            jax.ShapeDtypeStruct((N_SAMPLE, D_MODEL), F32),
        ),
        grid=(N_SAMPLE // SCONV_B,),
        in_specs=[
            pl.BlockSpec((1, SCONV_B, CONV_HIST, D_MODEL), lambda i: (0, i, 0, 0)),
            pl.BlockSpec((SCONV_B, D_MODEL), lambda i: (N_PROMPT // SCONV_B + i, ACT_GLU)),
            pl.BlockSpec((CONV_HIST, D_MODEL), lambda i: (0, 0)),
            pl.BlockSpec((1, D_MODEL), lambda i: (0, 0)),
            pl.BlockSpec((1, D_MODEL), lambda i: (0, 0)),
        ],
        out_specs=(
            pl.BlockSpec((1, SCONV_B, CONV_HIST, D_MODEL), lambda i: (0, i, 0, 0)),
            pl.BlockSpec((SCONV_B, D_MODEL), lambda i: (i, 0)),
        ),
        name="sconv",
    )(state, act, conv_w[:CONV_HIST], conv_w[CONV_HIST:], conv_b)


CONV_STRIP = 256
SUBLANES = 8
WIN_ROWS = HIST_BLOCK + CHUNK
WIN_PAD = 256


def _seqmix_kernel(gv_ref, gu_ref, cur_ref, prev_ref, convs_ref, wmix_ref, bmix_ref, w00_ref, b00_ref,
                   lvg_ref, lvb_ref, cw_ref, cb_ref, lcg_ref, lcb_ref, sel_ref, a_ref, b_ref, vns_ref,
                   win_ref, shift_ref, conv_ref):
    c = pl.program_id(0)
    is_sample = c == N_CHUNKS - 1

    vn = _layer_norm(gv_ref[...], lvg_ref[...], lvb_ref[...])

    @pl.when(is_sample)
    def _():
        vns_ref[...] = vn

    vnb = vn.astype(BF16)
    for g in range(A_GROUPS):
        sl = slice(g * A_GROUP_DIM, (g + 1) * A_GROUP_DIM)
        s_chunk = _bdot(wmix_ref[g], vnb[:, sl]) + bmix_ref[:, sl]
        s_first = vn[:, sl] * w00_ref[:, sl] + b00_ref[:, sl]
        a_ref[:, sl] = (gu_ref[:, sl] * jnp.where(is_sample, s_first, s_chunk)).astype(BF16)

    fresh = (c % CHUNKS_PER_SEQ) == 0

    @pl.when(c == 0)
    def _():
        win_ref[WIN_ROWS:, :] = jnp.zeros((WIN_PAD - WIN_ROWS, D_MODEL), BF16)

    win_ref[0:HIST_BLOCK, :] = jnp.where(fresh, 0.0, prev_ref[...]).astype(BF16)
    win_ref[HIST_BLOCK:WIN_ROWS, :] = cur_ref[...].astype(BF16)
    shift_ref[0] = win_ref[0:WIN_ROWS, :].astype(F32)
    for r in range(1, SUBLANES):
        shift_ref[r] = _bdot(sel_ref[r - 1], win_ref[...])
    off = HIST_BLOCK - CONV_HIST

    def tap(k, cs):
        q, r = divmod(off + k, SUBLANES)
        return (shift_ref[r, SUBLANES * q:SUBLANES * q + CHUNK, cs]
                * cw_ref[k:k + 1, cs].astype(BF16).astype(F32))

    for s0 in range(0, D_MODEL, CONV_STRIP):
        cs = slice(s0, s0 + CONV_STRIP)
        acc = tap(0, cs)
        for k in range(1, CONV_WIDTH):
            acc = acc + tap(k, cs)
        conv_ref[:, cs] = acc + cb_ref[:, cs]
    conv = jnp.where(is_sample, convs_ref[...], conv_ref[...])
    b_ref[...] = _silu(_layer_norm(conv, lcg_ref[...], lcb_ref[...])).astype(BF16)


def _seqmix(act, conv_s, wmix, bmix, w00, b00, ln_v_g, ln_v_b, conv_w, conv_b, ln_c_g, ln_c_b, sel):
    row = lambda c: (c, 0)
    const2 = lambda c: (0, 0)
    vec = pl.BlockSpec((1, D_MODEL), const2)
    hist_per_chunk = CHUNK // HIST_BLOCK
    act_block = lambda blk: pl.BlockSpec((CHUNK, D_MODEL), lambda c: (c, blk))
    return pl.pallas_call(
        _seqmix_kernel,
        out_shape=(
            jax.ShapeDtypeStruct((N_ROWS, D_MODEL), BF16),
            jax.ShapeDtypeStruct((N_ROWS, D_MODEL), BF16),
            jax.ShapeDtypeStruct((N_SAMPLE, D_MODEL), F32),
        ),
        grid=(N_CHUNKS,),
        in_specs=[
            act_block(ACT_GV), act_block(ACT_GU), act_block(ACT_GLU),
            pl.BlockSpec((HIST_BLOCK, D_MODEL), lambda c: (jnp.maximum(c * hist_per_chunk - 1, 0), ACT_GLU)),
            pl.BlockSpec((N_SAMPLE, D_MODEL), const2),
            pl.BlockSpec((A_GROUPS, CHUNK, CHUNK), lambda c: (0, 0, 0)),
            pl.BlockSpec((CHUNK, D_MODEL), const2),
            vec, vec, vec, vec,
            pl.BlockSpec((CONV_WIDTH + 1, D_MODEL), const2),
            vec, vec, vec,
            pl.BlockSpec((SUBLANES - 1, WIN_ROWS, WIN_PAD), lambda c: (0, 0, 0)),
        ],
        out_specs=(
            pl.BlockSpec((CHUNK, D_MODEL), row),
            pl.BlockSpec((CHUNK, D_MODEL), row),
            pl.BlockSpec((N_SAMPLE, D_MODEL), const2),
        ),
        scratch_shapes=[pltpu.VMEM((WIN_PAD, D_MODEL), BF16), pltpu.VMEM((SUBLANES, WIN_ROWS, D_MODEL), F32),
                        pltpu.VMEM((CHUNK, D_MODEL), F32)],
        compiler_params=pltpu.CompilerParams(
            dimension_semantics=("arbitrary",), vmem_limit_bytes=VMEM_LIMIT),
        name="seqmix",
    )(act, act, act, act, conv_s, wmix, bmix, w00, b00, ln_v_g, ln_v_b, conv_w, conv_b, ln_c_g, ln_c_b, sel)


def _route(logits):
    col = lax.broadcasted_iota(jnp.int32, logits.shape, 1).astype(F32)
    neg = jnp.float32(-jnp.inf)
    big = jnp.float32(1e9)
    lg = jnp.where(col < MOE_GROUPS, logits, neg)
    gmax = jnp.max(lg, axis=-1, keepdims=True)
    gi = jnp.min(jnp.where(lg == gmax, col, big), axis=-1, keepdims=True)
    gw = 1.0 / jnp.sum(jnp.exp(lg - gmax), axis=-1, keepdims=True)
    lo = MOE_GROUPS + gi * EXPERTS_PER_GROUP
    le = jnp.where((col >= lo) & (col < lo + EXPERTS_PER_GROUP), logits, neg)
    m1 = jnp.max(le, axis=-1, keepdims=True)
    i1 = jnp.min(jnp.where(le == m1, col, big), axis=-1, keepdims=True)
    le2 = jnp.where(col == i1, neg, le)
    m2 = jnp.max(le2, axis=-1, keepdims=True)
    i2 = jnp.min(jnp.where(le2 == m2, col, big), axis=-1, keepdims=True)
    e = jnp.exp(m2 - m1)
    w1 = gw / (1.0 + e)
    w2 = gw * e / (1.0 + e)
    eid = jnp.where(col == 0, i1 - MOE_GROUPS, jnp.where(col == 1, i2 - MOE_GROUPS, 0.0)).astype(jnp.int32)
    ew = jnp.where(col == 0, w1, jnp.where(col == 1, w2, 0.0))
    return eid, ew


def _mixout_kernel(a_ref, b_ref, sga_ref, sgb_ref, x_ref, wa_hbm, wb_hbm, wo_hbm, nf_ref, wr_ref, br_ref,
                   h_ref, eid_ref, ew_ref, wa_ref, wb_ref, wo_ref, sem):
    @pl.when(pl.program_id(0) == 0)
    def _():
        copies = [pltpu.make_async_copy(src, dst, sem.at[k])
                  for k, (src, dst) in enumerate(((wa_hbm, wa_ref), (wb_hbm, wb_ref), (wo_hbm, wo_ref)))]
        for cp in copies:
            cp.start()
        for cp in copies:
            cp.wait()

    pa = _bdot(a_ref[...], wa_ref[...])
    pb = _bdot(b_ref[...], wb_ref[...])
    m = (sga_ref[...] * pa + sgb_ref[...] * pb).astype(BF16)
    h = x_ref[...] + _bdot(m, wo_ref[...])
    h_ref[...] = h
    hn = _rms(h, nf_ref[...])
    logits = _bdot(hn.astype(BF16), wr_ref[...]) + br_ref[...]
    eid, ew = _route(logits)
    eid_ref[...] = eid
    ew_ref[...] = ew


def _mixout(a_in, b_in, act, x_all, wa, wb, wo, norm_ffn, w_route, b_route):
    row = lambda i: (i, 0)
    const2 = lambda i: (0, 0)
    rows = pl.BlockSpec((MIX_TM, D_MODEL), row)
    hbm = pl.BlockSpec(memory_space=pltpu.MemorySpace.HBM)
    return pl.pallas_call(
        _mixout_kernel,
        out_shape=(
            jax.ShapeDtypeStruct((N_ROWS, D_MODEL), F32),
            jax.ShapeDtypeStruct((N_ROWS, LANES), jnp.int32),
            jax.ShapeDtypeStruct((N_ROWS, LANES), F32),
        ),
        grid=(N_ROWS // MIX_TM,),
        in_specs=[rows, rows, pl.BlockSpec((MIX_TM, D_MODEL), lambda i: (i, ACT_SGA)),
                  pl.BlockSpec((MIX_TM, D_MODEL), lambda i: (i, ACT_SGB)), rows, hbm, hbm, hbm,
                  pl.BlockSpec((1, D_MODEL), const2),
                  pl.BlockSpec((D_MODEL, LANES), const2),
                  pl.BlockSpec((1, LANES), const2)],
        out_specs=(rows, pl.BlockSpec((MIX_TM, LANES), row), pl.BlockSpec((MIX_TM, LANES), row)),
        scratch_shapes=[pltpu.VMEM((D_MODEL, D_MODEL), BF16)] * 3 + [pltpu.SemaphoreType.DMA((3,))],
        compiler_params=pltpu.CompilerParams(
            dimension_semantics=("arbitrary",), vmem_limit_bytes=VMEM_LIMIT),
        name="mixout",
    )(a_in, b_in, act, act, x_all, wa, wb, wo, norm_ffn, w_route, b_route)


def _experts_kernel(item_e, item_blk0, item_nb, src_hbm, dst_hbm, h_hbm, nf_ref, wg_ref, wu_ref, wd_ref,
                    y_hbm, src_s, dst_s, pend_s, xg_ref, yst_ref, xs_ref, hm_ref, wgb_ref, wub_ref, wdb_ref,
                    idx_sem, g_sem, s_sem):
    w = pl.program_id(0)
    s = pl.program_id(1)
    nb = item_nb[w]
    nxt = jnp.minimum(w + 1, MAX_ITEMS - 1)
    nb_next = jnp.where(w + 1 < MAX_ITEMS, item_nb[nxt], 0)

    def table_fetch(tbl_hbm, tbl_s, sem_idx, item):
        slot0 = pl.multiple_of(item_blk0[item] * ROW_BLOCK, ROW_BLOCK)
        return pltpu.make_async_copy(tbl_hbm.at[pl.ds(slot0, ITEM_ROWS)], tbl_s, idx_sem.at[sem_idx])

    def row_copy_in(r, blk):
        return pltpu.make_async_copy(h_hbm.at[pl.ds(src_s[r], 1)], xg_ref.at[pl.ds(r, 1)], g_sem.at[blk])

    def row_copy_out(r, blk):
        return pltpu.make_async_copy(yst_ref.at[pl.ds(r, 1)], y_hbm.at[pl.ds(dst_s[r], 1)], s_sem.at[blk])

    def block_rows(blk):
        return pl.ds(pl.multiple_of(blk * ROW_BLOCK, ROW_BLOCK), ROW_BLOCK)

    def block_copy_wait(sem, blk):
        pltpu.make_async_copy(xg_ref.at[block_rows(blk)], xg_ref.at[block_rows(blk)], sem.at[blk]).wait()

    def issue_gather(n_blocks):
        def issue(blk, carry):
            def one(i, carry2):
                row_copy_in(blk * ROW_BLOCK + i, blk).start()
                return carry2
            return lax.fori_loop(0, ROW_BLOCK, one, carry, unroll=8)
        lax.fori_loop(0, n_blocks, issue, 0)

    def drain_scatter():
        def drain(blk, carry):
            block_copy_wait(s_sem, blk)
            return carry
        lax.fori_loop(0, pend_s[0], drain, 0)
        pend_s[0] = 0

    @pl.when((w == 0) & (s == 0))
    def _():
        pad_rows = pl.ds(0, ROW_BLOCK)
        xg_ref[pad_rows, :] = jnp.zeros((ROW_BLOCK, D_MODEL), F32)
        for row0 in (N_ROWS, DUMP_ROW0):
            cz = pltpu.make_async_copy(xg_ref.at[pad_rows], y_hbm.at[pl.ds(row0, ROW_BLOCK)], idx_sem.at[0])
            cz.start()
            cz.wait()
        pend_s[0] = 0
        ci = table_fetch(src_hbm, src_s, 0, w)
        ci.start()
        ci.wait()
        issue_gather(nb)

    @pl.when((s == 0) & (nb > 0))
    def _():
        table_fetch(dst_hbm, dst_s, 1, w).start()

        def land(blk, carry):
            block_copy_wait(g_sem, blk)
            xs_ref[block_rows(blk), :] = _rms(xg_ref[block_rows(blk), :], nf_ref[...]).astype(BF16)
            return carry
        lax.fori_loop(0, nb, land, 0)

    @pl.when((s == 1) & (nb_next > 0))
    def _():
        table_fetch(src_hbm, src_s, 0, nxt).start()

    @pl.when((s == 2) & (nb_next > 0))
    def _():
        table_fetch(src_hbm, src_s, 0, nxt).wait()
        issue_gather(nb_next)

    def over_row_blocks(fn):
        def four_blocks(j, carry):
            fn(pl.multiple_of(j * (4 * ROW_BLOCK), 4 * ROW_BLOCK), 4 * ROW_BLOCK)
            return carry
        lax.fori_loop(0, nb >> 2, four_blocks, 0)

        @pl.when((nb & 2) == 2)
        def _():
            fn(pl.multiple_of((nb >> 2) * (4 * ROW_BLOCK), 2 * ROW_BLOCK), 2 * ROW_BLOCK)

        @pl.when((nb & 1) == 1)
        def _():
            fn(pl.multiple_of((nb - 1) * ROW_BLOCK, ROW_BLOCK), ROW_BLOCK)

    @pl.when(nb > 0)
    def _():
        wgb_ref[...] = wg_ref[...].astype(BF16)
        wub_ref[...] = wu_ref[...].astype(BF16)
        wdb_ref[pl.ds(pl.multiple_of(s * EXPERT_TN, EXPERT_TN), EXPERT_TN), :] = wd_ref[...].astype(BF16)

        def gate_up(row0, nrows):
            rows = pl.ds(row0, nrows)
            x = xs_ref[rows, :]
            hm_ref[s, rows, :] = (_silu(_bdot(x, wgb_ref[...])) * _bdot(x, wub_ref[...])).astype(BF16)
        over_row_blocks(gate_up)

    @pl.when((s == EXPERT_SLICES - 1) & (nb > 0))
    def _():
        drain_scatter()

        def down(row0, nrows):
            rows = pl.ds(row0, nrows)
            hm = jnp.concatenate([hm_ref[k, rows, :] for k in range(EXPERT_SLICES)], axis=1)
            yst_ref[rows, :] = _bdot(hm, wdb_ref[...])
        over_row_blocks(down)

        table_fetch(dst_hbm, dst_s, 1, w).wait()

        def send(blk, carry):
            def one(i, carry2):
                row_copy_out(blk * ROW_BLOCK + i, blk).start()
                return carry2
            return lax.fori_loop(0, ROW_BLOCK, one, carry, unroll=8)
        lax.fori_loop(0, nb, send, 0)
        pend_s[0] = nb

    @pl.when((w == MAX_ITEMS - 1) & (s == EXPERT_SLICES - 1))
    def _():
        drain_scatter()


def _experts(item_e, item_blk0, item_nb, src_row, dst_row, h_all, norm_ffn, w_gate, w_up, w_down):
    hbm = pl.BlockSpec(memory_space=pltpu.MemorySpace.HBM)

    def w_slice(w, s, inb):
        return jnp.where(inb[w] > 0, s, EXPERT_SLICES - 1)

    grid_spec = pltpu.PrefetchScalarGridSpec(
        num_scalar_prefetch=3,
        grid=(MAX_ITEMS, EXPERT_SLICES),
        in_specs=[
            hbm, hbm, hbm,
            pl.BlockSpec((1, D_MODEL), lambda w, s, ie, ib, inb: (0, 0)),
            pl.BlockSpec((None, D_MODEL, EXPERT_TN), lambda w, s, ie, ib, inb: (ie[w], 0, w_slice(w, s, inb))),
            pl.BlockSpec((None, D_MODEL, EXPERT_TN), lambda w, s, ie, ib, inb: (ie[w], 0, w_slice(w, s, inb))),
            pl.BlockSpec((None, EXPERT_TN, D_MODEL), lambda w, s, ie, ib, inb: (ie[w], w_slice(w, s, inb), 0)),
        ],
        out_specs=hbm,
        scratch_shapes=[
            pltpu.SMEM((ITEM_ROWS,), jnp.int32),
            pltpu.SMEM((ITEM_ROWS,), jnp.int32),
            pltpu.SMEM((1,), jnp.int32),
            pltpu.VMEM((ITEM_ROWS, D_MODEL), F32),
            pltpu.VMEM((ITEM_ROWS, D_MODEL), F32),
            pltpu.VMEM((ITEM_ROWS, D_MODEL), BF16),
            pltpu.VMEM((EXPERT_SLICES, ITEM_ROWS, EXPERT_TN), BF16),
            pltpu.VMEM((D_MODEL, EXPERT_TN), BF16),
            pltpu.VMEM((D_MODEL, EXPERT_TN), BF16),
            pltpu.VMEM((D_EXPERT, D_MODEL), BF16),
            pltpu.SemaphoreType.DMA((2,)),
            pltpu.SemaphoreType.DMA((ITEM_BLOCKS,)),
            pltpu.SemaphoreType.DMA((ITEM_BLOCKS,)),
        ],
    )
    return pl.pallas_call(
        _experts_kernel,
        out_shape=jax.ShapeDtypeStruct((PAIR_ROWS, D_MODEL), F32),
        grid_spec=grid_spec,
        compiler_params=pltpu.CompilerParams(
            dimension_semantics=("arbitrary", "arbitrary"), vmem_limit_bytes=VMEM_LIMIT),
        name="experts",
    )(item_e, item_blk0, item_nb, src_row, dst_row, h_all, norm_ffn, w_gate, w_up, w_down)


def _dispatch_tables(eid):
    flat_e = eid.reshape(-1)
    onehot = (flat_e[:, None] == jnp.arange(N_EXPERTS, dtype=jnp.int32)[None, :]).astype(jnp.int32)
    rank = jnp.take_along_axis(jnp.cumsum(onehot, axis=0) - onehot, flat_e[:, None], axis=1)[:, 0]
    counts = jnp.sum(onehot, axis=0)
    nblk = (counts + ROW_BLOCK - 1) // ROW_BLOCK
    blk0 = jnp.cumsum(nblk) - nblk
    slot = blk0[flat_e] * ROW_BLOCK + rank
    slot_pair = jnp.full((SLOT_TABLE,), N_PAIRS, jnp.int32).at[slot].set(jnp.arange(N_PAIRS, dtype=jnp.int32))
    slot_id = jnp.arange(SLOT_TABLE, dtype=jnp.int32)
    src_row = jnp.minimum(slot_pair // TOP_K, N_ROWS - 1)
    dst_row = jnp.where(slot_pair < N_PAIRS, (slot_pair % TOP_K) * PAIR_REGION + slot_pair // TOP_K,
                        DUMP_ROW0 + slot_id % ROW_BLOCK)
    nitem = (nblk + ITEM_BLOCKS - 1) // ITEM_BLOCKS
    item_end = jnp.cumsum(nitem)
    total = item_end[-1]
    idx = jnp.arange(MAX_ITEMS, dtype=jnp.int32)
    live = idx < total
    e_of = jnp.minimum(jnp.searchsorted(item_end, jnp.minimum(idx, total - 1), side='right'),
                       N_EXPERTS - 1).astype(jnp.int32)
    local = jnp.minimum(idx, total - 1) - (item_end - nitem)[e_of]
    item_blk0 = (blk0[e_of] + local * ITEM_BLOCKS).astype(jnp.int32)
    item_nb = jnp.where(live, jnp.clip(nblk[e_of] - local * ITEM_BLOCKS, 0, ITEM_BLOCKS), 0).astype(jnp.int32)
    return e_of, item_blk0, item_nb, src_row, dst_row


def _final_rows(h, y0, y1, ew, p, wg_ref, wp_ref, np_ref, fn_ref):
    h = h + (ew[:, 0:1] * y0 + ew[:, 1:2] * y1)
    gate = _sigmoid(_bdot(_rms(h, np_ref[...]).astype(BF16), wg_ref[...]))
    h = h + gate * _bdot(p.astype(BF16), wp_ref[...])
    return _rms(h, fn_ref[...])


def _final_kernel(h_ref, y0_ref, y1_ref, ew_ref, p_ref, hs_ref, y0s_ref, y1s_ref, ews_ref, ps_ref,
                  wg_hbm, wp_hbm, np_ref, fn_ref, op_ref, os_ref, wg_ref, wp_ref, sem):
    i = pl.program_id(0)

    @pl.when(i == 0)
    def _():
        c0 = pltpu.make_async_copy(wg_hbm, wg_ref, sem.at[0])
        c1 = pltpu.make_async_copy(wp_hbm, wp_ref, sem.at[1])
        c0.start()
        c1.start()
        c0.wait()
        c1.wait()

    @pl.when(i < FIN_PROMPT_STEPS)
    def _():
        op_ref[...] = _final_rows(h_ref[...], y0_ref[...], y1_ref[...], ew_ref[...], p_ref[...],
                                  wg_ref, wp_ref, np_ref, fn_ref)

    @pl.when(i == FIN_PROMPT_STEPS)
    def _():
        os_ref[...] = _final_rows(hs_ref[...], y0s_ref[...], y1s_ref[...], ews_ref[...], ps_ref[...],
                                  wg_ref, wp_ref, np_ref, fn_ref)


def _final(h_all, ypairs, ew, p_prompt, p_sample, w_gate, w_proj, norm_ple, final_norm):
    last = FIN_PROMPT_STEPS - 1
    prow = lambda i: (jnp.minimum(i, last), 0)
    const2 = lambda i: (0, 0)
    hbm = pl.BlockSpec(memory_space=pltpu.MemorySpace.HBM)
    sblk = N_PROMPT // N_SAMPLE
    pact = pl.BlockSpec((FIN_TM, D_MODEL), prow)
    sact = lambda off: pl.BlockSpec((N_SAMPLE, D_MODEL), lambda i: (off + sblk, 0))
    return pl.pallas_call(
        _final_kernel,
        out_shape=(jax.ShapeDtypeStruct((N_PROMPT, D_MODEL), F32), jax.ShapeDtypeStruct((N_SAMPLE, D_MODEL), F32)),
        grid=(FIN_PROMPT_STEPS + 1,),
        in_specs=[pact, pact,
                  pl.BlockSpec((FIN_TM, D_MODEL), lambda i: (jnp.minimum(i, last) + PAIR_REGION // FIN_TM, 0)),
                  pl.BlockSpec((FIN_TM, LANES), prow),
                  pl.BlockSpec((FIN_TM, PLE_DIM), prow),
                  sact(0), sact(0), sact(PAIR_REGION // N_SAMPLE),
                  pl.BlockSpec((N_SAMPLE, LANES), lambda i: (sblk, 0)),
                  pl.BlockSpec((N_SAMPLE, PLE_DIM), const2),
                  hbm, hbm,
                  pl.BlockSpec((1, D_MODEL), const2),
                  pl.BlockSpec((1, D_MODEL), const2)],
        out_specs=(pact, pl.BlockSpec((N_SAMPLE, D_MODEL), const2)),
        scratch_shapes=[pltpu.VMEM((D_MODEL, D_MODEL), BF16), pltpu.VMEM((PLE_DIM, D_MODEL), BF16),
                        pltpu.SemaphoreType.DMA((2,))],
        compiler_params=pltpu.CompilerParams(
            dimension_semantics=("arbitrary",), vmem_limit_bytes=VMEM_LIMIT),
        name="final",
    )(h_all, ypairs, ypairs, ew, p_prompt, h_all, ypairs, ypairs, ew, p_sample, w_gate, w_proj, norm_ple, final_norm)


def kernel(x_prompt, x_sample, state_conv, p_prompt, p_sample, norm_mix, w_in, ln_v_g, ln_v_b, w_spatial,
           b_spatial, w_proj_a, conv_w, conv_b, ln_c_g, ln_c_b, w_proj_b, w_out, norm_ffn, w_router_group,
           b_router_group, w_router_expert, b_router_expert, w_exp_gate, w_exp_up, w_exp_down, norm_ple,
           w_ple_gate, w_ple_proj, final_norm):
    assert w_in.shape[0] == 1, "single layer"
    vec = lambda v: v.reshape(1, -1)
    xn, x_all = _xnorm(x_prompt.reshape(N_PROMPT, D_MODEL), x_sample.reshape(N_SAMPLE, D_MODEL), vec(norm_mix[0]))

    w_s, b_s = w_spatial[0], b_spatial[0]
    wmix = jnp.tril(w_s).astype(BF16)
    bmix = jnp.repeat(b_s.T, A_GROUP_DIM, axis=1)
    w00 = jnp.repeat(w_s[:, 0, 0], A_GROUP_DIM).reshape(1, D_MODEL)
    b00 = jnp.repeat(b_s[:, 0], A_GROUP_DIM).reshape(1, D_MODEL)

    act = _in_proj(xn, w_in[0])

    cw = conv_w[0]
    new_conv_sample, conv_s = _sconv(state_conv, act, cw, vec(conv_b[0]))
    cw_pad = jnp.concatenate([cw, jnp.zeros((1, D_MODEL), F32)], axis=0)
    wi = jnp.arange(WIN_ROWS, dtype=jnp.int32)[None, :, None]
    wj = jnp.arange(WIN_PAD, dtype=jnp.int32)[None, None, :]
    wr = jnp.arange(1, SUBLANES, dtype=jnp.int32)[:, None, None]
    sel = (wj == wi + wr).astype(BF16)
    a_in, b_in, vn_s = _seqmix(act, conv_s, wmix, bmix, w00, b00, vec(ln_v_g[0]), vec(ln_v_b[0]), cw_pad,
                               vec(conv_b[0]), vec(ln_c_g[0]), vec(ln_c_b[0]), sel)

    w_route = jnp.zeros((D_MODEL, LANES), F32)
    w_route = w_route.at[:, :MOE_GROUPS].set(w_router_group[0])
    w_route = w_route.at[:, MOE_GROUPS:MOE_GROUPS + N_EXPERTS].set(w_router_expert[0])
    b_route = jnp.zeros((1, LANES), F32)
    b_route = b_route.at[0, :MOE_GROUPS].set(b_router_group[0])
    b_route = b_route.at[0, MOE_GROUPS:MOE_GROUPS + N_EXPERTS].set(b_router_expert[0])

    wa = _cast_bf16(w_proj_a[0], 512)
    wb = _cast_bf16(w_proj_b[0], 512)
    wo = _cast_bf16(w_out[0], 512)
    h_all, eid, ew = _mixout(a_in, b_in, act, x_all, wa, wb, wo, vec(norm_ffn[0]),
                             w_route.astype(BF16), b_route)

    tables = _dispatch_tables(eid[:, :TOP_K])
    ypairs = _experts(*tables, h_all, vec(norm_ffn[0]), w_exp_gate[0], w_exp_up[0], w_exp_down[0])

    wpg = _cast_bf16(w_ple_gate[0], 512)
    wpp = _cast_bf16(w_ple_proj[0], PLE_DIM)
    y_prompt, y_sample = _final(h_all, ypairs, ew, p_prompt[0].reshape(N_PROMPT, PLE_DIM),
                                p_sample[0].reshape(N_SAMPLE, PLE_DIM), wpg, wpp, vec(norm_ple[0]), vec(final_norm))
    y_prompt = y_prompt.reshape(N_PROMPT_SEQ, SEQ, D_MODEL)
    y_sample = y_sample.reshape(N_SAMPLE, 1, D_MODEL)
    glu_cols = slice(ACT_GLU * D_MODEL, (ACT_GLU + 1) * D_MODEL)
    new_conv_prompt = jnp.stack([act[(b + 1) * SEQ - CONV_HIST:(b + 1) * SEQ, glu_cols]
                                 for b in range(N_PROMPT_SEQ)])[None]
    return (y_prompt, y_sample, new_conv_prompt, new_conv_sample, vn_s.reshape(1, N_SAMPLE, 1, D_MODEL))
```

```python
import jax
import jax.numpy as jnp
from jax import lax
from jax.experimental import pallas as pl
from jax.experimental.pallas import tpu as pltpu

F32 = jnp.float32
BF16 = jnp.bfloat16

D_MODEL = 2048
N_PROMPT_SEQ = 4
SEQ = 2048
N_PROMPT = N_PROMPT_SEQ * SEQ
N_SAMPLE = 128
N_ROWS = N_PROMPT + N_SAMPLE
CHUNK = 128
N_CHUNKS = N_ROWS // CHUNK
CHUNKS_PER_SEQ = SEQ // CHUNK
A_GROUPS = 8
A_GROUP_DIM = D_MODEL // A_GROUPS
CONV_WIDTH = 31
CONV_HIST = CONV_WIDTH - 1
HIST_BLOCK = 32
MOE_GROUPS = 4
EXPERTS_PER_GROUP = 8
N_EXPERTS = MOE_GROUPS * EXPERTS_PER_GROUP
TOP_K = 2
D_EXPERT = D_MODEL // 2
PLE_DIM = 256
EPS = 1e-6
LANES = 128

XN_TM = 512
IN_TM = 1664
IN_TN = 512
IN_COL_TILES = D_MODEL // IN_TN
ACT_BLOCKS = 5
ACT_GU, ACT_GV, ACT_GLU, ACT_SGA, ACT_SGB = range(ACT_BLOCKS)
MIX_TM = 832
FIN_TM = 256
FIN_PROMPT_STEPS = N_PROMPT // FIN_TM
ROW_BLOCK = 128
SUBLANES = 8
BLOCK_TILES = ROW_BLOCK // SUBLANES
N_PAIRS = N_ROWS * TOP_K
MAX_BLOCKS = N_PAIRS // ROW_BLOCK + N_EXPERTS
ITEM_BLOCKS = 8
ITEM_ROWS = ITEM_BLOCKS * ROW_BLOCK
MAX_ITEMS = (MAX_BLOCKS + (ITEM_BLOCKS - 1) * N_EXPERTS) // ITEM_BLOCKS
N_SLOTS = MAX_BLOCKS * ROW_BLOCK
SLOT_TABLE = N_SLOTS + ITEM_ROWS
EXPERT_SLICES = 4
EXPERT_TN = D_EXPERT // EXPERT_SLICES
PAIR_REGION = 8448
DUMP_ROW0 = PAIR_REGION + N_ROWS
PAIR_ROWS = DUMP_ROW0 + ROW_BLOCK

VMEM_LIMIT = 56 * 1024 * 1024


def _rms(x, g):
    return x * lax.rsqrt(jnp.mean(x * x, axis=-1, keepdims=True) + EPS) * g


def _layer_norm(x, g, b):
    mu = jnp.mean(x, axis=-1, keepdims=True)
    xc = x - mu
    return xc * lax.rsqrt(jnp.mean(xc * xc, axis=-1, keepdims=True) + EPS) * g + b


def _sigmoid(x):
    return 1.0 / (1.0 + jnp.exp(-x))


def _silu(x):
    return x * _sigmoid(x)


def _gelu(x):
    return jax.nn.gelu(x, approximate=True)


def _bdot(a, b):
    return jnp.dot(a, b, preferred_element_type=F32)


def _cast_kernel(w_ref, o_ref):
    o_ref[...] = w_ref[...].astype(BF16)


def _cast_bf16(w, rows):
    k, n = w.shape
    return pl.pallas_call(
        _cast_kernel,
        out_shape=jax.ShapeDtypeStruct((k, n), BF16),
        grid=(k // rows,),
        in_specs=[pl.BlockSpec((rows, n), lambda i: (i, 0))],
        out_specs=pl.BlockSpec((rows, n), lambda i: (i, 0)),
        name="cast_bf16",
    )(w)


def _xnorm_kernel(xp_ref, xs_ref, nm_ref, xn_ref, xall_ref):
    i = pl.program_id(0)

    @pl.when(i < N_PROMPT // XN_TM)
    def _():
        x = xp_ref[...]
        xall_ref[...] = x
        xn_ref[...] = _rms(x, nm_ref[...]).astype(BF16)

    @pl.when(i == N_PROMPT // XN_TM)
    def _():
        x = xs_ref[...]
        xall_ref[0:N_SAMPLE, :] = x
        xall_ref[N_SAMPLE:, :] = jnp.zeros((XN_TM - N_SAMPLE, D_MODEL), F32)
        xn_ref[0:N_SAMPLE, :] = _rms(x, nm_ref[...]).astype(BF16)
        xn_ref[N_SAMPLE:, :] = jnp.zeros((XN_TM - N_SAMPLE, D_MODEL), BF16)


def _xnorm(x_prompt, x_sample, norm_mix):
    last = N_PROMPT // XN_TM - 1
    row = lambda i: (i, 0)
    return pl.pallas_call(
        _xnorm_kernel,
        out_shape=(jax.ShapeDtypeStruct((N_ROWS, D_MODEL), BF16), jax.ShapeDtypeStruct((N_ROWS, D_MODEL), F32)),
        grid=(N_PROMPT // XN_TM + 1,),
        in_specs=[pl.BlockSpec((XN_TM, D_MODEL), lambda i: (jnp.minimum(i, last), 0)),
                  pl.BlockSpec((N_SAMPLE, D_MODEL), lambda i: (0, 0)),
                  pl.BlockSpec((1, D_MODEL), lambda i: (0, 0))],
        out_specs=(pl.BlockSpec((XN_TM, D_MODEL), row), pl.BlockSpec((XN_TM, D_MODEL), row)),
        name="xnorm",
    )(x_prompt, x_sample, norm_mix)


def _in_w_col(j):
    t = IN_COL_TILES
    jj = j - 2 * t
    glu_col = 2 * t + (jj % 2) * t + jj // 2
    return jnp.where((j >= 2 * t) & (j < 4 * t), glu_col, j)


def _in_out_col(j):
    t = IN_COL_TILES
    return jnp.where(j < 2 * t, j, jnp.where(j < 4 * t, 2 * t + (j - 2 * t) // 2, j - t))


def _in_proj_kernel(xn_ref, w_ref, o_ref, val_ref):
    j = pl.program_id(1)
    t = IN_COL_TILES

    def z():
        return _bdot(xn_ref[...], w_ref[...].astype(BF16))

    @pl.when(j < 2 * t)
    def _():
        o_ref[...] = _gelu(z())

    @pl.when((j >= 2 * t) & (j < 4 * t) & (j % 2 == 0))
    def _():
        val_ref[...] = z()

    @pl.when((j >= 2 * t) & (j < 4 * t) & (j % 2 == 1))
    def _():
        o_ref[...] = val_ref[...] * _sigmoid(z())

    @pl.when(j >= 4 * t)
    def _():
        o_ref[...] = _sigmoid(z())


def _in_proj(xn, w_in):
    return pl.pallas_call(
        _in_proj_kernel,
        out_shape=jax.ShapeDtypeStruct((N_ROWS, ACT_BLOCKS * D_MODEL), F32),
        grid=(N_ROWS // IN_TM, 6 * IN_COL_TILES),
        in_specs=[
            pl.BlockSpec((IN_TM, D_MODEL), lambda i, j: (i, 0)),
            pl.BlockSpec((D_MODEL, IN_TN), lambda i, j: (0, _in_w_col(j))),
        ],
        out_specs=pl.BlockSpec((IN_TM, IN_TN), lambda i, j: (i, _in_out_col(j))),
        scratch_shapes=[pltpu.VMEM((IN_TM, IN_TN), F32)],
        compiler_params=pltpu.CompilerParams(
            dimension_semantics=("arbitrary", "arbitrary"), vmem_limit_bytes=VMEM_LIMIT),
        name="in_proj",
    )(xn, w_in)


SCONV_B = 16


def _sconv_kernel(st_ref, glu_ref, w_ref, wl_ref, cb_ref, new_ref, conv_ref):
    st = st_ref[0]
    glu = glu_ref[...]
    new_ref[0, :, 0:CONV_HIST - 1, :] = st[:, 1:CONV_HIST, :]
    new_ref[0, :, CONV_HIST - 1, :] = glu
    acc = st[:, 0, :] * w_ref[0:1, :]
    for k in range(1, CONV_HIST):
        acc = acc + st[:, k, :] * w_ref[k:k + 1, :]
    conv_ref[...] = acc + glu * wl_ref[...] + cb_ref[...]


def _sconv(state, act, conv_w, conv_b):
    return pl.pallas_call(
        _sconv_kernel,
        out_shape=(
            jax.ShapeDtypeStruct((1, N_SAMPLE, CONV_HIST, D_MODEL), F32),
            jax.ShapeDtypeStruct((N_SAMPLE, D_MODEL), F32),
        ),
        grid=(N_SAMPLE // SCONV_B,),
        in_specs=[
            pl.BlockSpec((1, SCONV_B, CONV_HIST, D_MODEL), lambda i: (0, i, 0, 0)),
            pl.BlockSpec((SCONV_B, D_MODEL), lambda i: (N_PROMPT // SCONV_B + i, ACT_GLU)),
            pl.BlockSpec((CONV_HIST, D_MODEL), lambda i: (0, 0)),
            pl.BlockSpec((1, D_MODEL), lambda i: (0, 0)),
            pl.BlockSpec((1, D_MODEL), lambda i: (0, 0)),
        ],
        out_specs=(
            pl.BlockSpec((1, SCONV_B, CONV_HIST, D_MODEL), lambda i: (0, i, 0, 0)),
            pl.BlockSpec((SCONV_B, D_MODEL), lambda i: (i, 0)),
        ),
        name="sconv",
    )(state, act, conv_w[:CONV_HIST], conv_w[CONV_HIST:], conv_b)


CONV_STRIP = 256
WIN_ROWS = HIST_BLOCK + CHUNK
WIN_PAD = 256


def _seqmix_kernel(gv_ref, gu_ref, cur_ref, prev_ref, sga_ref, sgb_ref, convs_ref, wmix_ref, bmix_ref, w00_ref,
                   b00_ref, lvg_ref, lvb_ref, cw_ref, cb_ref, lcg_ref, lcb_ref, sel_ref, wa_hbm, wb_hbm,
                   m_ref, vns_ref, win_ref, shift_ref, conv_ref, a_ref, b_ref, wa_ref, wb_ref, sem):
    c = pl.program_id(0)
    is_sample = c == N_CHUNKS - 1

    @pl.when(c == 0)
    def _():
        ca = pltpu.make_async_copy(wa_hbm, wa_ref, sem.at[0])
        cb = pltpu.make_async_copy(wb_hbm, wb_ref, sem.at[1])
        ca.start()
        cb.start()
        ca.wait()
        cb.wait()
        win_ref[WIN_ROWS:, :] = jnp.zeros((WIN_PAD - WIN_ROWS, D_MODEL), BF16)

    fresh = (c % CHUNKS_PER_SEQ) == 0
    win_ref[0:HIST_BLOCK, :] = jnp.where(fresh, 0.0, prev_ref[...]).astype(BF16)
    win_ref[HIST_BLOCK:WIN_ROWS, :] = cur_ref[...].astype(BF16)
    shift_ref[0] = win_ref[0:WIN_ROWS, :].astype(F32)
    for r in range(1, SUBLANES):
        shift_ref[r] = _bdot(sel_ref[r - 1], win_ref[...])
    off = HIST_BLOCK - CONV_HIST

    vn = _layer_norm(gv_ref[...], lvg_ref[...], lvb_ref[...])
    vns_ref[...] = vn

    vnb = vn.astype(BF16)
    for g in range(A_GROUPS):
        sl = slice(g * A_GROUP_DIM, (g + 1) * A_GROUP_DIM)
        s_chunk = _bdot(wmix_ref[g], vnb[:, sl]) + bmix_ref[:, sl]
        s_first = vn[:, sl] * w00_ref[:, sl] + b00_ref[:, sl]
        a_ref[:, sl] = (gu_ref[:, sl] * jnp.where(is_sample, s_first, s_chunk)).astype(BF16)
    pa = _bdot(a_ref[...], wa_ref[...])


    def tap(k, cs):
        q, r = divmod(off + k, SUBLANES)
        return (shift_ref[r, SUBLANES * q:SUBLANES * q + CHUNK, cs]
                * cw_ref[k:k + 1, cs].astype(BF16).astype(F32))

    for s0 in range(0, D_MODEL, CONV_STRIP):
        cs = slice(s0, s0 + CONV_STRIP)
        acc = tap(0, cs)
        for k in range(1, CONV_WIDTH):
            acc = acc + tap(k, cs)
        conv_ref[:, cs] = acc + cb_ref[:, cs]
    conv = jnp.where(is_sample, convs_ref[...], conv_ref[...])
    b_ref[...] = _silu(_layer_norm(conv, lcg_ref[...], lcb_ref[...])).astype(BF16)

    pb = _bdot(b_ref[...], wb_ref[...])
    m_ref[...] = (sga_ref[...] * pa + sgb_ref[...] * pb).astype(BF16)


def _seqmix(act, conv_s, wmix, bmix, w00, b00, ln_v_g, ln_v_b, conv_w, conv_b, ln_c_g, ln_c_b, sel, wa, wb):
    row = lambda c: (c, 0)
    const2 = lambda c: (0, 0)
    vec = pl.BlockSpec((1, D_MODEL), const2)
    hbm = pl.BlockSpec(memory_space=pltpu.MemorySpace.HBM)
    hist_per_chunk = CHUNK // HIST_BLOCK
    act_block = lambda blk: pl.BlockSpec((CHUNK, D_MODEL), lambda c: (c, blk))
    return pl.pallas_call(
        _seqmix_kernel,
        out_shape=(
            jax.ShapeDtypeStruct((N_ROWS, D_MODEL), BF16),
            jax.ShapeDtypeStruct((N_SAMPLE, D_MODEL), F32),
        ),
        grid=(N_CHUNKS,),
        in_specs=[
            act_block(ACT_GV), act_block(ACT_GU), act_block(ACT_GLU),
            pl.BlockSpec((HIST_BLOCK, D_MODEL), lambda c: (jnp.maximum(c * hist_per_chunk - 1, 0), ACT_GLU)),
            act_block(ACT_SGA), act_block(ACT_SGB),
            pl.BlockSpec((N_SAMPLE, D_MODEL), const2),
            pl.BlockSpec((A_GROUPS, CHUNK, CHUNK), lambda c: (0, 0, 0)),
            pl.BlockSpec((CHUNK, D_MODEL), const2),
            vec, vec, vec, vec,
            pl.BlockSpec((CONV_WIDTH + 1, D_MODEL), const2),
            vec, vec, vec,
            pl.BlockSpec((SUBLANES - 1, WIN_ROWS, WIN_PAD), lambda c: (0, 0, 0)),
            hbm, hbm,
        ],
        out_specs=(
            pl.BlockSpec((CHUNK, D_MODEL), row),
            pl.BlockSpec((N_SAMPLE, D_MODEL), const2),
        ),
        scratch_shapes=[pltpu.VMEM((WIN_PAD, D_MODEL), BF16), pltpu.VMEM((SUBLANES, WIN_ROWS, D_MODEL), F32),
                        pltpu.VMEM((CHUNK, D_MODEL), F32),
                        pltpu.VMEM((CHUNK, D_MODEL), BF16), pltpu.VMEM((CHUNK, D_MODEL), BF16),
                        pltpu.VMEM((D_MODEL, D_MODEL), BF16), pltpu.VMEM((D_MODEL, D_MODEL), BF16),
                        pltpu.SemaphoreType.DMA((2,))],
        compiler_params=pltpu.CompilerParams(
            dimension_semantics=("arbitrary",), vmem_limit_bytes=VMEM_LIMIT),
        name="seqmix",
    )(act, act, act, act, act, act, conv_s, wmix, bmix, w00, b00, ln_v_g, ln_v_b, conv_w, conv_b, ln_c_g, ln_c_b,
      sel, wa, wb)


def _route(logits):
    col = lax.broadcasted_iota(jnp.int32, logits.shape, 1).astype(F32)
    neg = jnp.float32(-jnp.inf)
    big = jnp.float32(1e9)
    lg = jnp.where(col < MOE_GROUPS, logits, neg)
    gmax = jnp.max(lg, axis=-1, keepdims=True)
    gi = jnp.min(jnp.where(lg == gmax, col, big), axis=-1, keepdims=True)
    gw = 1.0 / jnp.sum(jnp.exp(lg - gmax), axis=-1, keepdims=True)
    lo = MOE_GROUPS + gi * EXPERTS_PER_GROUP
    le = jnp.where((col >= lo) & (col < lo + EXPERTS_PER_GROUP), logits, neg)
    m1 = jnp.max(le, axis=-1, keepdims=True)
    i1 = jnp.min(jnp.where(le == m1, col, big), axis=-1, keepdims=True)
    le2 = jnp.where(col == i1, neg, le)
    m2 = jnp.max(le2, axis=-1, keepdims=True)
    i2 = jnp.min(jnp.where(le2 == m2, col, big), axis=-1, keepdims=True)
    e = jnp.exp(m2 - m1)
    w1 = gw / (1.0 + e)
    w2 = gw * e / (1.0 + e)
    eid = jnp.where(col == 0, i1 - MOE_GROUPS, jnp.where(col == 1, i2 - MOE_GROUPS, 0.0)).astype(jnp.int32)
    ew = jnp.where(col == 0, w1, jnp.where(col == 1, w2, 0.0))
    return eid, ew


def _mixout_kernel(m_ref, x_ref, wo_hbm, nf_ref, wr_ref, br_ref, h_ref, eid_ref, ew_ref, wo_ref, sem):
    @pl.when(pl.program_id(0) == 0)
    def _():
        cp = pltpu.make_async_copy(wo_hbm, wo_ref, sem.at[0])
        cp.start()
        cp.wait()

    h = x_ref[...] + _bdot(m_ref[...], wo_ref[...])
    h_ref[...] = h
    hn = _rms(h, nf_ref[...])
    logits = _bdot(hn.astype(BF16), wr_ref[...]) + br_ref[...]
    eid, ew = _route(logits)
    eid_ref[...] = eid
    ew_ref[...] = ew


def _mixout(m_all, x_all, wo, norm_ffn, w_route, b_route):
    row = lambda i: (i, 0)
    const2 = lambda i: (0, 0)
    rows = pl.BlockSpec((MIX_TM, D_MODEL), row)
    hbm = pl.BlockSpec(memory_space=pltpu.MemorySpace.HBM)
    return pl.pallas_call(
        _mixout_kernel,
        out_shape=(
            jax.ShapeDtypeStruct((N_ROWS, D_MODEL), F32),
            jax.ShapeDtypeStruct((N_ROWS, LANES), jnp.int32),
            jax.ShapeDtypeStruct((N_ROWS, LANES), F32),
        ),
        grid=(N_ROWS // MIX_TM,),
        in_specs=[rows, rows, hbm,
                  pl.BlockSpec((1, D_MODEL), const2),
                  pl.BlockSpec((D_MODEL, LANES), const2),
                  pl.BlockSpec((1, LANES), const2)],
        out_specs=(rows, pl.BlockSpec((MIX_TM, LANES), row), pl.BlockSpec((MIX_TM, LANES), row)),
        scratch_shapes=[pltpu.VMEM((D_MODEL, D_MODEL), BF16), pltpu.SemaphoreType.DMA((1,))],
        compiler_params=pltpu.CompilerParams(
            dimension_semantics=("arbitrary",), vmem_limit_bytes=VMEM_LIMIT),
        name="mixout",
    )(m_all, x_all, wo, norm_ffn, w_route, b_route)


def _experts_kernel(item_e, item_blk0, item_nb, src_hbm, dst_hbm, h_hbm, nf_ref, wg_ref, wu_ref, wd_ref,
                    y_hbm, src_s, dst_s, pend_s, xg_ref, yst_ref, xs_ref, hm_ref, wgb_ref, wub_ref, wdb_ref,
                    idx_sem, g_sem, s_sem):
    w = pl.program_id(0)
    s = pl.program_id(1)
    nb = item_nb[w]
    nxt = jnp.minimum(w + 1, MAX_ITEMS - 1)
    nb_next = jnp.where(w + 1 < MAX_ITEMS, item_nb[nxt], 0)

    def table_fetch(tbl_hbm, tbl_s, sem_idx, item):
        slot0 = pl.multiple_of(item_blk0[item] * ROW_BLOCK, ROW_BLOCK)
        return pltpu.make_async_copy(tbl_hbm.at[pl.ds(slot0, ITEM_ROWS)], tbl_s, idx_sem.at[sem_idx])

    def row_copy_in(t, u, blk):
        src = src_s[t * SUBLANES + u]
        return pltpu.make_async_copy(h_hbm.at[src >> 3, pl.ds(src & (SUBLANES - 1), 1)],
                                     xg_ref.at[t, pl.ds(u, 1)], g_sem.at[blk])

    def row_copy_out(t, u, blk):
        dst = dst_s[t * SUBLANES + u]
        return pltpu.make_async_copy(yst_ref.at[t, pl.ds(u, 1)],
                                     y_hbm.at[dst >> 3, pl.ds(dst & (SUBLANES - 1), 1)], s_sem.at[blk])

    def block_rows(blk):
        return pl.ds(pl.multiple_of(blk * ROW_BLOCK, ROW_BLOCK), ROW_BLOCK)

    def block_tiles(blk):
        return pl.ds(pl.multiple_of(blk * BLOCK_TILES, BLOCK_TILES), BLOCK_TILES)

    def block_copy_wait(sem, blk):
        pltpu.make_async_copy(xg_ref.at[block_tiles(blk)], xg_ref.at[block_tiles(blk)], sem.at[blk]).wait()

    def for_each_row(n_blocks, row_copy):
        def tile(t, carry):
            for u in range(SUBLANES):
                row_copy(t, u, lax.shift_right_logical(t, BLOCK_TILES.bit_length() - 1)).start()
            return carry
        lax.fori_loop(0, n_blocks * BLOCK_TILES, tile, 0)

    def issue_gather(n_blocks):
        for_each_row(n_blocks, row_copy_in)

    def drain_scatter():
        def drain(blk, carry):
            block_copy_wait(s_sem, blk)
            return carry
        lax.fori_loop(0, pend_s[0], drain, 0)
        pend_s[0] = 0

    @pl.when((w == 0) & (s == 0))
    def _():
        pad_tiles = pl.ds(0, BLOCK_TILES)
        xg_ref[pad_tiles] = jnp.zeros((BLOCK_TILES, SUBLANES, D_MODEL), F32)
        for row0 in (N_ROWS, DUMP_ROW0):
            cz = pltpu.make_async_copy(xg_ref.at[pad_tiles], y_hbm.at[pl.ds(row0 // SUBLANES, BLOCK_TILES)],
                                       idx_sem.at[0])
            cz.start()
            cz.wait()
        pend_s[0] = 0
        ci = table_fetch(src_hbm, src_s, 0, w)
        ci.start()
        ci.wait()
        issue_gather(nb)

    @pl.when((s == 0) & (nb > 0))
    def _():
        table_fetch(dst_hbm, dst_s, 1, w).start()

        def land(blk, carry):
            block_copy_wait(g_sem, blk)
            x = xg_ref[block_tiles(blk)].reshape(ROW_BLOCK, D_MODEL)
            xs_ref[block_rows(blk), :] = _rms(x, nf_ref[...]).astype(BF16)
            return carry
        lax.fori_loop(0, nb, land, 0)

    @pl.when((s == 1) & (nb_next > 0))
    def _():
        table_fetch(src_hbm, src_s, 0, nxt).start()

    @pl.when((s == 2) & (nb_next > 0))
    def _():
        table_fetch(src_hbm, src_s, 0, nxt).wait()
        issue_gather(nb_next)

    def over_row_blocks(fn):
        def four_blocks(j, carry):
            fn(pl.multiple_of(j * (4 * ROW_BLOCK), 4 * ROW_BLOCK), 4 * ROW_BLOCK)
            return carry
        lax.fori_loop(0, nb >> 2, four_blocks, 0)

        @pl.when((nb & 2) == 2)
        def _():
            fn(pl.multiple_of((nb >> 2) * (4 * ROW_BLOCK), 2 * ROW_BLOCK), 2 * ROW_BLOCK)

        @pl.when((nb & 1) == 1)
        def _():
            fn(pl.multiple_of((nb - 1) * ROW_BLOCK, ROW_BLOCK), ROW_BLOCK)

    @pl.when(nb > 0)
    def _():
        wgb_ref[...] = wg_ref[...].astype(BF16)
        wub_ref[...] = wu_ref[...].astype(BF16)
        wdb_ref[pl.ds(pl.multiple_of(s * EXPERT_TN, EXPERT_TN), EXPERT_TN), :] = wd_ref[...].astype(BF16)

        def gate_up(row0, nrows):
            rows = pl.ds(row0, nrows)
            x = xs_ref[rows, :]
            hm_ref[s, rows, :] = (_silu(_bdot(x, wgb_ref[...])) * _bdot(x, wub_ref[...])).astype(BF16)
        over_row_blocks(gate_up)

    @pl.when((s == EXPERT_SLICES - 1) & (nb > 0))
    def _():
        drain_scatter()

        def down(row0, nrows):
            rows = pl.ds(row0, nrows)
            hm = jnp.concatenate([hm_ref[k, rows, :] for k in range(EXPERT_SLICES)], axis=1)
            tiles = pl.ds(pl.multiple_of(row0 // SUBLANES, BLOCK_TILES), nrows // SUBLANES)
            yst_ref[tiles] = _bdot(hm, wdb_ref[...]).reshape(nrows // SUBLANES, SUBLANES, D_MODEL)
        over_row_blocks(down)

        table_fetch(dst_hbm, dst_s, 1, w).wait()

        for_each_row(nb, row_copy_out)
        pend_s[0] = nb

    @pl.when((w == MAX_ITEMS - 1) & (s == EXPERT_SLICES - 1))
    def _():
        drain_scatter()


def _experts(item_e, item_blk0, item_nb, src_row, dst_row, h_all, norm_ffn, w_gate, w_up, w_down):
    hbm = pl.BlockSpec(memory_space=pltpu.MemorySpace.HBM)

    def w_slice(w, s, inb):
        return jnp.where(inb[w] > 0, s, EXPERT_SLICES - 1)

    grid_spec = pltpu.PrefetchScalarGridSpec(
        num_scalar_prefetch=3,
        grid=(MAX_ITEMS, EXPERT_SLICES),
        in_specs=[
            hbm, hbm, hbm,
            pl.BlockSpec((1, D_MODEL), lambda w, s, ie, ib, inb: (0, 0)),
            pl.BlockSpec((None, D_MODEL, EXPERT_TN), lambda w, s, ie, ib, inb: (ie[w], 0, w_slice(w, s, inb))),
            pl.BlockSpec((None, D_MODEL, EXPERT_TN), lambda w, s, ie, ib, inb: (ie[w], 0, w_slice(w, s, inb))),
            pl.BlockSpec((None, EXPERT_TN, D_MODEL), lambda w, s, ie, ib, inb: (ie[w], w_slice(w, s, inb), 0)),
        ],
        out_specs=hbm,
        scratch_shapes=[
            pltpu.SMEM((ITEM_ROWS,), jnp.int32),
            pltpu.SMEM((ITEM_ROWS,), jnp.int32),
            pltpu.SMEM((1,), jnp.int32),
            pltpu.VMEM((ITEM_ROWS // SUBLANES, SUBLANES, D_MODEL), F32),
            pltpu.VMEM((ITEM_ROWS // SUBLANES, SUBLANES, D_MODEL), F32),
            pltpu.VMEM((ITEM_ROWS, D_MODEL), BF16),
            pltpu.VMEM((EXPERT_SLICES, ITEM_ROWS, EXPERT_TN), BF16),
            pltpu.VMEM((D_MODEL, EXPERT_TN), BF16),
            pltpu.VMEM((D_MODEL, EXPERT_TN), BF16),
            pltpu.VMEM((D_EXPERT, D_MODEL), BF16),
            pltpu.SemaphoreType.DMA((2,)),
            pltpu.SemaphoreType.DMA((ITEM_BLOCKS,)),
            pltpu.SemaphoreType.DMA((ITEM_BLOCKS,)),
        ],
    )
    return pl.pallas_call(
        _experts_kernel,
        out_shape=jax.ShapeDtypeStruct((PAIR_ROWS // SUBLANES, SUBLANES, D_MODEL), F32),
        grid_spec=grid_spec,
        compiler_params=pltpu.CompilerParams(
            dimension_semantics=("arbitrary", "arbitrary"), vmem_limit_bytes=VMEM_LIMIT),
        name="experts",
    )(item_e, item_blk0, item_nb, src_row, dst_row, h_all.reshape(N_ROWS // SUBLANES, SUBLANES, D_MODEL),
      norm_ffn, w_gate, w_up, w_down).reshape(PAIR_ROWS, D_MODEL)


def _dispatch_tables(eid):
    flat_e = eid.reshape(-1)
    onehot = (flat_e[:, None] == jnp.arange(N_EXPERTS, dtype=jnp.int32)[None, :]).astype(jnp.int32)
    rank = jnp.take_along_axis(jnp.cumsum(onehot, axis=0) - onehot, flat_e[:, None], axis=1)[:, 0]
    counts = jnp.sum(onehot, axis=0)
    nblk = (counts + ROW_BLOCK - 1) // ROW_BLOCK
    blk0 = jnp.cumsum(nblk) - nblk
    slot = blk0[flat_e] * ROW_BLOCK + rank
    slot_pair = jnp.full((SLOT_TABLE,), N_PAIRS, jnp.int32).at[slot].set(jnp.arange(N_PAIRS, dtype=jnp.int32))
    slot_id = jnp.arange(SLOT_TABLE, dtype=jnp.int32)
    src_row = jnp.minimum(slot_pair // TOP_K, N_ROWS - 1)
    dst_row = jnp.where(slot_pair < N_PAIRS, (slot_pair % TOP_K) * PAIR_REGION + slot_pair // TOP_K,
                        DUMP_ROW0 + slot_id % ROW_BLOCK)
    nitem = (nblk + ITEM_BLOCKS - 1) // ITEM_BLOCKS
    item_end = jnp.cumsum(nitem)
    total = item_end[-1]
    idx = jnp.arange(MAX_ITEMS, dtype=jnp.int32)
    live = idx < total
    e_of = jnp.minimum(jnp.searchsorted(item_end, jnp.minimum(idx, total - 1), side='right'),
                       N_EXPERTS - 1).astype(jnp.int32)
    local = jnp.minimum(idx, total - 1) - (item_end - nitem)[e_of]
    item_blk0 = (blk0[e_of] + local * ITEM_BLOCKS).astype(jnp.int32)
    item_nb = jnp.where(live, jnp.clip(nblk[e_of] - local * ITEM_BLOCKS, 0, ITEM_BLOCKS), 0).astype(jnp.int32)
    return e_of, item_blk0, item_nb, src_row, dst_row


def _final_rows(h, y0, y1, ew, p, wg_ref, wp_ref, np_ref, fn_ref):
    h = h + (ew[:, 0:1] * y0 + ew[:, 1:2] * y1)
    gate = _sigmoid(_bdot(_rms(h, np_ref[...]).astype(BF16), wg_ref[...]))
    h = h + gate * _bdot(p.astype(BF16), wp_ref[...])
    return _rms(h, fn_ref[...])


def _final_kernel(h_ref, y0_ref, y1_ref, ew_ref, p_ref, hs_ref, y0s_ref, y1s_ref, ews_ref, ps_ref,
                  wg_hbm, wp_hbm, np_ref, fn_ref, op_ref, os_ref, wg_ref, wp_ref, sem):
    i = pl.program_id(0)

    @pl.when(i == 0)
    def _():
        c0 = pltpu.make_async_copy(wg_hbm, wg_ref, sem.at[0])
        c1 = pltpu.make_async_copy(wp_hbm, wp_ref, sem.at[1])
        c0.start()
        c1.start()
        c0.wait()
        c1.wait()

    @pl.when(i < FIN_PROMPT_STEPS)
    def _():
        op_ref[...] = _final_rows(h_ref[...], y0_ref[...], y1_ref[...], ew_ref[...], p_ref[...],
                                  wg_ref, wp_ref, np_ref, fn_ref)

    @pl.when(i == FIN_PROMPT_STEPS)
    def _():
        os_ref[...] = _final_rows(hs_ref[...], y0s_ref[...], y1s_ref[...], ews_ref[...], ps_ref[...],
                                  wg_ref, wp_ref, np_ref, fn_ref)


def _final(h_all, ypairs, ew, p_prompt, p_sample, w_gate, w_proj, norm_ple, final_norm):
    last = FIN_PROMPT_STEPS - 1
    prow = lambda i: (jnp.minimum(i, last), 0)
    const2 = lambda i: (0, 0)
    hbm = pl.BlockSpec(memory_space=pltpu.MemorySpace.HBM)
    sblk = N_PROMPT // N_SAMPLE
    pact = pl.BlockSpec((FIN_TM, D_MODEL), prow)
    sact = lambda off: pl.BlockSpec((N_SAMPLE, D_MODEL), lambda i: (off + sblk, 0))
    return pl.pallas_call(
        _final_kernel,
        out_shape=(jax.ShapeDtypeStruct((N_PROMPT, D_MODEL), F32), jax.ShapeDtypeStruct((N_SAMPLE, D_MODEL), F32)),
        grid=(FIN_PROMPT_STEPS + 1,),
        in_specs=[pact, pact,
                  pl.BlockSpec((FIN_TM, D_MODEL), lambda i: (jnp.minimum(i, last) + PAIR_REGION // FIN_TM, 0)),
                  pl.BlockSpec((FIN_TM, LANES), prow),
                  pl.BlockSpec((FIN_TM, PLE_DIM), prow),
                  sact(0), sact(0), sact(PAIR_REGION // N_SAMPLE),
                  pl.BlockSpec((N_SAMPLE, LANES), lambda i: (sblk, 0)),
                  pl.BlockSpec((N_SAMPLE, PLE_DIM), const2),
                  hbm, hbm,
                  pl.BlockSpec((1, D_MODEL), const2),
                  pl.BlockSpec((1, D_MODEL), const2)],
        out_specs=(pact, pl.BlockSpec((N_SAMPLE, D_MODEL), const2)),
        scratch_shapes=[pltpu.VMEM((D_MODEL, D_MODEL), BF16), pltpu.VMEM((PLE_DIM, D_MODEL), BF16),
                        pltpu.SemaphoreType.DMA((2,))],
        compiler_params=pltpu.CompilerParams(
            dimension_semantics=("arbitrary",), vmem_limit_bytes=VMEM_LIMIT),
        name="final",
    )(h_all, ypairs, ypairs, ew, p_prompt, h_all, ypairs, ypairs, ew, p_sample, w_gate, w_proj, norm_ple, final_norm)


def kernel(x_prompt, x_sample, state_conv, p_prompt, p_sample, norm_mix, w_in, ln_v_g, ln_v_b, w_spatial,
           b_spatial, w_proj_a, conv_w, conv_b, ln_c_g, ln_c_b, w_proj_b, w_out, norm_ffn, w_router_group,
           b_router_group, w_router_expert, b_router_expert, w_exp_gate, w_exp_up, w_exp_down, norm_ple,
           w_ple_gate, w_ple_proj, final_norm):
    assert w_in.shape[0] == 1, "single layer"
    vec = lambda v: v.reshape(1, -1)
    xn, x_all = _xnorm(x_prompt.reshape(N_PROMPT, D_MODEL), x_sample.reshape(N_SAMPLE, D_MODEL), vec(norm_mix[0]))

    w_s, b_s = w_spatial[0], b_spatial[0]
    wmix = jnp.tril(w_s).astype(BF16)
    bmix = jnp.repeat(b_s.T, A_GROUP_DIM, axis=1)
    w00 = jnp.repeat(w_s[:, 0, 0], A_GROUP_DIM).reshape(1, D_MODEL)
    b00 = jnp.repeat(b_s[:, 0], A_GROUP_DIM).reshape(1, D_MODEL)

    act = _in_proj(xn, w_in[0])

    cw = conv_w[0]
    new_conv_sample, conv_s = _sconv(state_conv, act, cw, vec(conv_b[0]))
    cw_pad = jnp.concatenate([cw, jnp.zeros((1, D_MODEL), F32)], axis=0)
    wi = jnp.arange(WIN_ROWS, dtype=jnp.int32)[None, :, None]
    wj = jnp.arange(WIN_PAD, dtype=jnp.int32)[None, None, :]
    wr = jnp.arange(1, SUBLANES, dtype=jnp.int32)[:, None, None]
    sel = (wj == wi + wr).astype(BF16)
    wa = _cast_bf16(w_proj_a[0], 512)
    wb = _cast_bf16(w_proj_b[0], 512)
    m_all, vn_s = _seqmix(act, conv_s, wmix, bmix, w00, b00, vec(ln_v_g[0]), vec(ln_v_b[0]), cw_pad,
                          vec(conv_b[0]), vec(ln_c_g[0]), vec(ln_c_b[0]), sel, wa, wb)

    w_route = jnp.zeros((D_MODEL, LANES), F32)
    w_route = w_route.at[:, :MOE_GROUPS].set(w_router_group[0])
    w_route = w_route.at[:, MOE_GROUPS:MOE_GROUPS + N_EXPERTS].set(w_router_expert[0])
    b_route = jnp.zeros((1, LANES), F32)
    b_route = b_route.at[0, :MOE_GROUPS].set(b_router_group[0])
    b_route = b_route.at[0, MOE_GROUPS:MOE_GROUPS + N_EXPERTS].set(b_router_expert[0])

    wo = _cast_bf16(w_out[0], 512)
    h_all, eid, ew = _mixout(m_all, x_all, wo, vec(norm_ffn[0]), w_route.astype(BF16), b_route)

    tables = _dispatch_tables(eid[:, :TOP_K])
    ypairs = _experts(*tables, h_all, vec(norm_ffn[0]), w_exp_gate[0], w_exp_up[0], w_exp_down[0])

    wpg = _cast_bf16(w_ple_gate[0], 512)
    wpp = _cast_bf16(w_ple_proj[0], PLE_DIM)
    y_prompt, y_sample = _final(h_all, ypairs, ew, p_prompt[0].reshape(N_PROMPT, PLE_DIM),
                                p_sample[0].reshape(N_SAMPLE, PLE_DIM), wpg, wpp, vec(norm_ple[0]), vec(final_norm))
    y_prompt = y_prompt.reshape(N_PROMPT_SEQ, SEQ, D_MODEL)
    y_sample = y_sample.reshape(N_SAMPLE, 1, D_MODEL)
    glu_cols = slice(ACT_GLU * D_MODEL, (ACT_GLU + 1) * D_MODEL)
    new_conv_prompt = jnp.stack([act[(b + 1) * SEQ - CONV_HIST:(b + 1) * SEQ, glu_cols]
                                 for b in range(N_PROMPT_SEQ)])[None]
    return (y_prompt, y_sample, new_conv_prompt, new_conv_sample, vn_s.reshape(1, N_SAMPLE, 1, D_MODEL))
```

```python
import jax
import jax.numpy as jnp
from jax import lax
from jax.experimental import pallas as pl
from jax.experimental.pallas import tpu as pltpu

F32 = jnp.float32
BF16 = jnp.bfloat16

D_MODEL = 2048
N_PROMPT_SEQ = 4
SEQ = 2048
N_PROMPT = N_PROMPT_SEQ * SEQ
N_SAMPLE = 128
N_ROWS = N_PROMPT + N_SAMPLE
CHUNK = 128
N_CHUNKS = N_ROWS // CHUNK
CHUNKS_PER_SEQ = SEQ // CHUNK
A_GROUPS = 8
A_GROUP_DIM = D_MODEL // A_GROUPS
CONV_WIDTH = 31
CONV_HIST = CONV_WIDTH - 1
HIST_BLOCK = 32
MOE_GROUPS = 4
EXPERTS_PER_GROUP = 8
N_EXPERTS = MOE_GROUPS * EXPERTS_PER_GROUP
TOP_K = 2
D_EXPERT = D_MODEL // 2
PLE_DIM = 256
EPS = 1e-6
LANES = 128

XN_TM = 512
IN_TM = 1664
IN_TN = 512
IN_COL_TILES = D_MODEL // IN_TN
ACT_BLOCKS = 5
ACT_GU, ACT_GV, ACT_GLU, ACT_SGA, ACT_SGB = range(ACT_BLOCKS)
MIX_TM = 832
FIN_TM = 256
FIN_PROMPT_STEPS = N_PROMPT // FIN_TM
ROW_BLOCK = 128
SUBLANES = 8
BLOCK_TILES = ROW_BLOCK // SUBLANES
N_PAIRS = N_ROWS * TOP_K
MAX_BLOCKS = N_PAIRS // ROW_BLOCK + N_EXPERTS
ITEM_BLOCKS = 8
ITEM_ROWS = ITEM_BLOCKS * ROW_BLOCK
MAX_ITEMS = (MAX_BLOCKS + (ITEM_BLOCKS - 1) * N_EXPERTS) // ITEM_BLOCKS
N_SLOTS = MAX_BLOCKS * ROW_BLOCK
SLOT_TABLE = N_SLOTS + ITEM_ROWS
EXPERT_SLICES = 4
EXPERT_TN = D_EXPERT // EXPERT_SLICES
PAIR_REGION = 8448
DUMP_ROW0 = PAIR_REGION + N_ROWS
PAIR_ROWS = DUMP_ROW0 + ROW_BLOCK

VMEM_LIMIT = 56 * 1024 * 1024


def _rms(x, g):
    return x * lax.rsqrt(jnp.mean(x * x, axis=-1, keepdims=True) + EPS) * g


def _layer_norm(x, g, b):
    mu = jnp.mean(x, axis=-1, keepdims=True)
    xc = x - mu
    return xc * lax.rsqrt(jnp.mean(xc * xc, axis=-1, keepdims=True) + EPS) * g + b


def _sigmoid(x):
    return 1.0 / (1.0 + jnp.exp(-x))


def _silu(x):
    return x * _sigmoid(x)


def _gelu(x):
    return jax.nn.gelu(x, approximate=True)


def _bdot(a, b):
    return jnp.dot(a, b, preferred_element_type=F32)


def _cast_kernel(w_ref, o_ref):
    o_ref[...] = w_ref[...].astype(BF16)


def _cast_bf16(w, rows):
    k, n = w.shape
    return pl.pallas_call(
        _cast_kernel,
        out_shape=jax.ShapeDtypeStruct((k, n), BF16),
        grid=(k // rows,),
        in_specs=[pl.BlockSpec((rows, n), lambda i: (i, 0))],
        out_specs=pl.BlockSpec((rows, n), lambda i: (i, 0)),
        name="cast_bf16",
    )(w)


def _xnorm_kernel(xp_ref, xs_ref, nm_ref, xn_ref, xall_ref):
    i = pl.program_id(0)

    @pl.when(i < N_PROMPT // XN_TM)
    def _():
        x = xp_ref[...]
        xall_ref[...] = x
        xn_ref[...] = _rms(x, nm_ref[...]).astype(BF16)

    @pl.when(i == N_PROMPT // XN_TM)
    def _():
        x = xs_ref[...]
        xall_ref[0:N_SAMPLE, :] = x
        xall_ref[N_SAMPLE:, :] = jnp.zeros((XN_TM - N_SAMPLE, D_MODEL), F32)
        xn_ref[0:N_SAMPLE, :] = _rms(x, nm_ref[...]).astype(BF16)
        xn_ref[N_SAMPLE:, :] = jnp.zeros((XN_TM - N_SAMPLE, D_MODEL), BF16)


def _xnorm(x_prompt, x_sample, norm_mix):
    last = N_PROMPT // XN_TM - 1
    row = lambda i: (i, 0)
    return pl.pallas_call(
        _xnorm_kernel,
        out_shape=(jax.ShapeDtypeStruct((N_ROWS, D_MODEL), BF16), jax.ShapeDtypeStruct((N_ROWS, D_MODEL), F32)),
        grid=(N_PROMPT // XN_TM + 1,),
        in_specs=[pl.BlockSpec((XN_TM, D_MODEL), lambda i: (jnp.minimum(i, last), 0)),
                  pl.BlockSpec((N_SAMPLE, D_MODEL), lambda i: (0, 0)),
                  pl.BlockSpec((1, D_MODEL), lambda i: (0, 0))],
        out_specs=(pl.BlockSpec((XN_TM, D_MODEL), row), pl.BlockSpec((XN_TM, D_MODEL), row)),
        name="xnorm",
    )(x_prompt, x_sample, norm_mix)


def _in_w_col(j):
    t = IN_COL_TILES
    jj = j - 2 * t
    glu_col = 2 * t + (jj % 2) * t + jj // 2
    return jnp.where((j >= 2 * t) & (j < 4 * t), glu_col, j)


def _in_out_col(j):
    t = IN_COL_TILES
    return jnp.where(j < 2 * t, j, jnp.where(j < 4 * t, 2 * t + (j - 2 * t) // 2, j - t))


def _in_proj_kernel(xn_ref, w_ref, o_ref, val_ref):
    j = pl.program_id(1)
    t = IN_COL_TILES

    def z():
        return _bdot(xn_ref[...], w_ref[...].astype(BF16))

    @pl.when(j < 2 * t)
    def _():
        o_ref[...] = _gelu(z())

    @pl.when((j >= 2 * t) & (j < 4 * t) & (j % 2 == 0))
    def _():
        val_ref[...] = z()

    @pl.when((j >= 2 * t) & (j < 4 * t) & (j % 2 == 1))
    def _():
        o_ref[...] = val_ref[...] * _sigmoid(z())

    @pl.when(j >= 4 * t)
    def _():
        o_ref[...] = _sigmoid(z())


def _in_proj(xn, w_in):
    return pl.pallas_call(
        _in_proj_kernel,
        out_shape=jax.ShapeDtypeStruct((N_ROWS, ACT_BLOCKS * D_MODEL), F32),
        grid=(N_ROWS // IN_TM, 6 * IN_COL_TILES),
        in_specs=[
            pl.BlockSpec((IN_TM, D_MODEL), lambda i, j: (i, 0)),
            pl.BlockSpec((D_MODEL, IN_TN), lambda i, j: (0, _in_w_col(j))),
        ],
        out_specs=pl.BlockSpec((IN_TM, IN_TN), lambda i, j: (i, _in_out_col(j))),
        scratch_shapes=[pltpu.VMEM((IN_TM, IN_TN), F32)],
        compiler_params=pltpu.CompilerParams(
            dimension_semantics=("arbitrary", "arbitrary"), vmem_limit_bytes=VMEM_LIMIT),
        name="in_proj",
    )(xn, w_in)


SCONV_B = 16


def _sconv_kernel(st_ref, glu_ref, w_ref, wl_ref, cb_ref, new_ref, conv_ref):
    glu = glu_ref[...]
    new_ref[0:CONV_HIST - 1] = st_ref[1:CONV_HIST]
    new_ref[CONV_HIST - 1] = glu
    acc = st_ref[0] * w_ref[0:1, :]
    for k in range(1, CONV_HIST):
        acc = acc + st_ref[k] * w_ref[k:k + 1, :]
    conv_ref[...] = acc + glu * wl_ref[...] + cb_ref[...]


def _sconv(state, act, conv_w, conv_b):
    return pl.pallas_call(
        _sconv_kernel,
        out_shape=(
            jax.ShapeDtypeStruct((CONV_HIST, N_SAMPLE, D_MODEL), F32),
            jax.ShapeDtypeStruct((N_SAMPLE, D_MODEL), F32),
        ),
        grid=(N_SAMPLE // SCONV_B,),
        in_specs=[
            pl.BlockSpec((CONV_HIST, SCONV_B, D_MODEL), lambda i: (0, i, 0)),
            pl.BlockSpec((SCONV_B, D_MODEL), lambda i: (N_PROMPT // SCONV_B + i, ACT_GLU)),
            pl.BlockSpec((CONV_HIST, D_MODEL), lambda i: (0, 0)),
            pl.BlockSpec((1, D_MODEL), lambda i: (0, 0)),
            pl.BlockSpec((1, D_MODEL), lambda i: (0, 0)),
        ],
        out_specs=(
            pl.BlockSpec((CONV_HIST, SCONV_B, D_MODEL), lambda i: (0, i, 0)),
            pl.BlockSpec((SCONV_B, D_MODEL), lambda i: (i, 0)),
        ),
        name="sconv",
    )(state, act, conv_w[:CONV_HIST], conv_w[CONV_HIST:], conv_b)


CONV_STRIP = 256
WIN_ROWS = HIST_BLOCK + CHUNK


def _seqmix_kernel(gv_ref, gu_ref, cur_ref, prev_ref, sga_ref, sgb_ref, convs_ref, wmix_ref, bmix_ref, w00_ref,
                   b00_ref, lvg_ref, lvb_ref, cw_ref, cb_ref, lcg_ref, lcb_ref, wa_hbm, wb_hbm,
                   m_ref, vns_ref, win_ref, conv_ref, a_ref, b_ref, wa_ref, wb_ref, sem):
    c = pl.program_id(0)
    is_sample = c == N_CHUNKS - 1

    @pl.when(c == 0)
    def _():
        ca = pltpu.make_async_copy(wa_hbm, wa_ref, sem.at[0])
        cb = pltpu.make_async_copy(wb_hbm, wb_ref, sem.at[1])
        ca.start()
        cb.start()
        ca.wait()
        cb.wait()

    fresh = (c % CHUNKS_PER_SEQ) == 0
    win_ref[0:HIST_BLOCK, :] = jnp.where(fresh, 0.0, prev_ref[...]).astype(BF16).astype(F32)
    win_ref[HIST_BLOCK:WIN_ROWS, :] = cur_ref[...].astype(BF16).astype(F32)
    off = HIST_BLOCK - CONV_HIST

    vn = _layer_norm(gv_ref[...], lvg_ref[...], lvb_ref[...])
    vns_ref[...] = vn

    vnb = vn.astype(BF16)
    for g in range(A_GROUPS):
        sl = slice(g * A_GROUP_DIM, (g + 1) * A_GROUP_DIM)
        s_chunk = _bdot(wmix_ref[g], vnb[:, sl]) + bmix_ref[:, sl]
        s_first = vn[:, sl] * w00_ref[:, sl] + b00_ref[:, sl]
        a_ref[:, sl] = (gu_ref[:, sl] * jnp.where(is_sample, s_first, s_chunk)).astype(BF16)
    pa = _bdot(a_ref[...], wa_ref[...])

    def phase_sum(r, cs):
        rows = CHUNK + (SUBLANES if r else 0)
        total = None
        for k in range(CONV_WIDTH):
            q, rk = divmod(off + k, SUBLANES)
            if rk == r:
                term = win_ref[SUBLANES * q:SUBLANES * q + rows, cs] * cw_ref[k:k + 1, cs].astype(BF16).astype(F32)
                total = term if total is None else total + term
        return total[r:r + CHUNK, :]

    for s0 in range(0, D_MODEL, CONV_STRIP):
        cs = slice(s0, s0 + CONV_STRIP)
        acc = phase_sum(0, cs)
        for r in range(1, SUBLANES):
            acc = acc + phase_sum(r, cs)
        conv_ref[:, cs] = acc + cb_ref[:, cs]
    conv = jnp.where(is_sample, convs_ref[...], conv_ref[...])
    b_ref[...] = _silu(_layer_norm(conv, lcg_ref[...], lcb_ref[...])).astype(BF16)

    pb = _bdot(b_ref[...], wb_ref[...])
    m_ref[...] = (sga_ref[...] * pa + sgb_ref[...] * pb).astype(BF16)


def _seqmix(act, conv_s, wmix, bmix, w00, b00, ln_v_g, ln_v_b, conv_w, conv_b, ln_c_g, ln_c_b, wa, wb):
    row = lambda c: (c, 0)
    const2 = lambda c: (0, 0)
    vec = pl.BlockSpec((1, D_MODEL), const2)
    hbm = pl.BlockSpec(memory_space=pltpu.MemorySpace.HBM)
    hist_per_chunk = CHUNK // HIST_BLOCK
    act_block = lambda blk: pl.BlockSpec((CHUNK, D_MODEL), lambda c: (c, blk))
    return pl.pallas_call(
        _seqmix_kernel,
        out_shape=(
            jax.ShapeDtypeStruct((N_ROWS, D_MODEL), BF16),
            jax.ShapeDtypeStruct((N_SAMPLE, D_MODEL), F32),
        ),
        grid=(N_CHUNKS,),
        in_specs=[
            act_block(ACT_GV), act_block(ACT_GU), act_block(ACT_GLU),
            pl.BlockSpec((HIST_BLOCK, D_MODEL), lambda c: (jnp.maximum(c * hist_per_chunk - 1, 0), ACT_GLU)),
            act_block(ACT_SGA), act_block(ACT_SGB),
            pl.BlockSpec((N_SAMPLE, D_MODEL), const2),
            pl.BlockSpec((A_GROUPS, CHUNK, CHUNK), lambda c: (0, 0, 0)),
            pl.BlockSpec((CHUNK, D_MODEL), const2),
            vec, vec, vec, vec,
            pl.BlockSpec((CONV_WIDTH + 1, D_MODEL), const2),
            vec, vec, vec,
            hbm, hbm,
        ],
        out_specs=(
            pl.BlockSpec((CHUNK, D_MODEL), row),
            pl.BlockSpec((N_SAMPLE, D_MODEL), const2),
        ),
        scratch_shapes=[pltpu.VMEM((WIN_ROWS, D_MODEL), F32),
                        pltpu.VMEM((CHUNK, D_MODEL), F32),
                        pltpu.VMEM((CHUNK, D_MODEL), BF16), pltpu.VMEM((CHUNK, D_MODEL), BF16),
                        pltpu.VMEM((D_MODEL, D_MODEL), BF16), pltpu.VMEM((D_MODEL, D_MODEL), BF16),
                        pltpu.SemaphoreType.DMA((2,))],
        compiler_params=pltpu.CompilerParams(
            dimension_semantics=("arbitrary",), vmem_limit_bytes=VMEM_LIMIT),
        name="seqmix",
    )(act, act, act, act, act, act, conv_s, wmix, bmix, w00, b00, ln_v_g, ln_v_b, conv_w, conv_b, ln_c_g, ln_c_b,
      wa, wb)


def _route(logits):
    col = lax.broadcasted_iota(jnp.int32, logits.shape, 1).astype(F32)
    neg = jnp.float32(-jnp.inf)
    big = jnp.float32(1e9)
    lg = jnp.where(col < MOE_GROUPS, logits, neg)
    gmax = jnp.max(lg, axis=-1, keepdims=True)
    gi = jnp.min(jnp.where(lg == gmax, col, big), axis=-1, keepdims=True)
    gw = 1.0 / jnp.sum(jnp.exp(lg - gmax), axis=-1, keepdims=True)
    lo = MOE_GROUPS + gi * EXPERTS_PER_GROUP
    le = jnp.where((col >= lo) & (col < lo + EXPERTS_PER_GROUP), logits, neg)
    m1 = jnp.max(le, axis=-1, keepdims=True)
    i1 = jnp.min(jnp.where(le == m1, col, big), axis=-1, keepdims=True)
    le2 = jnp.where(col == i1, neg, le)
    m2 = jnp.max(le2, axis=-1, keepdims=True)
    i2 = jnp.min(jnp.where(le2 == m2, col, big), axis=-1, keepdims=True)
    e = jnp.exp(m2 - m1)
    w1 = gw / (1.0 + e)
    w2 = gw * e / (1.0 + e)
    eid = jnp.where(col == 0, i1 - MOE_GROUPS, jnp.where(col == 1, i2 - MOE_GROUPS, 0.0)).astype(jnp.int32)
    ew = jnp.where(col == 0, w1, jnp.where(col == 1, w2, 0.0))
    return eid, ew


def _mixout_kernel(m_ref, x_ref, wo_hbm, nf_ref, wr_ref, br_ref, h_ref, eid_ref, ew_ref, wo_ref, sem):
    @pl.when(pl.program_id(0) == 0)
    def _():
        cp = pltpu.make_async_copy(wo_hbm, wo_ref, sem.at[0])
        cp.start()
        cp.wait()

    h = x_ref[...] + _bdot(m_ref[...], wo_ref[...])
    h_ref[...] = h
    hn = _rms(h, nf_ref[...])
    logits = _bdot(hn.astype(BF16), wr_ref[...]) + br_ref[...]
    eid, ew = _route(logits)
    eid_ref[...] = eid
    ew_ref[...] = ew


def _mixout(m_all, x_all, wo, norm_ffn, w_route, b_route):
    row = lambda i: (i, 0)
    const2 = lambda i: (0, 0)
    rows = pl.BlockSpec((MIX_TM, D_MODEL), row)
    hbm = pl.BlockSpec(memory_space=pltpu.MemorySpace.HBM)
    return pl.pallas_call(
        _mixout_kernel,
        out_shape=(
            jax.ShapeDtypeStruct((N_ROWS, D_MODEL), F32),
            jax.ShapeDtypeStruct((N_ROWS, LANES), jnp.int32),
            jax.ShapeDtypeStruct((N_ROWS, LANES), F32),
        ),
        grid=(N_ROWS // MIX_TM,),
        in_specs=[rows, rows, hbm,
                  pl.BlockSpec((1, D_MODEL), const2),
                  pl.BlockSpec((D_MODEL, LANES), const2),
                  pl.BlockSpec((1, LANES), const2)],
        out_specs=(rows, pl.BlockSpec((MIX_TM, LANES), row), pl.BlockSpec((MIX_TM, LANES), row)),
        scratch_shapes=[pltpu.VMEM((D_MODEL, D_MODEL), BF16), pltpu.SemaphoreType.DMA((1,))],
        compiler_params=pltpu.CompilerParams(
            dimension_semantics=("arbitrary",), vmem_limit_bytes=VMEM_LIMIT),
        name="mixout",
    )(m_all, x_all, wo, norm_ffn, w_route, b_route)


def _experts_kernel(item_e, item_blk0, item_nb, src_hbm, dst_hbm, h_hbm, nf_ref, wg_ref, wu_ref, wd_ref,
                    y_hbm, src_s, dst_s, pend_s, xg_ref, yst_ref, xs_ref, hm_ref, wgb_ref, wub_ref, wdb_ref,
                    idx_sem, g_sem, s_sem):
    w = pl.program_id(0)
    s = pl.program_id(1)
    nb = item_nb[w]
    nxt = jnp.minimum(w + 1, MAX_ITEMS - 1)
    nb_next = jnp.where(w + 1 < MAX_ITEMS, item_nb[nxt], 0)

    def table_fetch(tbl_hbm, tbl_s, sem_idx, item):
        slot0 = pl.multiple_of(item_blk0[item] * ROW_BLOCK, ROW_BLOCK)
        return pltpu.make_async_copy(tbl_hbm.at[pl.ds(slot0, ITEM_ROWS)], tbl_s, idx_sem.at[sem_idx])

    def row_copy_in(t, u, blk):
        src = src_s[t * SUBLANES + u]
        return pltpu.make_async_copy(h_hbm.at[src >> 3, pl.ds(src & (SUBLANES - 1), 1)],
                                     xg_ref.at[t, pl.ds(u, 1)], g_sem.at[blk])

    def row_copy_out(t, u, blk):
        dst = dst_s[t * SUBLANES + u]
        return pltpu.make_async_copy(yst_ref.at[t, pl.ds(u, 1)],
                                     y_hbm.at[dst >> 3, pl.ds(dst & (SUBLANES - 1), 1)], s_sem.at[blk])

    def block_rows(blk):
        return pl.ds(pl.multiple_of(blk * ROW_BLOCK, ROW_BLOCK), ROW_BLOCK)

    def block_tiles(blk):
        return pl.ds(pl.multiple_of(blk * BLOCK_TILES, BLOCK_TILES), BLOCK_TILES)

    def block_copy_wait(sem, blk):
        pltpu.make_async_copy(xg_ref.at[block_tiles(blk)], xg_ref.at[block_tiles(blk)], sem.at[blk]).wait()

    def for_each_row(n_blocks, row_copy):
        def tile(t, carry):
            for u in range(SUBLANES):
                row_copy(t, u, lax.shift_right_logical(t, BLOCK_TILES.bit_length() - 1)).start()
            return carry
        lax.fori_loop(0, n_blocks * BLOCK_TILES, tile, 0)

    def issue_gather(n_blocks):
        for_each_row(n_blocks, row_copy_in)

    def drain_scatter():
        def drain(blk, carry):
            block_copy_wait(s_sem, blk)
            return carry
        lax.fori_loop(0, pend_s[0], drain, 0)
        pend_s[0] = 0

    @pl.when((w == 0) & (s == 0))
    def _():
        pad_tiles = pl.ds(0, BLOCK_TILES)
        xg_ref[pad_tiles] = jnp.zeros((BLOCK_TILES, SUBLANES, D_MODEL), F32)
        for row0 in (N_ROWS, DUMP_ROW0):
            cz = pltpu.make_async_copy(xg_ref.at[pad_tiles], y_hbm.at[pl.ds(row0 // SUBLANES, BLOCK_TILES)],
                                       idx_sem.at[0])
            cz.start()
            cz.wait()
        pend_s[0] = 0
        ci = table_fetch(src_hbm, src_s, 0, w)
        ci.start()
        ci.wait()
        issue_gather(nb)

    @pl.when((s == 0) & (nb > 0))
    def _():
        table_fetch(dst_hbm, dst_s, 1, w).start()

        def land(blk, carry):
            block_copy_wait(g_sem, blk)
            x = xg_ref[block_tiles(blk)].reshape(ROW_BLOCK, D_MODEL)
            xs_ref[block_rows(blk), :] = _rms(x, nf_ref[...]).astype(BF16)
            return carry
        lax.fori_loop(0, nb, land, 0)

    @pl.when((s == 1) & (nb_next > 0))
    def _():
        table_fetch(src_hbm, src_s, 0, nxt).start()

    @pl.when((s == 2) & (nb_next > 0))
    def _():
        table_fetch(src_hbm, src_s, 0, nxt).wait()
        issue_gather(nb_next)

    def over_row_blocks(fn):
        def four_blocks(j, carry):
            fn(pl.multiple_of(j * (4 * ROW_BLOCK), 4 * ROW_BLOCK), 4 * ROW_BLOCK)
            return carry
        lax.fori_loop(0, nb >> 2, four_blocks, 0)

        @pl.when((nb & 2) == 2)
        def _():
            fn(pl.multiple_of((nb >> 2) * (4 * ROW_BLOCK), 2 * ROW_BLOCK), 2 * ROW_BLOCK)

        @pl.when((nb & 1) == 1)
        def _():
            fn(pl.multiple_of((nb - 1) * ROW_BLOCK, ROW_BLOCK), ROW_BLOCK)

    @pl.when(nb > 0)
    def _():
        wgb_ref[...] = wg_ref[...].astype(BF16)
        wub_ref[...] = wu_ref[...].astype(BF16)
        wdb_ref[pl.ds(pl.multiple_of(s * EXPERT_TN, EXPERT_TN), EXPERT_TN), :] = wd_ref[...].astype(BF16)

        def gate_up(row0, nrows):
            rows = pl.ds(row0, nrows)
            x = xs_ref[rows, :]
            hm_ref[s, rows, :] = (_silu(_bdot(x, wgb_ref[...])) * _bdot(x, wub_ref[...])).astype(BF16)
        over_row_blocks(gate_up)

    @pl.when((s == EXPERT_SLICES - 1) & (nb > 0))
    def _():
        drain_scatter()

        def down(row0, nrows):
            rows = pl.ds(row0, nrows)
            hm = jnp.concatenate([hm_ref[k, rows, :] for k in range(EXPERT_SLICES)], axis=1)
            tiles = pl.ds(pl.multiple_of(row0 // SUBLANES, BLOCK_TILES), nrows // SUBLANES)
            yst_ref[tiles] = _bdot(hm, wdb_ref[...]).reshape(nrows // SUBLANES, SUBLANES, D_MODEL)
        over_row_blocks(down)

        table_fetch(dst_hbm, dst_s, 1, w).wait()

        for_each_row(nb, row_copy_out)
        pend_s[0] = nb

    @pl.when((w == MAX_ITEMS - 1) & (s == EXPERT_SLICES - 1))
    def _():
        drain_scatter()


def _experts(item_e, item_blk0, item_nb, src_row, dst_row, h_all, norm_ffn, w_gate, w_up, w_down):
    hbm = pl.BlockSpec(memory_space=pltpu.MemorySpace.HBM)

    def w_slice(w, s, inb):
        return jnp.where(inb[w] > 0, s, EXPERT_SLICES - 1)

    grid_spec = pltpu.PrefetchScalarGridSpec(
        num_scalar_prefetch=3,
        grid=(MAX_ITEMS, EXPERT_SLICES),
        in_specs=[
            hbm, hbm, hbm,
            pl.BlockSpec((1, D_MODEL), lambda w, s, ie, ib, inb: (0, 0)),
            pl.BlockSpec((None, D_MODEL, EXPERT_TN), lambda w, s, ie, ib, inb: (ie[w], 0, w_slice(w, s, inb))),
            pl.BlockSpec((None, D_MODEL, EXPERT_TN), lambda w, s, ie, ib, inb: (ie[w], 0, w_slice(w, s, inb))),
            pl.BlockSpec((None, EXPERT_TN, D_MODEL), lambda w, s, ie, ib, inb: (ie[w], w_slice(w, s, inb), 0)),
        ],
        out_specs=hbm,
        scratch_shapes=[
            pltpu.SMEM((ITEM_ROWS,), jnp.int32),
            pltpu.SMEM((ITEM_ROWS,), jnp.int32),
            pltpu.SMEM((1,), jnp.int32),
            pltpu.VMEM((ITEM_ROWS // SUBLANES, SUBLANES, D_MODEL), F32),
            pltpu.VMEM((ITEM_ROWS // SUBLANES, SUBLANES, D_MODEL), F32),
            pltpu.VMEM((ITEM_ROWS, D_MODEL), BF16),
            pltpu.VMEM((EXPERT_SLICES, ITEM_ROWS, EXPERT_TN), BF16),
            pltpu.VMEM((D_MODEL, EXPERT_TN), BF16),
            pltpu.VMEM((D_MODEL, EXPERT_TN), BF16),
            pltpu.VMEM((D_EXPERT, D_MODEL), BF16),
            pltpu.SemaphoreType.DMA((2,)),
            pltpu.SemaphoreType.DMA((ITEM_BLOCKS,)),
            pltpu.SemaphoreType.DMA((ITEM_BLOCKS,)),
        ],
    )
    return pl.pallas_call(
        _experts_kernel,
        out_shape=jax.ShapeDtypeStruct((PAIR_ROWS // SUBLANES, SUBLANES, D_MODEL), F32),
        grid_spec=grid_spec,
        compiler_params=pltpu.CompilerParams(
            dimension_semantics=("arbitrary", "arbitrary"), vmem_limit_bytes=VMEM_LIMIT),
        name="experts",
    )(item_e, item_blk0, item_nb, src_row, dst_row, h_all.reshape(N_ROWS // SUBLANES, SUBLANES, D_MODEL),
      norm_ffn, w_gate, w_up, w_down).reshape(PAIR_ROWS, D_MODEL)


def _dispatch_tables(eid):
    flat_e = eid.reshape(-1)
    onehot = (flat_e[:, None] == jnp.arange(N_EXPERTS, dtype=jnp.int32)[None, :]).astype(jnp.int32)
    rank = jnp.take_along_axis(jnp.cumsum(onehot, axis=0) - onehot, flat_e[:, None], axis=1)[:, 0]
    counts = jnp.sum(onehot, axis=0)
    nblk = (counts + ROW_BLOCK - 1) // ROW_BLOCK
    blk0 = jnp.cumsum(nblk) - nblk
    slot = blk0[flat_e] * ROW_BLOCK + rank
    slot_pair = jnp.full((SLOT_TABLE,), N_PAIRS, jnp.int32).at[slot].set(jnp.arange(N_PAIRS, dtype=jnp.int32))
    slot_id = jnp.arange(SLOT_TABLE, dtype=jnp.int32)
    src_row = jnp.minimum(slot_pair // TOP_K, N_ROWS - 1)
    dst_row = jnp.where(slot_pair < N_PAIRS, (slot_pair % TOP_K) * PAIR_REGION + slot_pair // TOP_K,
                        DUMP_ROW0 + slot_id % ROW_BLOCK)
    nitem = (nblk + ITEM_BLOCKS - 1) // ITEM_BLOCKS
    item_end = jnp.cumsum(nitem)
    total = item_end[-1]
    idx = jnp.arange(MAX_ITEMS, dtype=jnp.int32)
    live = idx < total
    e_of = jnp.minimum(jnp.searchsorted(item_end, jnp.minimum(idx, total - 1), side='right'),
                       N_EXPERTS - 1).astype(jnp.int32)
    local = jnp.minimum(idx, total - 1) - (item_end - nitem)[e_of]
    item_blk0 = (blk0[e_of] + local * ITEM_BLOCKS).astype(jnp.int32)
    item_nb = jnp.where(live, jnp.clip(nblk[e_of] - local * ITEM_BLOCKS, 0, ITEM_BLOCKS), 0).astype(jnp.int32)
    return e_of, item_blk0, item_nb, src_row, dst_row


def _final_rows(h, y0, y1, ew, p, wg_ref, wp_ref, np_ref, fn_ref):
    h = h + (ew[:, 0:1] * y0 + ew[:, 1:2] * y1)
    gate = _sigmoid(_bdot(_rms(h, np_ref[...]).astype(BF16), wg_ref[...]))
    h = h + gate * _bdot(p.astype(BF16), wp_ref[...])
    return _rms(h, fn_ref[...])


def _final_kernel(h_ref, y0_ref, y1_ref, ew_ref, p_ref, hs_ref, y0s_ref, y1s_ref, ews_ref, ps_ref,
                  wg_hbm, wp_hbm, np_ref, fn_ref, op_ref, os_ref, wg_ref, wp_ref, sem):
    i = pl.program_id(0)

    @pl.when(i == 0)
    def _():
        c0 = pltpu.make_async_copy(wg_hbm, wg_ref, sem.at[0])
        c1 = pltpu.make_async_copy(wp_hbm, wp_ref, sem.at[1])
        c0.start()
        c1.start()
        c0.wait()
        c1.wait()

    @pl.when(i < FIN_PROMPT_STEPS)
    def _():
        op_ref[...] = _final_rows(h_ref[...], y0_ref[...], y1_ref[...], ew_ref[...], p_ref[...],
                                  wg_ref, wp_ref, np_ref, fn_ref)

    @pl.when(i == FIN_PROMPT_STEPS)
    def _():
        os_ref[...] = _final_rows(hs_ref[...], y0s_ref[...], y1s_ref[...], ews_ref[...], ps_ref[...],
                                  wg_ref, wp_ref, np_ref, fn_ref)


def _final(h_all, ypairs, ew, p_prompt, p_sample, w_gate, w_proj, norm_ple, final_norm):
    last = FIN_PROMPT_STEPS - 1
    prow = lambda i: (jnp.minimum(i, last), 0)
    const2 = lambda i: (0, 0)
    hbm = pl.BlockSpec(memory_space=pltpu.MemorySpace.HBM)
    sblk = N_PROMPT // N_SAMPLE
    pact = pl.BlockSpec((FIN_TM, D_MODEL), prow)
    sact = lambda off: pl.BlockSpec((N_SAMPLE, D_MODEL), lambda i: (off + sblk, 0))
    return pl.pallas_call(
        _final_kernel,
        out_shape=(jax.ShapeDtypeStruct((N_PROMPT, D_MODEL), F32), jax.ShapeDtypeStruct((N_SAMPLE, D_MODEL), F32)),
        grid=(FIN_PROMPT_STEPS + 1,),
        in_specs=[pact, pact,
                  pl.BlockSpec((FIN_TM, D_MODEL), lambda i: (jnp.minimum(i, last) + PAIR_REGION // FIN_TM, 0)),
                  pl.BlockSpec((FIN_TM, LANES), prow),
                  pl.BlockSpec((FIN_TM, PLE_DIM), prow),
                  sact(0), sact(0), sact(PAIR_REGION // N_SAMPLE),
                  pl.BlockSpec((N_SAMPLE, LANES), lambda i: (sblk, 0)),
                  pl.BlockSpec((N_SAMPLE, PLE_DIM), const2),
                  hbm, hbm,
                  pl.BlockSpec((1, D_MODEL), const2),
                  pl.BlockSpec((1, D_MODEL), const2)],
        out_specs=(pact, pl.BlockSpec((N_SAMPLE, D_MODEL), const2)),
        scratch_shapes=[pltpu.VMEM((D_MODEL, D_MODEL), BF16), pltpu.VMEM((PLE_DIM, D_MODEL), BF16),
                        pltpu.SemaphoreType.DMA((2,))],
        compiler_params=pltpu.CompilerParams(
            dimension_semantics=("arbitrary",), vmem_limit_bytes=VMEM_LIMIT),
        name="final",
    )(h_all, ypairs, ypairs, ew, p_prompt, h_all, ypairs, ypairs, ew, p_sample, w_gate, w_proj, norm_ple, final_norm)


def kernel(x_prompt, x_sample, state_conv, p_prompt, p_sample, norm_mix, w_in, ln_v_g, ln_v_b, w_spatial,
           b_spatial, w_proj_a, conv_w, conv_b, ln_c_g, ln_c_b, w_proj_b, w_out, norm_ffn, w_router_group,
           b_router_group, w_router_expert, b_router_expert, w_exp_gate, w_exp_up, w_exp_down, norm_ple,
           w_ple_gate, w_ple_proj, final_norm):
    assert w_in.shape[0] == 1, "single layer"
    vec = lambda v: v.reshape(1, -1)
    xn, x_all = _xnorm(x_prompt.reshape(N_PROMPT, D_MODEL), x_sample.reshape(N_SAMPLE, D_MODEL), vec(norm_mix[0]))

    w_s, b_s = w_spatial[0], b_spatial[0]
    wmix = jnp.tril(w_s).astype(BF16)
    bmix = jnp.repeat(b_s.T, A_GROUP_DIM, axis=1)
    w00 = jnp.repeat(w_s[:, 0, 0], A_GROUP_DIM).reshape(1, D_MODEL)
    b00 = jnp.repeat(b_s[:, 0], A_GROUP_DIM).reshape(1, D_MODEL)

    act = _in_proj(xn, w_in[0])

    cw = conv_w[0]
    new_state, conv_s = _sconv(jnp.transpose(state_conv[0], (1, 0, 2)), act, cw, vec(conv_b[0]))
    new_conv_sample = jnp.transpose(new_state, (1, 0, 2))[None]
    cw_pad = jnp.concatenate([cw, jnp.zeros((1, D_MODEL), F32)], axis=0)
    wa = _cast_bf16(w_proj_a[0], 512)
    wb = _cast_bf16(w_proj_b[0], 512)
    m_all, vn_s = _seqmix(act, conv_s, wmix, bmix, w00, b00, vec(ln_v_g[0]), vec(ln_v_b[0]), cw_pad,
                          vec(conv_b[0]), vec(ln_c_g[0]), vec(ln_c_b[0]), wa, wb)

    w_route = jnp.zeros((D_MODEL, LANES), F32)
    w_route = w_route.at[:, :MOE_GROUPS].set(w_router_group[0])
    w_route = w_route.at[:, MOE_GROUPS:MOE_GROUPS + N_EXPERTS].set(w_router_expert[0])
    b_route = jnp.zeros((1, LANES), F32)
    b_route = b_route.at[0, :MOE_GROUPS].set(b_router_group[0])
    b_route = b_route.at[0, MOE_GROUPS:MOE_GROUPS + N_EXPERTS].set(b_router_expert[0])

    wo = _cast_bf16(w_out[0], 512)
    h_all, eid, ew = _mixout(m_all, x_all, wo, vec(norm_ffn[0]), w_route.astype(BF16), b_route)

    tables = _dispatch_tables(eid[:, :TOP_K])
    ypairs = _experts(*tables, h_all, vec(norm_ffn[0]), w_exp_gate[0], w_exp_up[0], w_exp_down[0])

    wpg = _cast_bf16(w_ple_gate[0], 512)
    wpp = _cast_bf16(w_ple_proj[0], PLE_DIM)
    y_prompt, y_sample = _final(h_all, ypairs, ew, p_prompt[0].reshape(N_PROMPT, PLE_DIM),
                                p_sample[0].reshape(N_SAMPLE, PLE_DIM), wpg, wpp, vec(norm_ple[0]), vec(final_norm))
    y_prompt = y_prompt.reshape(N_PROMPT_SEQ, SEQ, D_MODEL)
    y_sample = y_sample.reshape(N_SAMPLE, 1, D_MODEL)
    glu_cols = slice(ACT_GLU * D_MODEL, (ACT_GLU + 1) * D_MODEL)
    new_conv_prompt = jnp.stack([act[(b + 1) * SEQ - CONV_HIST:(b + 1) * SEQ, glu_cols]
                                 for b in range(N_PROMPT_SEQ)])[None]
    return (y_prompt, y_sample, new_conv_prompt, new_conv_sample, vn_s.reshape(1, N_SAMPLE, 1, D_MODEL))
```

```python
import jax
import jax.numpy as jnp
from jax import lax
from jax.experimental import pallas as pl
from jax.experimental.pallas import tpu as pltpu

F32 = jnp.float32
BF16 = jnp.bfloat16

D_MODEL = 2048
N_PROMPT_SEQ = 4
SEQ = 2048
N_PROMPT = N_PROMPT_SEQ * SEQ
N_SAMPLE = 128
N_ROWS = N_PROMPT + N_SAMPLE
CHUNK = 128
N_CHUNKS = N_ROWS // CHUNK
CHUNKS_PER_SEQ = SEQ // CHUNK
A_GROUPS = 8
A_GROUP_DIM = D_MODEL // A_GROUPS
CONV_WIDTH = 31
CONV_HIST = CONV_WIDTH - 1
HIST_BLOCK = 32
MOE_GROUPS = 4
EXPERTS_PER_GROUP = 8
N_EXPERTS = MOE_GROUPS * EXPERTS_PER_GROUP
TOP_K = 2
D_EXPERT = D_MODEL // 2
PLE_DIM = 256
EPS = 1e-6
LANES = 128

XN_TM = 512
IN_TM = 1664
IN_TN = 512
IN_COL_TILES = D_MODEL // IN_TN
ACT_BLOCKS = 5
ACT_GU, ACT_GV, ACT_GLU, ACT_SGA, ACT_SGB = range(ACT_BLOCKS)
MIX_TM = 512
MIX_PROMPT_STEPS = N_PROMPT // MIX_TM
FIN_TM = 256
FIN_PROMPT_STEPS = N_PROMPT // FIN_TM
ROW_BLOCK = 128
SUBLANES = 8
BLOCK_TILES = ROW_BLOCK // SUBLANES
N_PAIRS = N_ROWS * TOP_K
MAX_BLOCKS = N_PAIRS // ROW_BLOCK + N_EXPERTS
ITEM_BLOCKS = 8
ITEM_ROWS = ITEM_BLOCKS * ROW_BLOCK
MAX_ITEMS = (MAX_BLOCKS + (ITEM_BLOCKS - 1) * N_EXPERTS) // ITEM_BLOCKS
N_SLOTS = MAX_BLOCKS * ROW_BLOCK
SLOT_TABLE = N_SLOTS + ITEM_ROWS
EXPERT_SLICES = 4
EXPERT_TN = D_EXPERT // EXPERT_SLICES
PAIR_REGION = 8448
DUMP_ROW0 = PAIR_REGION + N_ROWS
PAIR_ROWS = DUMP_ROW0 + ROW_BLOCK

VMEM_LIMIT = 56 * 1024 * 1024


def _rms(x, g):
    return x * lax.rsqrt(jnp.mean(x * x, axis=-1, keepdims=True) + EPS) * g


def _layer_norm(x, g, b):
    mu = jnp.mean(x, axis=-1, keepdims=True)
    xc = x - mu
    return xc * lax.rsqrt(jnp.mean(xc * xc, axis=-1, keepdims=True) + EPS) * g + b


def _sigmoid(x):
    return 1.0 / (1.0 + jnp.exp(-x))


def _silu(x):
    return x * _sigmoid(x)


def _gelu(x):
    return jax.nn.gelu(x, approximate=True)


def _bdot(a, b):
    return jnp.dot(a, b, preferred_element_type=F32)


STAGE_ROWS = 256


def _load_weight_bf16(w_hbm, dst_ref, stage_ref, sem):
    n_chunks = w_hbm.shape[0] // STAGE_ROWS

    def chunk_copy(j):
        return pltpu.make_async_copy(w_hbm.at[pl.ds(j * STAGE_ROWS, STAGE_ROWS)], stage_ref.at[j % 2], sem.at[j % 2])

    chunk_copy(0).start()
    for j in range(n_chunks):
        if j + 1 < n_chunks:
            chunk_copy(j + 1).start()
        chunk_copy(j).wait()
        dst_ref[pl.ds(j * STAGE_ROWS, STAGE_ROWS), :] = stage_ref[j % 2].astype(BF16)


_WEIGHT_STAGE = [pltpu.VMEM((2, STAGE_ROWS, D_MODEL), F32), pltpu.SemaphoreType.DMA((2,))]


def _xnorm_kernel(xp_ref, xs_ref, nm_ref, xn_ref):
    i = pl.program_id(0)

    @pl.when(i < N_PROMPT // XN_TM)
    def _():
        xn_ref[...] = _rms(xp_ref[...], nm_ref[...]).astype(BF16)

    @pl.when(i == N_PROMPT // XN_TM)
    def _():
        xn_ref[0:N_SAMPLE, :] = _rms(xs_ref[...], nm_ref[...]).astype(BF16)
        xn_ref[N_SAMPLE:, :] = jnp.zeros((XN_TM - N_SAMPLE, D_MODEL), BF16)


def _xnorm(x_prompt, x_sample, norm_mix):
    last = N_PROMPT // XN_TM - 1
    return pl.pallas_call(
        _xnorm_kernel,
        out_shape=jax.ShapeDtypeStruct((N_ROWS, D_MODEL), BF16),
        grid=(N_PROMPT // XN_TM + 1,),
        in_specs=[pl.BlockSpec((XN_TM, D_MODEL), lambda i: (jnp.minimum(i, last), 0)),
                  pl.BlockSpec((N_SAMPLE, D_MODEL), lambda i: (0, 0)),
                  pl.BlockSpec((1, D_MODEL), lambda i: (0, 0))],
        out_specs=pl.BlockSpec((XN_TM, D_MODEL), lambda i: (i, 0)),
        name="xnorm",
    )(x_prompt, x_sample, norm_mix)


def _in_w_col(j):
    t = IN_COL_TILES
    jj = j - 2 * t
    glu_col = 2 * t + (jj % 2) * t + jj // 2
    return jnp.where((j >= 2 * t) & (j < 4 * t), glu_col, j)


def _in_out_col(j):
    t = IN_COL_TILES
    return jnp.where(j < 2 * t, j, jnp.where(j < 4 * t, 2 * t + (j - 2 * t) // 2, j - t))


def _in_proj_kernel(xn_ref, w_ref, o_ref, val_ref):
    j = pl.program_id(1)
    t = IN_COL_TILES

    def z():
        return _bdot(xn_ref[...], w_ref[...].astype(BF16))

    @pl.when(j < 2 * t)
    def _():
        o_ref[...] = _gelu(z())

    @pl.when((j >= 2 * t) & (j < 4 * t) & (j % 2 == 0))
    def _():
        val_ref[...] = z()

    @pl.when((j >= 2 * t) & (j < 4 * t) & (j % 2 == 1))
    def _():
        o_ref[...] = val_ref[...] * _sigmoid(z())

    @pl.when(j >= 4 * t)
    def _():
        o_ref[...] = _sigmoid(z())


def _in_proj(xn, w_in):
    return pl.pallas_call(
        _in_proj_kernel,
        out_shape=jax.ShapeDtypeStruct((N_ROWS, ACT_BLOCKS * D_MODEL), F32),
        grid=(N_ROWS // IN_TM, 6 * IN_COL_TILES),
        in_specs=[
            pl.BlockSpec((IN_TM, D_MODEL), lambda i, j: (i, 0)),
            pl.BlockSpec((D_MODEL, IN_TN), lambda i, j: (0, _in_w_col(j))),
        ],
        out_specs=pl.BlockSpec((IN_TM, IN_TN), lambda i, j: (i, _in_out_col(j))),
        scratch_shapes=[pltpu.VMEM((IN_TM, IN_TN), F32)],
        compiler_params=pltpu.CompilerParams(
            dimension_semantics=("arbitrary", "arbitrary"), vmem_limit_bytes=VMEM_LIMIT),
        name="in_proj",
    )(xn, w_in)


SCONV_B = 16


def _sconv_kernel(st_ref, glu_ref, w_ref, wl_ref, cb_ref, new_ref, conv_ref):
    glu = glu_ref[...]
    new_ref[0:CONV_HIST - 1] = st_ref[1:CONV_HIST]
    new_ref[CONV_HIST - 1] = glu
    acc = st_ref[0] * w_ref[0:1, :]
    for k in range(1, CONV_HIST):
        acc = acc + st_ref[k] * w_ref[k:k + 1, :]
    conv_ref[...] = acc + glu * wl_ref[...] + cb_ref[...]


def _sconv(state, act, conv_w, conv_b):
    return pl.pallas_call(
        _sconv_kernel,
        out_shape=(
            jax.ShapeDtypeStruct((CONV_HIST, N_SAMPLE, D_MODEL), F32),
            jax.ShapeDtypeStruct((N_SAMPLE, D_MODEL), F32),
        ),
        grid=(N_SAMPLE // SCONV_B,),
        in_specs=[
            pl.BlockSpec((CONV_HIST, SCONV_B, D_MODEL), lambda i: (0, i, 0)),
            pl.BlockSpec((SCONV_B, D_MODEL), lambda i: (N_PROMPT // SCONV_B + i, ACT_GLU)),
            pl.BlockSpec((CONV_HIST, D_MODEL), lambda i: (0, 0)),
            pl.BlockSpec((1, D_MODEL), lambda i: (0, 0)),
            pl.BlockSpec((1, D_MODEL), lambda i: (0, 0)),
        ],
        out_specs=(
            pl.BlockSpec((CONV_HIST, SCONV_B, D_MODEL), lambda i: (0, i, 0)),
            pl.BlockSpec((SCONV_B, D_MODEL), lambda i: (i, 0)),
        ),
        name="sconv",
    )(state, act, conv_w[:CONV_HIST], conv_w[CONV_HIST:], conv_b)


CONV_STRIP = 256
WIN_ROWS = HIST_BLOCK + CHUNK


def _seqmix_kernel(gv_ref, gu_ref, cur_ref, prev_ref, sga_ref, sgb_ref, convs_ref, wmix_ref, bmix_ref, w00_ref,
                   b00_ref, lvg_ref, lvb_ref, cw_ref, cb_ref, lcg_ref, lcb_ref, wa_hbm, wb_hbm,
                   m_ref, vns_ref, win_ref, conv_ref, a_ref, b_ref, wa_ref, wb_ref, stage_ref, sem):
    c = pl.program_id(0)
    is_sample = c == N_CHUNKS - 1

    @pl.when(c == 0)
    def _():
        _load_weight_bf16(wa_hbm, wa_ref, stage_ref, sem)
        _load_weight_bf16(wb_hbm, wb_ref, stage_ref, sem)

    fresh = (c % CHUNKS_PER_SEQ) == 0
    win_ref[0:HIST_BLOCK, :] = jnp.where(fresh, 0.0, prev_ref[...]).astype(BF16).astype(F32)
    win_ref[HIST_BLOCK:WIN_ROWS, :] = cur_ref[...].astype(BF16).astype(F32)
    off = HIST_BLOCK - CONV_HIST

    vn = _layer_norm(gv_ref[...], lvg_ref[...], lvb_ref[...])
    vns_ref[...] = vn

    vnb = vn.astype(BF16)
    for g in range(A_GROUPS):
        sl = slice(g * A_GROUP_DIM, (g + 1) * A_GROUP_DIM)
        s_chunk = _bdot(wmix_ref[g], vnb[:, sl]) + bmix_ref[:, sl]
        s_first = vn[:, sl] * w00_ref[:, sl] + b00_ref[:, sl]
        a_ref[:, sl] = (gu_ref[:, sl] * jnp.where(is_sample, s_first, s_chunk)).astype(BF16)
    pa = _bdot(a_ref[...], wa_ref[...])

    def phase_sum(r, cs):
        rows = CHUNK + (SUBLANES if r else 0)
        total = None
        for k in range(CONV_WIDTH):
            q, rk = divmod(off + k, SUBLANES)
            if rk == r:
                term = win_ref[SUBLANES * q:SUBLANES * q + rows, cs] * cw_ref[k:k + 1, cs].astype(BF16).astype(F32)
                total = term if total is None else total + term
        return total[r:r + CHUNK, :]

    for s0 in range(0, D_MODEL, CONV_STRIP):
        cs = slice(s0, s0 + CONV_STRIP)
        acc = phase_sum(0, cs)
        for r in range(1, SUBLANES):
            acc = acc + phase_sum(r, cs)
        conv_ref[:, cs] = acc + cb_ref[:, cs]
    conv = jnp.where(is_sample, convs_ref[...], conv_ref[...])
    b_ref[...] = _silu(_layer_norm(conv, lcg_ref[...], lcb_ref[...])).astype(BF16)

    pb = _bdot(b_ref[...], wb_ref[...])
    m_ref[...] = (sga_ref[...] * pa + sgb_ref[...] * pb).astype(BF16)


def _seqmix(act, conv_s, wmix, bmix, w00, b00, ln_v_g, ln_v_b, conv_w, conv_b, ln_c_g, ln_c_b, wa, wb):
    row = lambda c: (c, 0)
    const2 = lambda c: (0, 0)
    vec = pl.BlockSpec((1, D_MODEL), const2)
    hbm = pl.BlockSpec(memory_space=pltpu.MemorySpace.HBM)
    hist_per_chunk = CHUNK // HIST_BLOCK
    act_block = lambda blk: pl.BlockSpec((CHUNK, D_MODEL), lambda c: (c, blk))
    return pl.pallas_call(
        _seqmix_kernel,
        out_shape=(
            jax.ShapeDtypeStruct((N_ROWS, D_MODEL), BF16),
            jax.ShapeDtypeStruct((N_SAMPLE, D_MODEL), F32),
        ),
        grid=(N_CHUNKS,),
        in_specs=[
            act_block(ACT_GV), act_block(ACT_GU), act_block(ACT_GLU),
            pl.BlockSpec((HIST_BLOCK, D_MODEL), lambda c: (jnp.maximum(c * hist_per_chunk - 1, 0), ACT_GLU)),
            act_block(ACT_SGA), act_block(ACT_SGB),
            pl.BlockSpec((N_SAMPLE, D_MODEL), const2),
            pl.BlockSpec((A_GROUPS, CHUNK, CHUNK), lambda c: (0, 0, 0)),
            pl.BlockSpec((CHUNK, D_MODEL), const2),
            vec, vec, vec, vec,
            pl.BlockSpec((CONV_WIDTH + 1, D_MODEL), const2),
            vec, vec, vec,
            hbm, hbm,
        ],
        out_specs=(
            pl.BlockSpec((CHUNK, D_MODEL), row),
            pl.BlockSpec((N_SAMPLE, D_MODEL), const2),
        ),
        scratch_shapes=[pltpu.VMEM((WIN_ROWS, D_MODEL), F32),
                        pltpu.VMEM((CHUNK, D_MODEL), F32),
                        pltpu.VMEM((CHUNK, D_MODEL), BF16), pltpu.VMEM((CHUNK, D_MODEL), BF16),
                        pltpu.VMEM((D_MODEL, D_MODEL), BF16), pltpu.VMEM((D_MODEL, D_MODEL), BF16)] + _WEIGHT_STAGE,
        compiler_params=pltpu.CompilerParams(
            dimension_semantics=("arbitrary",), vmem_limit_bytes=VMEM_LIMIT),
        name="seqmix",
    )(act, act, act, act, act, act, conv_s, wmix, bmix, w00, b00, ln_v_g, ln_v_b, conv_w, conv_b, ln_c_g, ln_c_b,
      wa, wb)


def _route(logits):
    col = lax.broadcasted_iota(jnp.int32, logits.shape, 1).astype(F32)
    neg = jnp.float32(-jnp.inf)
    big = jnp.float32(1e9)
    lg = jnp.where(col < MOE_GROUPS, logits, neg)
    gmax = jnp.max(lg, axis=-1, keepdims=True)
    gi = jnp.min(jnp.where(lg == gmax, col, big), axis=-1, keepdims=True)
    gw = 1.0 / jnp.sum(jnp.exp(lg - gmax), axis=-1, keepdims=True)
    lo = MOE_GROUPS + gi * EXPERTS_PER_GROUP
    le = jnp.where((col >= lo) & (col < lo + EXPERTS_PER_GROUP), logits, neg)
    m1 = jnp.max(le, axis=-1, keepdims=True)
    i1 = jnp.min(jnp.where(le == m1, col, big), axis=-1, keepdims=True)
    le2 = jnp.where(col == i1, neg, le)
    m2 = jnp.max(le2, axis=-1, keepdims=True)
    i2 = jnp.min(jnp.where(le2 == m2, col, big), axis=-1, keepdims=True)
    e = jnp.exp(m2 - m1)
    w1 = gw / (1.0 + e)
    w2 = gw * e / (1.0 + e)
    eid = jnp.where(col == 0, i1 - MOE_GROUPS, jnp.where(col == 1, i2 - MOE_GROUPS, 0.0)).astype(jnp.int32)
    ew = jnp.where(col == 0, w1, jnp.where(col == 1, w2, 0.0))
    return eid, ew


def _mixout_rows(m, x, wo_ref, nf_ref, wr_ref, br_ref):
    h = x + _bdot(m, wo_ref[...])
    hn = _rms(h, nf_ref[...])
    logits = _bdot(hn.astype(BF16), wr_ref[...]) + br_ref[...]
    eid, ew = _route(logits)
    return h, eid, ew


def _mixout_kernel(mp_ref, xp_ref, ms_ref, xs_ref, wo_hbm, nf_ref, wr_ref, br_ref, h_ref, eid_ref, ew_ref,
                   wo_ref, stage_ref, sem):
    i = pl.program_id(0)

    @pl.when(i == 0)
    def _():
        _load_weight_bf16(wo_hbm, wo_ref, stage_ref, sem)

    @pl.when(i < MIX_PROMPT_STEPS)
    def _():
        h, eid, ew = _mixout_rows(mp_ref[...], xp_ref[...], wo_ref, nf_ref, wr_ref, br_ref)
        h_ref[...] = h
        eid_ref[...] = eid
        ew_ref[...] = ew

    @pl.when(i == MIX_PROMPT_STEPS)
    def _():
        h, eid, ew = _mixout_rows(ms_ref[...], xs_ref[...], wo_ref, nf_ref, wr_ref, br_ref)
        for ref, val in ((h_ref, h), (eid_ref, eid), (ew_ref, ew)):
            ref[0:N_SAMPLE, :] = val
            ref[N_SAMPLE:, :] = jnp.zeros((MIX_TM - N_SAMPLE, val.shape[1]), val.dtype)


def _mixout(m_all, x_prompt, x_sample, wo, norm_ffn, w_route, b_route):
    last = MIX_PROMPT_STEPS - 1
    row = lambda i: (i, 0)
    prow = lambda i: (jnp.minimum(i, last), 0)
    const2 = lambda i: (0, 0)
    hbm = pl.BlockSpec(memory_space=pltpu.MemorySpace.HBM)
    return pl.pallas_call(
        _mixout_kernel,
        out_shape=(
            jax.ShapeDtypeStruct((N_ROWS, D_MODEL), F32),
            jax.ShapeDtypeStruct((N_ROWS, LANES), jnp.int32),
            jax.ShapeDtypeStruct((N_ROWS, LANES), F32),
        ),
        grid=(MIX_PROMPT_STEPS + 1,),
        in_specs=[pl.BlockSpec((MIX_TM, D_MODEL), prow), pl.BlockSpec((MIX_TM, D_MODEL), prow),
                  pl.BlockSpec((N_SAMPLE, D_MODEL), lambda i: (N_PROMPT // N_SAMPLE, 0)),
                  pl.BlockSpec((N_SAMPLE, D_MODEL), const2),
                  hbm,
                  pl.BlockSpec((1, D_MODEL), const2),
                  pl.BlockSpec((D_MODEL, LANES), const2),
                  pl.BlockSpec((1, LANES), const2)],
        out_specs=(pl.BlockSpec((MIX_TM, D_MODEL), row), pl.BlockSpec((MIX_TM, LANES), row),
                   pl.BlockSpec((MIX_TM, LANES), row)),
        scratch_shapes=[pltpu.VMEM((D_MODEL, D_MODEL), BF16)] + _WEIGHT_STAGE,
        compiler_params=pltpu.CompilerParams(
            dimension_semantics=("arbitrary",), vmem_limit_bytes=VMEM_LIMIT),
        name="mixout",
    )(m_all, x_prompt, m_all, x_sample, wo, norm_ffn, w_route, b_route)


def _experts_kernel(item_e, item_blk0, item_nb, src_hbm, dst_hbm, h_hbm, nf_ref, wg_ref, wu_ref, wd_ref,
                    y_hbm, src_s, dst_s, pend_s, xg_ref, yst_ref, xs_ref, hm_ref, wgb_ref, wub_ref, wdb_ref,
                    idx_sem, g_sem, s_sem):
    w = pl.program_id(0)
    s = pl.program_id(1)
    nb = item_nb[w]
    nxt = jnp.minimum(w + 1, MAX_ITEMS - 1)
    nb_next = jnp.where(w + 1 < MAX_ITEMS, item_nb[nxt], 0)

    def table_fetch(tbl_hbm, tbl_s, sem_idx, item):
        slot0 = pl.multiple_of(item_blk0[item] * ROW_BLOCK, ROW_BLOCK)
        return pltpu.make_async_copy(tbl_hbm.at[pl.ds(slot0, ITEM_ROWS)], tbl_s, idx_sem.at[sem_idx])

    def row_copy_in(t, u, blk):
        src = src_s[t * SUBLANES + u]
        return pltpu.make_async_copy(h_hbm.at[src >> 3, pl.ds(src & (SUBLANES - 1), 1)],
                                     xg_ref.at[t, pl.ds(u, 1)], g_sem.at[blk])

    def row_copy_out(t, u, blk):
        dst = dst_s[t * SUBLANES + u]
        return pltpu.make_async_copy(yst_ref.at[t, pl.ds(u, 1)],
                                     y_hbm.at[dst >> 3, pl.ds(dst & (SUBLANES - 1), 1)], s_sem.at[blk])

    def block_rows(blk):
        return pl.ds(pl.multiple_of(blk * ROW_BLOCK, ROW_BLOCK), ROW_BLOCK)

    def block_tiles(blk):
        return pl.ds(pl.multiple_of(blk * BLOCK_TILES, BLOCK_TILES), BLOCK_TILES)

    def block_copy_wait(sem, blk):
        pltpu.make_async_copy(xg_ref.at[block_tiles(blk)], xg_ref.at[block_tiles(blk)], sem.at[blk]).wait()

    def for_each_row(n_blocks, row_copy):
        def tile(t, carry):
            for u in range(SUBLANES):
                row_copy(t, u, lax.shift_right_logical(t, BLOCK_TILES.bit_length() - 1)).start()
            return carry
        lax.fori_loop(0, n_blocks * BLOCK_TILES, tile, 0)

    def issue_gather(n_blocks):
        for_each_row(n_blocks, row_copy_in)

    def drain_scatter():
        def drain(blk, carry):
            block_copy_wait(s_sem, blk)
            return carry
        lax.fori_loop(0, pend_s[0], drain, 0)
        pend_s[0] = 0

    @pl.when((w == 0) & (s == 0))
    def _():
        pad_tiles = pl.ds(0, BLOCK_TILES)
        xg_ref[pad_tiles] = jnp.zeros((BLOCK_TILES, SUBLANES, D_MODEL), F32)
        for row0 in (N_ROWS, DUMP_ROW0):
            cz = pltpu.make_async_copy(xg_ref.at[pad_tiles], y_hbm.at[pl.ds(row0 // SUBLANES, BLOCK_TILES)],
                                       idx_sem.at[0])
            cz.start()
            cz.wait()
        pend_s[0] = 0
        ci = table_fetch(src_hbm, src_s, 0, w)
        ci.start()
        ci.wait()
        issue_gather(nb)

    @pl.when((s == 0) & (nb > 0))
    def _():
        table_fetch(dst_hbm, dst_s, 1, w).start()

        def land(blk, carry):
            block_copy_wait(g_sem, blk)
            x = xg_ref[block_tiles(blk)].reshape(ROW_BLOCK, D_MODEL)
            xs_ref[block_rows(blk), :] = _rms(x, nf_ref[...]).astype(BF16)
            return carry
        lax.fori_loop(0, nb, land, 0)

    @pl.when((s == 1) & (nb_next > 0))
    def _():
        table_fetch(src_hbm, src_s, 0, nxt).start()

    @pl.when((s == 2) & (nb_next > 0))
    def _():
        table_fetch(src_hbm, src_s, 0, nxt).wait()
        issue_gather(nb_next)

    def over_row_blocks(fn):
        def four_blocks(j, carry):
            fn(pl.multiple_of(j * (4 * ROW_BLOCK), 4 * ROW_BLOCK), 4 * ROW_BLOCK)
            return carry
        lax.fori_loop(0, nb >> 2, four_blocks, 0)

        @pl.when((nb & 2) == 2)
        def _():
            fn(pl.multiple_of((nb >> 2) * (4 * ROW_BLOCK), 2 * ROW_BLOCK), 2 * ROW_BLOCK)

        @pl.when((nb & 1) == 1)
        def _():
            fn(pl.multiple_of((nb - 1) * ROW_BLOCK, ROW_BLOCK), ROW_BLOCK)

    @pl.when(nb > 0)
    def _():
        wgb_ref[...] = wg_ref[...].astype(BF16)
        wub_ref[...] = wu_ref[...].astype(BF16)
        wdb_ref[pl.ds(pl.multiple_of(s * EXPERT_TN, EXPERT_TN), EXPERT_TN), :] = wd_ref[...].astype(BF16)

        def gate_up(row0, nrows):
            rows = pl.ds(row0, nrows)
            x = xs_ref[rows, :]
            hm_ref[s, rows, :] = (_silu(_bdot(x, wgb_ref[...])) * _bdot(x, wub_ref[...])).astype(BF16)
        over_row_blocks(gate_up)

    @pl.when((s == EXPERT_SLICES - 1) & (nb > 0))
    def _():
        drain_scatter()

        def down(row0, nrows):
            rows = pl.ds(row0, nrows)
            hm = jnp.concatenate([hm_ref[k, rows, :] for k in range(EXPERT_SLICES)], axis=1)
            tiles = pl.ds(pl.multiple_of(row0 // SUBLANES, BLOCK_TILES), nrows // SUBLANES)
            yst_ref[tiles] = _bdot(hm, wdb_ref[...]).reshape(nrows // SUBLANES, SUBLANES, D_MODEL)
        over_row_blocks(down)

        table_fetch(dst_hbm, dst_s, 1, w).wait()

        for_each_row(nb, row_copy_out)
        pend_s[0] = nb

    @pl.when((w == MAX_ITEMS - 1) & (s == EXPERT_SLICES - 1))
    def _():
        drain_scatter()


def _experts(item_e, item_blk0, item_nb, src_row, dst_row, h_all, norm_ffn, w_gate, w_up, w_down):
    hbm = pl.BlockSpec(memory_space=pltpu.MemorySpace.HBM)

    def w_slice(w, s, inb):
        return jnp.where(inb[w] > 0, s, EXPERT_SLICES - 1)

    grid_spec = pltpu.PrefetchScalarGridSpec(
        num_scalar_prefetch=3,
        grid=(MAX_ITEMS, EXPERT_SLICES),
        in_specs=[
            hbm, hbm, hbm,
            pl.BlockSpec((1, D_MODEL), lambda w, s, ie, ib, inb: (0, 0)),
            pl.BlockSpec((None, D_MODEL, EXPERT_TN), lambda w, s, ie, ib, inb: (ie[w], 0, w_slice(w, s, inb))),
            pl.BlockSpec((None, D_MODEL, EXPERT_TN), lambda w, s, ie, ib, inb: (ie[w], 0, w_slice(w, s, inb))),
            pl.BlockSpec((None, EXPERT_TN, D_MODEL), lambda w, s, ie, ib, inb: (ie[w], w_slice(w, s, inb), 0)),
        ],
        out_specs=hbm,
        scratch_shapes=[
            pltpu.SMEM((ITEM_ROWS,), jnp.int32),
            pltpu.SMEM((ITEM_ROWS,), jnp.int32),
            pltpu.SMEM((1,), jnp.int32),
            pltpu.VMEM((ITEM_ROWS // SUBLANES, SUBLANES, D_MODEL), F32),
            pltpu.VMEM((ITEM_ROWS // SUBLANES, SUBLANES, D_MODEL), F32),
            pltpu.VMEM((ITEM_ROWS, D_MODEL), BF16),
            pltpu.VMEM((EXPERT_SLICES, ITEM_ROWS, EXPERT_TN), BF16),
            pltpu.VMEM((D_MODEL, EXPERT_TN), BF16),
            pltpu.VMEM((D_MODEL, EXPERT_TN), BF16),
            pltpu.VMEM((D_EXPERT, D_MODEL), BF16),
            pltpu.SemaphoreType.DMA((2,)),
            pltpu.SemaphoreType.DMA((ITEM_BLOCKS,)),
            pltpu.SemaphoreType.DMA((ITEM_BLOCKS,)),
        ],
    )
    return pl.pallas_call(
        _experts_kernel,
        out_shape=jax.ShapeDtypeStruct((PAIR_ROWS // SUBLANES, SUBLANES, D_MODEL), F32),
        grid_spec=grid_spec,
        compiler_params=pltpu.CompilerParams(
            dimension_semantics=("arbitrary", "arbitrary"), vmem_limit_bytes=VMEM_LIMIT),
        name="experts",
    )(item_e, item_blk0, item_nb, src_row, dst_row, h_all.reshape(N_ROWS // SUBLANES, SUBLANES, D_MODEL),
      norm_ffn, w_gate, w_up, w_down).reshape(PAIR_ROWS, D_MODEL)


def _dispatch_tables(eid):
    flat_e = eid.reshape(-1)
    onehot = (flat_e[:, None] == jnp.arange(N_EXPERTS, dtype=jnp.int32)[None, :]).astype(jnp.int32)
    rank = jnp.take_along_axis(jnp.cumsum(onehot, axis=0) - onehot, flat_e[:, None], axis=1)[:, 0]
    counts = jnp.sum(onehot, axis=0)
    nblk = (counts + ROW_BLOCK - 1) // ROW_BLOCK
    blk0 = jnp.cumsum(nblk) - nblk
    slot = blk0[flat_e] * ROW_BLOCK + rank
    slot_pair = jnp.full((SLOT_TABLE,), N_PAIRS, jnp.int32).at[slot].set(jnp.arange(N_PAIRS, dtype=jnp.int32))
    slot_id = jnp.arange(SLOT_TABLE, dtype=jnp.int32)
    src_row = jnp.minimum(slot_pair // TOP_K, N_ROWS - 1)
    dst_row = jnp.where(slot_pair < N_PAIRS, (slot_pair % TOP_K) * PAIR_REGION + slot_pair // TOP_K,
                        DUMP_ROW0 + slot_id % ROW_BLOCK)
    nitem = (nblk + ITEM_BLOCKS - 1) // ITEM_BLOCKS
    item_end = jnp.cumsum(nitem)
    total = item_end[-1]
    idx = jnp.arange(MAX_ITEMS, dtype=jnp.int32)
    live = idx < total
    e_of = jnp.minimum(jnp.searchsorted(item_end, jnp.minimum(idx, total - 1), side='right'),
                       N_EXPERTS - 1).astype(jnp.int32)
    local = jnp.minimum(idx, total - 1) - (item_end - nitem)[e_of]
    item_blk0 = (blk0[e_of] + local * ITEM_BLOCKS).astype(jnp.int32)
    item_nb = jnp.where(live, jnp.clip(nblk[e_of] - local * ITEM_BLOCKS, 0, ITEM_BLOCKS), 0).astype(jnp.int32)
    return e_of, item_blk0, item_nb, src_row, dst_row


def _final_rows(h, y0, y1, ew, p, wg_ref, wp_ref, np_ref, fn_ref):
    h = h + (ew[:, 0:1] * y0 + ew[:, 1:2] * y1)
    gate = _sigmoid(_bdot(_rms(h, np_ref[...]).astype(BF16), wg_ref[...]))
    h = h + gate * _bdot(p.astype(BF16), wp_ref[...])
    return _rms(h, fn_ref[...])


def _final_kernel(h_ref, y0_ref, y1_ref, ew_ref, p_ref, hs_ref, y0s_ref, y1s_ref, ews_ref, ps_ref,
                  wg_hbm, wp_hbm, np_ref, fn_ref, op_ref, os_ref, wg_ref, wp_ref, stage_ref, sem):
    i = pl.program_id(0)

    @pl.when(i == 0)
    def _():
        _load_weight_bf16(wg_hbm, wg_ref, stage_ref, sem)
        _load_weight_bf16(wp_hbm, wp_ref, stage_ref, sem)

    @pl.when(i < FIN_PROMPT_STEPS)
    def _():
        op_ref[...] = _final_rows(h_ref[...], y0_ref[...], y1_ref[...], ew_ref[...], p_ref[...],
                                  wg_ref, wp_ref, np_ref, fn_ref)

    @pl.when(i == FIN_PROMPT_STEPS)
    def _():
        os_ref[...] = _final_rows(hs_ref[...], y0s_ref[...], y1s_ref[...], ews_ref[...], ps_ref[...],
                                  wg_ref, wp_ref, np_ref, fn_ref)


def _final(h_all, ypairs, ew, p_prompt, p_sample, w_gate, w_proj, norm_ple, final_norm):
    last = FIN_PROMPT_STEPS - 1
    prow = lambda i: (jnp.minimum(i, last), 0)
    const2 = lambda i: (0, 0)
    hbm = pl.BlockSpec(memory_space=pltpu.MemorySpace.HBM)
    sblk = N_PROMPT // N_SAMPLE
    pact = pl.BlockSpec((FIN_TM, D_MODEL), prow)
    sact = lambda off: pl.BlockSpec((N_SAMPLE, D_MODEL), lambda i: (off + sblk, 0))
    return pl.pallas_call(
        _final_kernel,
        out_shape=(jax.ShapeDtypeStruct((N_PROMPT, D_MODEL), F32), jax.ShapeDtypeStruct((N_SAMPLE, D_MODEL), F32)),
        grid=(FIN_PROMPT_STEPS + 1,),
        in_specs=[pact, pact,
                  pl.BlockSpec((FIN_TM, D_MODEL), lambda i: (jnp.minimum(i, last) + PAIR_REGION // FIN_TM, 0)),
                  pl.BlockSpec((FIN_TM, LANES), prow),
                  pl.BlockSpec((FIN_TM, PLE_DIM), prow),
                  sact(0), sact(0), sact(PAIR_REGION // N_SAMPLE),
                  pl.BlockSpec((N_SAMPLE, LANES), lambda i: (sblk, 0)),
                  pl.BlockSpec((N_SAMPLE, PLE_DIM), const2),
                  hbm, hbm,
                  pl.BlockSpec((1, D_MODEL), const2),
                  pl.BlockSpec((1, D_MODEL), const2)],
        out_specs=(pact, pl.BlockSpec((N_SAMPLE, D_MODEL), const2)),
        scratch_shapes=[pltpu.VMEM((D_MODEL, D_MODEL), BF16), pltpu.VMEM((PLE_DIM, D_MODEL), BF16)] + _WEIGHT_STAGE,
        compiler_params=pltpu.CompilerParams(
            dimension_semantics=("arbitrary",), vmem_limit_bytes=VMEM_LIMIT),
        name="final",
    )(h_all, ypairs, ypairs, ew, p_prompt, h_all, ypairs, ypairs, ew, p_sample, w_gate, w_proj, norm_ple, final_norm)


def kernel(x_prompt, x_sample, state_conv, p_prompt, p_sample, norm_mix, w_in, ln_v_g, ln_v_b, w_spatial,
           b_spatial, w_proj_a, conv_w, conv_b, ln_c_g, ln_c_b, w_proj_b, w_out, norm_ffn, w_router_group,
           b_router_group, w_router_expert, b_router_expert, w_exp_gate, w_exp_up, w_exp_down, norm_ple,
           w_ple_gate, w_ple_proj, final_norm):
    assert w_in.shape[0] == 1, "single layer"
    vec = lambda v: v.reshape(1, -1)
    xp = x_prompt.reshape(N_PROMPT, D_MODEL)
    xs = x_sample.reshape(N_SAMPLE, D_MODEL)
    xn = _xnorm(xp, xs, vec(norm_mix[0]))

    w_s, b_s = w_spatial[0], b_spatial[0]
    wmix = jnp.tril(w_s).astype(BF16)
    bmix = jnp.repeat(b_s.T, A_GROUP_DIM, axis=1)
    w00 = jnp.repeat(w_s[:, 0, 0], A_GROUP_DIM).reshape(1, D_MODEL)
    b00 = jnp.repeat(b_s[:, 0], A_GROUP_DIM).reshape(1, D_MODEL)

    act = _in_proj(xn, w_in[0])

    cw = conv_w[0]
    new_state, conv_s = _sconv(jnp.transpose(state_conv[0], (1, 0, 2)), act, cw, vec(conv_b[0]))
    new_conv_sample = jnp.transpose(new_state, (1, 0, 2))[None]
    cw_pad = jnp.concatenate([cw, jnp.zeros((1, D_MODEL), F32)], axis=0)
    m_all, vn_s = _seqmix(act, conv_s, wmix, bmix, w00, b00, vec(ln_v_g[0]), vec(ln_v_b[0]), cw_pad,
                          vec(conv_b[0]), vec(ln_c_g[0]), vec(ln_c_b[0]), w_proj_a[0], w_proj_b[0])

    w_route = jnp.zeros((D_MODEL, LANES), F32)
    w_route = w_route.at[:, :MOE_GROUPS].set(w_router_group[0])
    w_route = w_route.at[:, MOE_GROUPS:MOE_GROUPS + N_EXPERTS].set(w_router_expert[0])
    b_route = jnp.zeros((1, LANES), F32)
    b_route = b_route.at[0, :MOE_GROUPS].set(b_router_group[0])
    b_route = b_route.at[0, MOE_GROUPS:MOE_GROUPS + N_EXPERTS].set(b_router_expert[0])

    h_all, eid, ew = _mixout(m_all, xp, xs, w_out[0], vec(norm_ffn[0]), w_route.astype(BF16), b_route)

    tables = _dispatch_tables(eid[:, :TOP_K])
    ypairs = _experts(*tables, h_all, vec(norm_ffn[0]), w_exp_gate[0], w_exp_up[0], w_exp_down[0])

    y_prompt, y_sample = _final(h_all, ypairs, ew, p_prompt[0].reshape(N_PROMPT, PLE_DIM),
                                p_sample[0].reshape(N_SAMPLE, PLE_DIM), w_ple_gate[0], w_ple_proj[0],
                                vec(norm_ple[0]), vec(final_norm))
    y_prompt = y_prompt.reshape(N_PROMPT_SEQ, SEQ, D_MODEL)
    y_sample = y_sample.reshape(N_SAMPLE, 1, D_MODEL)
    glu_cols = slice(ACT_GLU * D_MODEL, (ACT_GLU + 1) * D_MODEL)
    new_conv_prompt = jnp.stack([act[(b + 1) * SEQ - CONV_HIST:(b + 1) * SEQ, glu_cols]
                                 for b in range(N_PROMPT_SEQ)])[None]
    return (y_prompt, y_sample, new_conv_prompt, new_conv_sample, vn_s.reshape(1, N_SAMPLE, 1, D_MODEL))
```

```python
import jax
import jax.numpy as jnp
from jax import lax
from jax.experimental import pallas as pl
from jax.experimental.pallas import tpu as pltpu

F32 = jnp.float32
BF16 = jnp.bfloat16

D_MODEL = 2048
N_PROMPT_SEQ = 4
SEQ = 2048
N_PROMPT = N_PROMPT_SEQ * SEQ
N_SAMPLE = 128
N_ROWS = N_PROMPT + N_SAMPLE
CHUNK = 128
N_CHUNKS = N_ROWS // CHUNK
CHUNKS_PER_SEQ = SEQ // CHUNK
A_GROUPS = 8
A_GROUP_DIM = D_MODEL // A_GROUPS
CONV_WIDTH = 31
CONV_HIST = CONV_WIDTH - 1
HIST_BLOCK = 32
MOE_GROUPS = 4
EXPERTS_PER_GROUP = 8
N_EXPERTS = MOE_GROUPS * EXPERTS_PER_GROUP
TOP_K = 2
D_EXPERT = D_MODEL // 2
PLE_DIM = 256
EPS = 1e-6
LANES = 128

XN_TM = 512
IN_TM = 1664
IN_TN = 512
IN_COL_TILES = D_MODEL // IN_TN
ACT_BLOCKS = 5
ACT_GU, ACT_GV, ACT_GLU, ACT_SGA, ACT_SGB = range(ACT_BLOCKS)
MIX_TM = 512
MIX_PROMPT_STEPS = N_PROMPT // MIX_TM
FIN_TM = 256
FIN_PROMPT_STEPS = N_PROMPT // FIN_TM
ROW_BLOCK = 128
SUBLANES = 8
BLOCK_TILES = ROW_BLOCK // SUBLANES
N_PAIRS = N_ROWS * TOP_K
MAX_BLOCKS = N_PAIRS // ROW_BLOCK + N_EXPERTS
ITEM_BLOCKS = 8
ITEM_ROWS = ITEM_BLOCKS * ROW_BLOCK
MAX_ITEMS = (MAX_BLOCKS + (ITEM_BLOCKS - 1) * N_EXPERTS) // ITEM_BLOCKS
N_SLOTS = MAX_BLOCKS * ROW_BLOCK
SLOT_TABLE = N_SLOTS + ITEM_ROWS
EXPERT_SLICES = 4
EXPERT_TN = D_EXPERT // EXPERT_SLICES
PAIR_REGION = 8448
DUMP_ROW0 = PAIR_REGION + N_ROWS
PAIR_ROWS = DUMP_ROW0 + ROW_BLOCK

VMEM_LIMIT = 56 * 1024 * 1024


def _rms(x, g):
    return x * lax.rsqrt(jnp.mean(x * x, axis=-1, keepdims=True) + EPS) * g


def _layer_norm(x, g, b):
    mu = jnp.mean(x, axis=-1, keepdims=True)
    xc = x - mu
    return xc * lax.rsqrt(jnp.mean(xc * xc, axis=-1, keepdims=True) + EPS) * g + b


def _sigmoid(x):
    return 1.0 / (1.0 + jnp.exp(-x))


def _silu(x):
    return x * _sigmoid(x)


def _gelu(x):
    return jax.nn.gelu(x, approximate=True)


def _bdot(a, b):
    return jnp.dot(a, b, preferred_element_type=F32)


STAGE_ROWS = 256


def _load_weight_bf16(w_hbm, dst_ref, stage_ref, sem):
    n_chunks = w_hbm.shape[0] // STAGE_ROWS

    def chunk_copy(j):
        return pltpu.make_async_copy(w_hbm.at[pl.ds(j * STAGE_ROWS, STAGE_ROWS)], stage_ref.at[j % 2], sem.at[j % 2])

    chunk_copy(0).start()
    for j in range(n_chunks):
        if j + 1 < n_chunks:
            chunk_copy(j + 1).start()
        chunk_copy(j).wait()
        dst_ref[pl.ds(j * STAGE_ROWS, STAGE_ROWS), :] = stage_ref[j % 2].astype(BF16)


_WEIGHT_STAGE = [pltpu.VMEM((2, STAGE_ROWS, D_MODEL), F32), pltpu.SemaphoreType.DMA((2,))]


def _xnorm_kernel(xp_ref, xs_ref, nm_ref, xn_ref):
    i = pl.program_id(0)

    @pl.when(i < N_PROMPT // XN_TM)
    def _():
        xn_ref[...] = _rms(xp_ref[...], nm_ref[...]).astype(BF16)

    @pl.when(i == N_PROMPT // XN_TM)
    def _():
        xn_ref[0:N_SAMPLE, :] = _rms(xs_ref[...], nm_ref[...]).astype(BF16)
        xn_ref[N_SAMPLE:, :] = jnp.zeros((XN_TM - N_SAMPLE, D_MODEL), BF16)


def _xnorm(x_prompt, x_sample, norm_mix):
    last = N_PROMPT // XN_TM - 1
    return pl.pallas_call(
        _xnorm_kernel,
        out_shape=jax.ShapeDtypeStruct((N_ROWS, D_MODEL), BF16),
        grid=(N_PROMPT // XN_TM + 1,),
        in_specs=[pl.BlockSpec((XN_TM, D_MODEL), lambda i: (jnp.minimum(i, last), 0)),
                  pl.BlockSpec((N_SAMPLE, D_MODEL), lambda i: (0, 0)),
                  pl.BlockSpec((1, D_MODEL), lambda i: (0, 0))],
        out_specs=pl.BlockSpec((XN_TM, D_MODEL), lambda i: (i, 0)),
        name="xnorm",
    )(x_prompt, x_sample, norm_mix)


def _in_w_col(j):
    t = IN_COL_TILES
    jj = j - 2 * t
    glu_col = 2 * t + (jj % 2) * t + jj // 2
    return jnp.where((j >= 2 * t) & (j < 4 * t), glu_col, j)


def _in_out_col(j):
    t = IN_COL_TILES
    return jnp.where(j < 2 * t, j, jnp.where(j < 4 * t, 2 * t + (j - 2 * t) // 2, j - t))


def _in_proj_kernel(xn_ref, w_ref, o_ref, val_ref):
    j = pl.program_id(1)
    t = IN_COL_TILES

    def z():
        return _bdot(xn_ref[...], w_ref[...].astype(BF16))

    @pl.when(j < 2 * t)
    def _():
        o_ref[...] = _gelu(z())

    @pl.when((j >= 2 * t) & (j < 4 * t) & (j % 2 == 0))
    def _():
        val_ref[...] = z()

    @pl.when((j >= 2 * t) & (j < 4 * t) & (j % 2 == 1))
    def _():
        o_ref[...] = val_ref[...] * _sigmoid(z())

    @pl.when(j >= 4 * t)
    def _():
        o_ref[...] = _sigmoid(z())


def _in_proj(xn, w_in):
    return pl.pallas_call(
        _in_proj_kernel,
        out_shape=jax.ShapeDtypeStruct((N_ROWS, ACT_BLOCKS * D_MODEL), F32),
        grid=(N_ROWS // IN_TM, 6 * IN_COL_TILES),
        in_specs=[
            pl.BlockSpec((IN_TM, D_MODEL), lambda i, j: (i, 0)),
            pl.BlockSpec((D_MODEL, IN_TN), lambda i, j: (0, _in_w_col(j))),
        ],
        out_specs=pl.BlockSpec((IN_TM, IN_TN), lambda i, j: (i, _in_out_col(j))),
        scratch_shapes=[pltpu.VMEM((IN_TM, IN_TN), F32)],
        compiler_params=pltpu.CompilerParams(
            dimension_semantics=("arbitrary", "arbitrary"), vmem_limit_bytes=VMEM_LIMIT),
        name="in_proj",
    )(xn, w_in)


SCONV_B = 16


def _sconv_kernel(st_ref, glu_ref, w_ref, wl_ref, cb_ref, new_ref, conv_ref):
    glu = glu_ref[...]
    new_ref[0:CONV_HIST - 1] = st_ref[1:CONV_HIST]
    new_ref[CONV_HIST - 1] = glu
    acc = st_ref[0] * w_ref[0:1, :]
    for k in range(1, CONV_HIST):
        acc = acc + st_ref[k] * w_ref[k:k + 1, :]
    conv_ref[...] = acc + glu * wl_ref[...] + cb_ref[...]


def _sconv(state, act, conv_w, conv_b):
    return pl.pallas_call(
        _sconv_kernel,
        out_shape=(
            jax.ShapeDtypeStruct((CONV_HIST, N_SAMPLE, D_MODEL), F32),
            jax.ShapeDtypeStruct((N_SAMPLE, D_MODEL), F32),
        ),
        grid=(N_SAMPLE // SCONV_B,),
        in_specs=[
            pl.BlockSpec((CONV_HIST, SCONV_B, D_MODEL), lambda i: (0, i, 0)),
            pl.BlockSpec((SCONV_B, D_MODEL), lambda i: (N_PROMPT // SCONV_B + i, ACT_GLU)),
            pl.BlockSpec((CONV_HIST, D_MODEL), lambda i: (0, 0)),
            pl.BlockSpec((1, D_MODEL), lambda i: (0, 0)),
            pl.BlockSpec((1, D_MODEL), lambda i: (0, 0)),
        ],
        out_specs=(
            pl.BlockSpec((CONV_HIST, SCONV_B, D_MODEL), lambda i: (0, i, 0)),
            pl.BlockSpec((SCONV_B, D_MODEL), lambda i: (i, 0)),
        ),
        name="sconv",
    )(state, act, conv_w[:CONV_HIST], conv_w[CONV_HIST:], conv_b)


CONV_STRIP = 256
WIN_ROWS = HIST_BLOCK + CHUNK


def _seqmix_kernel(gv_ref, gu_ref, cur_ref, prev_ref, sga_ref, sgb_ref, convs_ref, wmix_ref, bmix_ref, w00_ref,
                   b00_ref, lvg_ref, lvb_ref, cw_ref, cb_ref, lcg_ref, lcb_ref, wa_hbm, wb_hbm,
                   m_ref, vns_ref, win_ref, conv_ref, a_ref, b_ref, wa_ref, wb_ref, stage_ref, sem):
    c = pl.program_id(0)
    is_sample = c == N_CHUNKS - 1

    @pl.when(c == 0)
    def _():
        _load_weight_bf16(wa_hbm, wa_ref, stage_ref, sem)
        _load_weight_bf16(wb_hbm, wb_ref, stage_ref, sem)

    fresh = (c % CHUNKS_PER_SEQ) == 0
    win_ref[0:HIST_BLOCK, :] = jnp.where(fresh, 0.0, prev_ref[...]).astype(BF16).astype(F32)
    win_ref[HIST_BLOCK:WIN_ROWS, :] = cur_ref[...].astype(BF16).astype(F32)
    off = HIST_BLOCK - CONV_HIST

    vn = _layer_norm(gv_ref[...], lvg_ref[...], lvb_ref[...])
    vns_ref[...] = vn

    vnb = vn.astype(BF16)
    for g in range(A_GROUPS):
        sl = slice(g * A_GROUP_DIM, (g + 1) * A_GROUP_DIM)
        s_chunk = _bdot(wmix_ref[g], vnb[:, sl]) + bmix_ref[:, sl]
        s_first = vn[:, sl] * w00_ref[:, sl] + b00_ref[:, sl]
        a_ref[:, sl] = (gu_ref[:, sl] * jnp.where(is_sample, s_first, s_chunk)).astype(BF16)
    pa = _bdot(a_ref[...], wa_ref[...])

    def phase_sum(r, cs):
        rows = CHUNK + (SUBLANES if r else 0)
        total = None
        for k in range(CONV_WIDTH):
            q, rk = divmod(off + k, SUBLANES)
            if rk == r:
                term = win_ref[SUBLANES * q:SUBLANES * q + rows, cs] * cw_ref[k:k + 1, cs].astype(BF16).astype(F32)
                total = term if total is None else total + term
        return total[r:r + CHUNK, :]

    for s0 in range(0, D_MODEL, CONV_STRIP):
        cs = slice(s0, s0 + CONV_STRIP)
        acc = phase_sum(0, cs)
        for r in range(1, SUBLANES):
            acc = acc + phase_sum(r, cs)
        conv_ref[:, cs] = acc + cb_ref[:, cs]
    conv = jnp.where(is_sample, convs_ref[...], conv_ref[...])
    b_ref[...] = _silu(_layer_norm(conv, lcg_ref[...], lcb_ref[...])).astype(BF16)

    pb = _bdot(b_ref[...], wb_ref[...])
    m_ref[...] = (sga_ref[...] * pa + sgb_ref[...] * pb).astype(BF16)


def _seqmix(act, conv_s, wmix, bmix, w00, b00, ln_v_g, ln_v_b, conv_w, conv_b, ln_c_g, ln_c_b, wa, wb):
    row = lambda c: (c, 0)
    const2 = lambda c: (0, 0)
    vec = pl.BlockSpec((1, D_MODEL), const2)
    hbm = pl.BlockSpec(memory_space=pltpu.MemorySpace.HBM)
    hist_per_chunk = CHUNK // HIST_BLOCK
    act_block = lambda blk: pl.BlockSpec((CHUNK, D_MODEL), lambda c: (c, blk))
    return pl.pallas_call(
        _seqmix_kernel,
        out_shape=(
            jax.ShapeDtypeStruct((N_ROWS, D_MODEL), BF16),
            jax.ShapeDtypeStruct((N_SAMPLE, D_MODEL), F32),
        ),
        grid=(N_CHUNKS,),
        in_specs=[
            act_block(ACT_GV), act_block(ACT_GU), act_block(ACT_GLU),
            pl.BlockSpec((HIST_BLOCK, D_MODEL), lambda c: (jnp.maximum(c * hist_per_chunk - 1, 0), ACT_GLU)),
            act_block(ACT_SGA), act_block(ACT_SGB),
            pl.BlockSpec((N_SAMPLE, D_MODEL), const2),
            pl.BlockSpec((A_GROUPS, CHUNK, CHUNK), lambda c: (0, 0, 0)),
            pl.BlockSpec((CHUNK, D_MODEL), const2),
            vec, vec, vec, vec,
            pl.BlockSpec((CONV_WIDTH + 1, D_MODEL), const2),
            vec, vec, vec,
            hbm, hbm,
        ],
        out_specs=(
            pl.BlockSpec((CHUNK, D_MODEL), row),
            pl.BlockSpec((N_SAMPLE, D_MODEL), const2),
        ),
        scratch_shapes=[pltpu.VMEM((WIN_ROWS, D_MODEL), F32),
                        pltpu.VMEM((CHUNK, D_MODEL), F32),
                        pltpu.VMEM((CHUNK, D_MODEL), BF16), pltpu.VMEM((CHUNK, D_MODEL), BF16),
                        pltpu.VMEM((D_MODEL, D_MODEL), BF16), pltpu.VMEM((D_MODEL, D_MODEL), BF16)] + _WEIGHT_STAGE,
        compiler_params=pltpu.CompilerParams(
            dimension_semantics=("arbitrary",), vmem_limit_bytes=VMEM_LIMIT),
        name="seqmix",
    )(act, act, act, act, act, act, conv_s, wmix, bmix, w00, b00, ln_v_g, ln_v_b, conv_w, conv_b, ln_c_g, ln_c_b,
      wa, wb)


def _route(logits):
    col = lax.broadcasted_iota(jnp.int32, logits.shape, 1).astype(F32)
    neg = jnp.float32(-jnp.inf)
    big = jnp.float32(1e9)
    lg = jnp.where(col < MOE_GROUPS, logits, neg)
    gmax = jnp.max(lg, axis=-1, keepdims=True)
    gi = jnp.min(jnp.where(lg == gmax, col, big), axis=-1, keepdims=True)
    gw = 1.0 / jnp.sum(jnp.exp(lg - gmax), axis=-1, keepdims=True)
    lo = MOE_GROUPS + gi * EXPERTS_PER_GROUP
    le = jnp.where((col >= lo) & (col < lo + EXPERTS_PER_GROUP), logits, neg)
    m1 = jnp.max(le, axis=-1, keepdims=True)
    i1 = jnp.min(jnp.where(le == m1, col, big), axis=-1, keepdims=True)
    le2 = jnp.where(col == i1, neg, le)
    m2 = jnp.max(le2, axis=-1, keepdims=True)
    i2 = jnp.min(jnp.where(le2 == m2, col, big), axis=-1, keepdims=True)
    e = jnp.exp(m2 - m1)
    w1 = gw / (1.0 + e)
    w2 = gw * e / (1.0 + e)
    eid = jnp.where(col == 0, i1 - MOE_GROUPS, jnp.where(col == 1, i2 - MOE_GROUPS, 0.0)).astype(jnp.int32)
    ew = jnp.where(col == 0, w1, jnp.where(col == 1, w2, 0.0))
    return eid, ew


def _mixout_rows(m, x, wo_ref, nf_ref, wr_ref, br_ref, tril, count_ref):
    h = x + _bdot(m, wo_ref[...])
    hn = _rms(h, nf_ref[...])
    logits = _bdot(hn.astype(BF16), wr_ref[...]) + br_ref[...]
    eid, ew = _route(logits)
    col = lax.broadcasted_iota(jnp.int32, eid.shape, 1)
    first = col == eid[:, 0:1]
    second = col == eid[:, 1:2]
    hits = first.astype(F32) + second.astype(F32)
    before = _bdot(tril, hits.astype(BF16)) + count_ref[...]
    rank0 = jnp.sum(jnp.where(first, before, 0.0), axis=-1, keepdims=True)
    rank1 = jnp.sum(jnp.where(second, before + first.astype(F32), 0.0), axis=-1, keepdims=True)
    count_ref[...] = count_ref[...] + jnp.sum(hits, axis=0, keepdims=True)
    ids = jnp.where(col == 2, rank0.astype(jnp.int32), jnp.where(col == 3, rank1.astype(jnp.int32), eid))
    return h, ids, ew


def _mixout_kernel(mp_ref, xp_ref, ms_ref, xs_ref, wo_hbm, nf_ref, wr_ref, br_ref, tril_ref,
                   h_ref, eid_ref, ew_ref, cnt_ref, wo_ref, count_ref, stage_ref, sem):
    i = pl.program_id(0)

    @pl.when(i == 0)
    def _():
        _load_weight_bf16(wo_hbm, wo_ref, stage_ref, sem)
        count_ref[...] = jnp.zeros((1, LANES), F32)

    @pl.when(i < MIX_PROMPT_STEPS)
    def _():
        h, eid, ew = _mixout_rows(mp_ref[...], xp_ref[...], wo_ref, nf_ref, wr_ref, br_ref, tril_ref[...], count_ref)
        h_ref[...] = h
        eid_ref[...] = eid
        ew_ref[...] = ew

    @pl.when(i == MIX_PROMPT_STEPS)
    def _():
        h, eid, ew = _mixout_rows(ms_ref[...], xs_ref[...], wo_ref, nf_ref, wr_ref, br_ref,
                                  tril_ref[0:N_SAMPLE, 0:N_SAMPLE], count_ref)
        for ref, val in ((h_ref, h), (eid_ref, eid), (ew_ref, ew)):
            ref[0:N_SAMPLE, :] = val
            ref[N_SAMPLE:, :] = jnp.zeros((MIX_TM - N_SAMPLE, val.shape[1]), val.dtype)

    cnt_ref[...] = jnp.broadcast_to(count_ref[...], cnt_ref.shape).astype(jnp.int32)


def _mixout(m_all, x_prompt, x_sample, wo, norm_ffn, w_route, b_route):
    last = MIX_PROMPT_STEPS - 1
    row = lambda i: (i, 0)
    prow = lambda i: (jnp.minimum(i, last), 0)
    const2 = lambda i: (0, 0)
    hbm = pl.BlockSpec(memory_space=pltpu.MemorySpace.HBM)
    tile_row = jnp.arange(MIX_TM, dtype=jnp.int32)
    tril = (tile_row[None, :] < tile_row[:, None]).astype(BF16)
    return pl.pallas_call(
        _mixout_kernel,
        out_shape=(
            jax.ShapeDtypeStruct((N_ROWS, D_MODEL), F32),
            jax.ShapeDtypeStruct((N_ROWS, LANES), jnp.int32),
            jax.ShapeDtypeStruct((N_ROWS, LANES), F32),
            jax.ShapeDtypeStruct((SUBLANES, LANES), jnp.int32),
        ),
        grid=(MIX_PROMPT_STEPS + 1,),
        in_specs=[pl.BlockSpec((MIX_TM, D_MODEL), prow), pl.BlockSpec((MIX_TM, D_MODEL), prow),
                  pl.BlockSpec((N_SAMPLE, D_MODEL), lambda i: (N_PROMPT // N_SAMPLE, 0)),
                  pl.BlockSpec((N_SAMPLE, D_MODEL), const2),
                  hbm,
                  pl.BlockSpec((1, D_MODEL), const2),
                  pl.BlockSpec((D_MODEL, LANES), const2),
                  pl.BlockSpec((1, LANES), const2),
                  pl.BlockSpec((MIX_TM, MIX_TM), const2)],
        out_specs=(pl.BlockSpec((MIX_TM, D_MODEL), row), pl.BlockSpec((MIX_TM, LANES), row),
                   pl.BlockSpec((MIX_TM, LANES), row), pl.BlockSpec((SUBLANES, LANES), const2)),
        scratch_shapes=[pltpu.VMEM((D_MODEL, D_MODEL), BF16), pltpu.VMEM((1, LANES), F32)] + _WEIGHT_STAGE,
        compiler_params=pltpu.CompilerParams(
            dimension_semantics=("arbitrary",), vmem_limit_bytes=VMEM_LIMIT),
        name="mixout",
    )(m_all, x_prompt, m_all, x_sample, wo, norm_ffn, w_route, b_route, tril)


def _experts_kernel(item_e, item_blk0, item_nb, src_hbm, dst_hbm, h_hbm, nf_ref, wg_ref, wu_ref, wd_ref,
                    y_hbm, src_s, dst_s, pend_s, xg_ref, yst_ref, xs_ref, hm_ref, wgb_ref, wub_ref, wdb_ref,
                    idx_sem, g_sem, s_sem):
    w = pl.program_id(0)
    s = pl.program_id(1)
    nb = item_nb[w]
    nxt = jnp.minimum(w + 1, MAX_ITEMS - 1)
    nb_next = jnp.where(w + 1 < MAX_ITEMS, item_nb[nxt], 0)

    def table_fetch(tbl_hbm, tbl_s, sem_idx, item):
        slot0 = pl.multiple_of(item_blk0[item] * ROW_BLOCK, ROW_BLOCK)
        return pltpu.make_async_copy(tbl_hbm.at[pl.ds(slot0, ITEM_ROWS)], tbl_s, idx_sem.at[sem_idx])

    def row_copy_in(t, u, blk):
        src = src_s[t * SUBLANES + u]
        return pltpu.make_async_copy(h_hbm.at[src >> 3, pl.ds(src & (SUBLANES - 1), 1)],
                                     xg_ref.at[t, pl.ds(u, 1)], g_sem.at[blk])

    def row_copy_out(t, u, blk):
        dst = dst_s[t * SUBLANES + u]
        return pltpu.make_async_copy(yst_ref.at[t, pl.ds(u, 1)],
                                     y_hbm.at[dst >> 3, pl.ds(dst & (SUBLANES - 1), 1)], s_sem.at[blk])

    def block_rows(blk):
        return pl.ds(pl.multiple_of(blk * ROW_BLOCK, ROW_BLOCK), ROW_BLOCK)

    def block_tiles(blk):
        return pl.ds(pl.multiple_of(blk * BLOCK_TILES, BLOCK_TILES), BLOCK_TILES)

    def block_copy_wait(sem, blk):
        pltpu.make_async_copy(xg_ref.at[block_tiles(blk)], xg_ref.at[block_tiles(blk)], sem.at[blk]).wait()

    def for_each_row(n_blocks, row_copy):
        def tile(t, carry):
            for u in range(SUBLANES):
                row_copy(t, u, lax.shift_right_logical(t, BLOCK_TILES.bit_length() - 1)).start()
            return carry
        lax.fori_loop(0, n_blocks * BLOCK_TILES, tile, 0)

    def issue_gather(n_blocks):
        for_each_row(n_blocks, row_copy_in)

    def drain_scatter():
        def drain(blk, carry):
            block_copy_wait(s_sem, blk)
            return carry
        lax.fori_loop(0, pend_s[0], drain, 0)
        pend_s[0] = 0

    @pl.when((w == 0) & (s == 0))
    def _():
        pad_tiles = pl.ds(0, BLOCK_TILES)
        xg_ref[pad_tiles] = jnp.zeros((BLOCK_TILES, SUBLANES, D_MODEL), F32)
        for row0 in (N_ROWS, DUMP_ROW0):
            cz = pltpu.make_async_copy(xg_ref.at[pad_tiles], y_hbm.at[pl.ds(row0 // SUBLANES, BLOCK_TILES)],
                                       idx_sem.at[0])
            cz.start()
            cz.wait()
        pend_s[0] = 0
        ci = table_fetch(src_hbm, src_s, 0, w)
        ci.start()
        ci.wait()
        issue_gather(nb)

    @pl.when((s == 0) & (nb > 0))
    def _():
        table_fetch(dst_hbm, dst_s, 1, w).start()

        def land(blk, carry):
            block_copy_wait(g_sem, blk)
            x = xg_ref[block_tiles(blk)].reshape(ROW_BLOCK, D_MODEL)
            xs_ref[block_rows(blk), :] = _rms(x, nf_ref[...]).astype(BF16)
            return carry
        lax.fori_loop(0, nb, land, 0)

    @pl.when((s == 1) & (nb_next > 0))
    def _():
        table_fetch(src_hbm, src_s, 0, nxt).start()

    @pl.when((s == 2) & (nb_next > 0))
    def _():
        table_fetch(src_hbm, src_s, 0, nxt).wait()
        issue_gather(nb_next)

    def over_row_blocks(fn):
        def four_blocks(j, carry):
            fn(pl.multiple_of(j * (4 * ROW_BLOCK), 4 * ROW_BLOCK), 4 * ROW_BLOCK)
            return carry
        lax.fori_loop(0, nb >> 2, four_blocks, 0)

        @pl.when((nb & 2) == 2)
        def _():
            fn(pl.multiple_of((nb >> 2) * (4 * ROW_BLOCK), 2 * ROW_BLOCK), 2 * ROW_BLOCK)

        @pl.when((nb & 1) == 1)
        def _():
            fn(pl.multiple_of((nb - 1) * ROW_BLOCK, ROW_BLOCK), ROW_BLOCK)

    @pl.when(nb > 0)
    def _():
        wgb_ref[...] = wg_ref[...].astype(BF16)
        wub_ref[...] = wu_ref[...].astype(BF16)
        wdb_ref[pl.ds(pl.multiple_of(s * EXPERT_TN, EXPERT_TN), EXPERT_TN), :] = wd_ref[...].astype(BF16)

        def gate_up(row0, nrows):
            rows = pl.ds(row0, nrows)
            x = xs_ref[rows, :]
            hm_ref[s, rows, :] = (_silu(_bdot(x, wgb_ref[...])) * _bdot(x, wub_ref[...])).astype(BF16)
        over_row_blocks(gate_up)

    @pl.when((s == EXPERT_SLICES - 1) & (nb > 0))
    def _():
        drain_scatter()

        def down(row0, nrows):
            rows = pl.ds(row0, nrows)
            hm = jnp.concatenate([hm_ref[k, rows, :] for k in range(EXPERT_SLICES)], axis=1)
            tiles = pl.ds(pl.multiple_of(row0 // SUBLANES, BLOCK_TILES), nrows // SUBLANES)
            yst_ref[tiles] = _bdot(hm, wdb_ref[...]).reshape(nrows // SUBLANES, SUBLANES, D_MODEL)
        over_row_blocks(down)

        table_fetch(dst_hbm, dst_s, 1, w).wait()

        for_each_row(nb, row_copy_out)
        pend_s[0] = nb

    @pl.when((w == MAX_ITEMS - 1) & (s == EXPERT_SLICES - 1))
    def _():
        drain_scatter()


def _experts(item_e, item_blk0, item_nb, src_row, dst_row, h_all, norm_ffn, w_gate, w_up, w_down):
    hbm = pl.BlockSpec(memory_space=pltpu.MemorySpace.HBM)

    def w_slice(w, s, inb):
        return jnp.where(inb[w] > 0, s, EXPERT_SLICES - 1)

    grid_spec = pltpu.PrefetchScalarGridSpec(
        num_scalar_prefetch=3,
        grid=(MAX_ITEMS, EXPERT_SLICES),
        in_specs=[
            hbm, hbm, hbm,
            pl.BlockSpec((1, D_MODEL), lambda w, s, ie, ib, inb: (0, 0)),
            pl.BlockSpec((None, D_MODEL, EXPERT_TN), lambda w, s, ie, ib, inb: (ie[w], 0, w_slice(w, s, inb))),
            pl.BlockSpec((None, D_MODEL, EXPERT_TN), lambda w, s, ie, ib, inb: (ie[w], 0, w_slice(w, s, inb))),
            pl.BlockSpec((None, EXPERT_TN, D_MODEL), lambda w, s, ie, ib, inb: (ie[w], w_slice(w, s, inb), 0)),
        ],
        out_specs=hbm,
        scratch_shapes=[
            pltpu.SMEM((ITEM_ROWS,), jnp.int32),
            pltpu.SMEM((ITEM_ROWS,), jnp.int32),
            pltpu.SMEM((1,), jnp.int32),
            pltpu.VMEM((ITEM_ROWS // SUBLANES, SUBLANES, D_MODEL), F32),
            pltpu.VMEM((ITEM_ROWS // SUBLANES, SUBLANES, D_MODEL), F32),
            pltpu.VMEM((ITEM_ROWS, D_MODEL), BF16),
            pltpu.VMEM((EXPERT_SLICES, ITEM_ROWS, EXPERT_TN), BF16),
            pltpu.VMEM((D_MODEL, EXPERT_TN), BF16),
            pltpu.VMEM((D_MODEL, EXPERT_TN), BF16),
            pltpu.VMEM((D_EXPERT, D_MODEL), BF16),
            pltpu.SemaphoreType.DMA((2,)),
            pltpu.SemaphoreType.DMA((ITEM_BLOCKS,)),
            pltpu.SemaphoreType.DMA((ITEM_BLOCKS,)),
        ],
    )
    return pl.pallas_call(
        _experts_kernel,
        out_shape=jax.ShapeDtypeStruct((PAIR_ROWS // SUBLANES, SUBLANES, D_MODEL), F32),
        grid_spec=grid_spec,
        compiler_params=pltpu.CompilerParams(
            dimension_semantics=("arbitrary", "arbitrary"), vmem_limit_bytes=VMEM_LIMIT),
        name="experts",
    )(item_e, item_blk0, item_nb, src_row, dst_row, h_all.reshape(N_ROWS // SUBLANES, SUBLANES, D_MODEL),
      norm_ffn, w_gate, w_up, w_down).reshape(PAIR_ROWS, D_MODEL)


def _dispatch_tables(ids, counts):
    flat_e = ids[:, 0:TOP_K].reshape(-1)
    rank = ids[:, TOP_K:2 * TOP_K].reshape(-1)
    nblk = (counts + ROW_BLOCK - 1) // ROW_BLOCK
    blk0 = jnp.cumsum(nblk) - nblk
    slot = blk0[flat_e] * ROW_BLOCK + rank
    slot_pair = jnp.full((SLOT_TABLE,), N_PAIRS, jnp.int32).at[slot].set(jnp.arange(N_PAIRS, dtype=jnp.int32))
    slot_id = jnp.arange(SLOT_TABLE, dtype=jnp.int32)
    src_row = jnp.minimum(slot_pair // TOP_K, N_ROWS - 1)
    dst_row = jnp.where(slot_pair < N_PAIRS, (slot_pair % TOP_K) * PAIR_REGION + slot_pair // TOP_K,
                        DUMP_ROW0 + slot_id % ROW_BLOCK)
    nitem = (nblk + ITEM_BLOCKS - 1) // ITEM_BLOCKS
    item_end = jnp.cumsum(nitem)
    total = item_end[-1]
    idx = jnp.arange(MAX_ITEMS, dtype=jnp.int32)
    live = idx < total
    e_of = jnp.minimum(jnp.searchsorted(item_end, jnp.minimum(idx, total - 1), side='right'),
                       N_EXPERTS - 1).astype(jnp.int32)
    local = jnp.minimum(idx, total - 1) - (item_end - nitem)[e_of]
    item_blk0 = (blk0[e_of] + local * ITEM_BLOCKS).astype(jnp.int32)
    item_nb = jnp.where(live, jnp.clip(nblk[e_of] - local * ITEM_BLOCKS, 0, ITEM_BLOCKS), 0).astype(jnp.int32)
    return e_of, item_blk0, item_nb, src_row, dst_row


def _final_rows(h, y0, y1, ew, p, wg_ref, wp_ref, np_ref, fn_ref):
    h = h + (ew[:, 0:1] * y0 + ew[:, 1:2] * y1)
    gate = _sigmoid(_bdot(_rms(h, np_ref[...]).astype(BF16), wg_ref[...]))
    h = h + gate * _bdot(p.astype(BF16), wp_ref[...])
    return _rms(h, fn_ref[...])


def _final_kernel(h_ref, y0_ref, y1_ref, ew_ref, p_ref, hs_ref, y0s_ref, y1s_ref, ews_ref, ps_ref,
                  wg_hbm, wp_hbm, np_ref, fn_ref, op_ref, os_ref, wg_ref, wp_ref, stage_ref, sem):
    i = pl.program_id(0)

    @pl.when(i == 0)
    def _():
        _load_weight_bf16(wg_hbm, wg_ref, stage_ref, sem)
        _load_weight_bf16(wp_hbm, wp_ref, stage_ref, sem)

    @pl.when(i < FIN_PROMPT_STEPS)
    def _():
        op_ref[...] = _final_rows(h_ref[...], y0_ref[...], y1_ref[...], ew_ref[...], p_ref[...],
                                  wg_ref, wp_ref, np_ref, fn_ref)

    @pl.when(i == FIN_PROMPT_STEPS)
    def _():
        os_ref[...] = _final_rows(hs_ref[...], y0s_ref[...], y1s_ref[...], ews_ref[...], ps_ref[...],
                                  wg_ref, wp_ref, np_ref, fn_ref)


def _final(h_all, ypairs, ew, p_prompt, p_sample, w_gate, w_proj, norm_ple, final_norm):
    last = FIN_PROMPT_STEPS - 1
    prow = lambda i: (jnp.minimum(i, last), 0)
    const2 = lambda i: (0, 0)
    hbm = pl.BlockSpec(memory_space=pltpu.MemorySpace.HBM)
    sblk = N_PROMPT // N_SAMPLE
    pact = pl.BlockSpec((FIN_TM, D_MODEL), prow)
    sact = lambda off: pl.BlockSpec((N_SAMPLE, D_MODEL), lambda i: (off + sblk, 0))
    return pl.pallas_call(
        _final_kernel,
        out_shape=(jax.ShapeDtypeStruct((N_PROMPT, D_MODEL), F32), jax.ShapeDtypeStruct((N_SAMPLE, D_MODEL), F32)),
        grid=(FIN_PROMPT_STEPS + 1,),
        in_specs=[pact, pact,
                  pl.BlockSpec((FIN_TM, D_MODEL), lambda i: (jnp.minimum(i, last) + PAIR_REGION // FIN_TM, 0)),
                  pl.BlockSpec((FIN_TM, LANES), prow),
                  pl.BlockSpec((FIN_TM, PLE_DIM), prow),
                  sact(0), sact(0), sact(PAIR_REGION // N_SAMPLE),
                  pl.BlockSpec((N_SAMPLE, LANES), lambda i: (sblk, 0)),
                  pl.BlockSpec((N_SAMPLE, PLE_DIM), const2),
                  hbm, hbm,
                  pl.BlockSpec((1, D_MODEL), const2),
                  pl.BlockSpec((1, D_MODEL), const2)],
        out_specs=(pact, pl.BlockSpec((N_SAMPLE, D_MODEL), const2)),
        scratch_shapes=[pltpu.VMEM((D_MODEL, D_MODEL), BF16), pltpu.VMEM((PLE_DIM, D_MODEL), BF16)] + _WEIGHT_STAGE,
        compiler_params=pltpu.CompilerParams(
            dimension_semantics=("arbitrary",), vmem_limit_bytes=VMEM_LIMIT),
        name="final",
    )(h_all, ypairs, ypairs, ew, p_prompt, h_all, ypairs, ypairs, ew, p_sample, w_gate, w_proj, norm_ple, final_norm)


def kernel(x_prompt, x_sample, state_conv, p_prompt, p_sample, norm_mix, w_in, ln_v_g, ln_v_b, w_spatial,
           b_spatial, w_proj_a, conv_w, conv_b, ln_c_g, ln_c_b, w_proj_b, w_out, norm_ffn, w_router_group,
           b_router_group, w_router_expert, b_router_expert, w_exp_gate, w_exp_up, w_exp_down, norm_ple,
           w_ple_gate, w_ple_proj, final_norm):
    assert w_in.shape[0] == 1, "single layer"
    vec = lambda v: v.reshape(1, -1)
    xp = x_prompt.reshape(N_PROMPT, D_MODEL)
    xs = x_sample.reshape(N_SAMPLE, D_MODEL)
    xn = _xnorm(xp, xs, vec(norm_mix[0]))

    w_s, b_s = w_spatial[0], b_spatial[0]
    wmix = jnp.tril(w_s).astype(BF16)
    bmix = jnp.repeat(b_s.T, A_GROUP_DIM, axis=1)
    w00 = jnp.repeat(w_s[:, 0, 0], A_GROUP_DIM).reshape(1, D_MODEL)
    b00 = jnp.repeat(b_s[:, 0], A_GROUP_DIM).reshape(1, D_MODEL)

    act = _in_proj(xn, w_in[0])

    cw = conv_w[0]
    new_state, conv_s = _sconv(jnp.transpose(state_conv[0], (1, 0, 2)), act, cw, vec(conv_b[0]))
    new_conv_sample = jnp.transpose(new_state, (1, 0, 2))[None]
    cw_pad = jnp.concatenate([cw, jnp.zeros((1, D_MODEL), F32)], axis=0)
    m_all, vn_s = _seqmix(act, conv_s, wmix, bmix, w00, b00, vec(ln_v_g[0]), vec(ln_v_b[0]), cw_pad,
                          vec(conv_b[0]), vec(ln_c_g[0]), vec(ln_c_b[0]), w_proj_a[0], w_proj_b[0])

    w_route = jnp.zeros((D_MODEL, LANES), F32)
    w_route = w_route.at[:, :MOE_GROUPS].set(w_router_group[0])
    w_route = w_route.at[:, MOE_GROUPS:MOE_GROUPS + N_EXPERTS].set(w_router_expert[0])
    b_route = jnp.zeros((1, LANES), F32)
    b_route = b_route.at[0, :MOE_GROUPS].set(b_router_group[0])
    b_route = b_route.at[0, MOE_GROUPS:MOE_GROUPS + N_EXPERTS].set(b_router_expert[0])

    h_all, ids, ew, counts = _mixout(m_all, xp, xs, w_out[0], vec(norm_ffn[0]), w_route.astype(BF16), b_route)

    tables = _dispatch_tables(ids[:, :2 * TOP_K], counts[0, :N_EXPERTS])
    ypairs = _experts(*tables, h_all, vec(norm_ffn[0]), w_exp_gate[0], w_exp_up[0], w_exp_down[0])

    y_prompt, y_sample = _final(h_all, ypairs, ew, p_prompt[0].reshape(N_PROMPT, PLE_DIM),
                                p_sample[0].reshape(N_SAMPLE, PLE_DIM), w_ple_gate[0], w_ple_proj[0],
                                vec(norm_ple[0]), vec(final_norm))
    y_prompt = y_prompt.reshape(N_PROMPT_SEQ, SEQ, D_MODEL)
    y_sample = y_sample.reshape(N_SAMPLE, 1, D_MODEL)
    glu_cols = slice(ACT_GLU * D_MODEL, (ACT_GLU + 1) * D_MODEL)
    new_conv_prompt = jnp.stack([act[(b + 1) * SEQ - CONV_HIST:(b + 1) * SEQ, glu_cols]
                                 for b in range(N_PROMPT_SEQ)])[None]
    return (y_prompt, y_sample, new_conv_prompt, new_conv_sample, vn_s.reshape(1, N_SAMPLE, 1, D_MODEL))
```

```python
import jax
import jax.numpy as jnp
from jax import lax
from jax.experimental import pallas as pl
from jax.experimental.pallas import tpu as pltpu

F32 = jnp.float32
BF16 = jnp.bfloat16

D_MODEL = 2048
N_PROMPT_SEQ = 4
SEQ = 2048
N_PROMPT = N_PROMPT_SEQ * SEQ
N_SAMPLE = 128
N_ROWS = N_PROMPT + N_SAMPLE
CHUNK = 128
N_CHUNKS = N_ROWS // CHUNK
CHUNKS_PER_SEQ = SEQ // CHUNK
A_GROUPS = 8
A_GROUP_DIM = D_MODEL // A_GROUPS
CONV_WIDTH = 31
CONV_HIST = CONV_WIDTH - 1
HIST_BLOCK = 32
MOE_GROUPS = 4
EXPERTS_PER_GROUP = 8
N_EXPERTS = MOE_GROUPS * EXPERTS_PER_GROUP
TOP_K = 2
D_EXPERT = D_MODEL // 2
PLE_DIM = 256
EPS = 1e-6
LANES = 128

XN_TM = 512
IN_TM = 1664
IN_TN = 512
IN_COL_TILES = D_MODEL // IN_TN
ACT_BLOCKS = 5
ACT_GU, ACT_GV, ACT_GLU, ACT_SGA, ACT_SGB = range(ACT_BLOCKS)
MIX_TM = 512
MIX_PROMPT_STEPS = N_PROMPT // MIX_TM
FIN_TM = 256
FIN_PROMPT_STEPS = N_PROMPT // FIN_TM
ROW_BLOCK = 128
SUBLANES = 8
BLOCK_TILES = ROW_BLOCK // SUBLANES
N_PAIRS = N_ROWS * TOP_K
MAX_BLOCKS = N_PAIRS // ROW_BLOCK + N_EXPERTS
ITEM_BLOCKS = 8
ITEM_ROWS = ITEM_BLOCKS * ROW_BLOCK
MAX_ITEMS = (MAX_BLOCKS + (ITEM_BLOCKS - 1) * N_EXPERTS) // ITEM_BLOCKS
EXPERT_BLOCKS = N_ROWS // ROW_BLOCK + ITEM_BLOCKS
EXPERT_SLOTS = EXPERT_BLOCKS * ROW_BLOCK
SLOT_TABLE = N_EXPERTS * EXPERT_SLOTS
EXPERT_SLICES = 4
EXPERT_TN = D_EXPERT // EXPERT_SLICES
PAIR_REGION = 8448
DUMP_ROW0 = PAIR_REGION + N_ROWS
PAIR_ROWS = DUMP_ROW0 + ROW_BLOCK

VMEM_LIMIT = 56 * 1024 * 1024


def _rms(x, g):
    return x * lax.rsqrt(jnp.mean(x * x, axis=-1, keepdims=True) + EPS) * g


def _layer_norm(x, g, b):
    mu = jnp.mean(x, axis=-1, keepdims=True)
    xc = x - mu
    return xc * lax.rsqrt(jnp.mean(xc * xc, axis=-1, keepdims=True) + EPS) * g + b


def _sigmoid(x):
    return 1.0 / (1.0 + jnp.exp(-x))


def _silu(x):
    return x * _sigmoid(x)


def _gelu(x):
    return jax.nn.gelu(x, approximate=True)


def _bdot(a, b):
    return jnp.dot(a, b, preferred_element_type=F32)


STAGE_ROWS = 256


def _load_weight_bf16(w_hbm, dst_ref, stage_ref, sem):
    n_chunks = w_hbm.shape[0] // STAGE_ROWS

    def chunk_copy(j):
        return pltpu.make_async_copy(w_hbm.at[pl.ds(j * STAGE_ROWS, STAGE_ROWS)], stage_ref.at[j % 2], sem.at[j % 2])

    chunk_copy(0).start()
    for j in range(n_chunks):
        if j + 1 < n_chunks:
            chunk_copy(j + 1).start()
        chunk_copy(j).wait()
        dst_ref[pl.ds(j * STAGE_ROWS, STAGE_ROWS), :] = stage_ref[j % 2].astype(BF16)


_WEIGHT_STAGE = [pltpu.VMEM((2, STAGE_ROWS, D_MODEL), F32), pltpu.SemaphoreType.DMA((2,))]


def _xnorm_kernel(xp_ref, xs_ref, nm_ref, xn_ref):
    i = pl.program_id(0)

    @pl.when(i < N_PROMPT // XN_TM)
    def _():
        xn_ref[...] = _rms(xp_ref[...], nm_ref[...]).astype(BF16)

    @pl.when(i == N_PROMPT // XN_TM)
    def _():
        xn_ref[0:N_SAMPLE, :] = _rms(xs_ref[...], nm_ref[...]).astype(BF16)
        xn_ref[N_SAMPLE:, :] = jnp.zeros((XN_TM - N_SAMPLE, D_MODEL), BF16)


def _xnorm(x_prompt, x_sample, norm_mix):
    last = N_PROMPT // XN_TM - 1
    return pl.pallas_call(
        _xnorm_kernel,
        out_shape=jax.ShapeDtypeStruct((N_ROWS, D_MODEL), BF16),
        grid=(N_PROMPT // XN_TM + 1,),
        in_specs=[pl.BlockSpec((XN_TM, D_MODEL), lambda i: (jnp.minimum(i, last), 0)),
                  pl.BlockSpec((N_SAMPLE, D_MODEL), lambda i: (0, 0)),
                  pl.BlockSpec((1, D_MODEL), lambda i: (0, 0))],
        out_specs=pl.BlockSpec((XN_TM, D_MODEL), lambda i: (i, 0)),
        name="xnorm",
    )(x_prompt, x_sample, norm_mix)


def _in_w_col(j):
    t = IN_COL_TILES
    jj = j - 2 * t
    glu_col = 2 * t + (jj % 2) * t + jj // 2
    return jnp.where((j >= 2 * t) & (j < 4 * t), glu_col, j)


def _in_out_col(j):
    t = IN_COL_TILES
    return jnp.where(j < 2 * t, j, jnp.where(j < 4 * t, 2 * t + (j - 2 * t) // 2, j - t))


def _in_proj_kernel(xn_ref, w_ref, o_ref, val_ref):
    j = pl.program_id(1)
    t = IN_COL_TILES

    def z():
        return _bdot(xn_ref[...], w_ref[...].astype(BF16))

    @pl.when(j < 2 * t)
    def _():
        o_ref[...] = _gelu(z())

    @pl.when((j >= 2 * t) & (j < 4 * t) & (j % 2 == 0))
    def _():
        val_ref[...] = z()

    @pl.when((j >= 2 * t) & (j < 4 * t) & (j % 2 == 1))
    def _():
        o_ref[...] = val_ref[...] * _sigmoid(z())

    @pl.when(j >= 4 * t)
    def _():
        o_ref[...] = _sigmoid(z())


def _in_proj(xn, w_in):
    return pl.pallas_call(
        _in_proj_kernel,
        out_shape=jax.ShapeDtypeStruct((N_ROWS, ACT_BLOCKS * D_MODEL), F32),
        grid=(N_ROWS // IN_TM, 6 * IN_COL_TILES),
        in_specs=[
            pl.BlockSpec((IN_TM, D_MODEL), lambda i, j: (i, 0)),
            pl.BlockSpec((D_MODEL, IN_TN), lambda i, j: (0, _in_w_col(j))),
        ],
        out_specs=pl.BlockSpec((IN_TM, IN_TN), lambda i, j: (i, _in_out_col(j))),
        scratch_shapes=[pltpu.VMEM((IN_TM, IN_TN), F32)],
        compiler_params=pltpu.CompilerParams(
            dimension_semantics=("arbitrary", "arbitrary"), vmem_limit_bytes=VMEM_LIMIT),
        name="in_proj",
    )(xn, w_in)


SCONV_B = 16


def _sconv_kernel(st_ref, glu_ref, w_ref, wl_ref, cb_ref, new_ref, conv_ref):
    glu = glu_ref[...]
    new_ref[0:CONV_HIST - 1] = st_ref[1:CONV_HIST]
    new_ref[CONV_HIST - 1] = glu
    acc = st_ref[0] * w_ref[0:1, :]
    for k in range(1, CONV_HIST):
        acc = acc + st_ref[k] * w_ref[k:k + 1, :]
    conv_ref[...] = acc + glu * wl_ref[...] + cb_ref[...]


def _sconv(state, act, conv_w, conv_b):
    return pl.pallas_call(
        _sconv_kernel,
        out_shape=(
            jax.ShapeDtypeStruct((CONV_HIST, N_SAMPLE, D_MODEL), F32),
            jax.ShapeDtypeStruct((N_SAMPLE, D_MODEL), F32),
        ),
        grid=(N_SAMPLE // SCONV_B,),
        in_specs=[
            pl.BlockSpec((CONV_HIST, SCONV_B, D_MODEL), lambda i: (0, i, 0)),
            pl.BlockSpec((SCONV_B, D_MODEL), lambda i: (N_PROMPT // SCONV_B + i, ACT_GLU)),
            pl.BlockSpec((CONV_HIST, D_MODEL), lambda i: (0, 0)),
            pl.BlockSpec((1, D_MODEL), lambda i: (0, 0)),
            pl.BlockSpec((1, D_MODEL), lambda i: (0, 0)),
        ],
        out_specs=(
            pl.BlockSpec((CONV_HIST, SCONV_B, D_MODEL), lambda i: (0, i, 0)),
            pl.BlockSpec((SCONV_B, D_MODEL), lambda i: (i, 0)),
        ),
        name="sconv",
    )(state, act, conv_w[:CONV_HIST], conv_w[CONV_HIST:], conv_b)


CONV_STRIP = 256
WIN_ROWS = HIST_BLOCK + CHUNK


def _seqmix_kernel(gv_ref, gu_ref, cur_ref, prev_ref, sga_ref, sgb_ref, convs_ref, wmix_ref, bmix_ref, w00_ref,
                   b00_ref, lvg_ref, lvb_ref, cw_ref, cb_ref, lcg_ref, lcb_ref, wa_hbm, wb_hbm,
                   m_ref, vns_ref, win_ref, conv_ref, a_ref, b_ref, wa_ref, wb_ref, stage_ref, sem):
    c = pl.program_id(0)
    is_sample = c == N_CHUNKS - 1

    @pl.when(c == 0)
    def _():
        _load_weight_bf16(wa_hbm, wa_ref, stage_ref, sem)
        _load_weight_bf16(wb_hbm, wb_ref, stage_ref, sem)

    fresh = (c % CHUNKS_PER_SEQ) == 0
    win_ref[0:HIST_BLOCK, :] = jnp.where(fresh, 0.0, prev_ref[...]).astype(BF16).astype(F32)
    win_ref[HIST_BLOCK:WIN_ROWS, :] = cur_ref[...].astype(BF16).astype(F32)
    off = HIST_BLOCK - CONV_HIST

    vn = _layer_norm(gv_ref[...], lvg_ref[...], lvb_ref[...])
    vns_ref[...] = vn

    vnb = vn.astype(BF16)
    for g in range(A_GROUPS):
        sl = slice(g * A_GROUP_DIM, (g + 1) * A_GROUP_DIM)
        s_chunk = _bdot(wmix_ref[g], vnb[:, sl]) + bmix_ref[:, sl]
        s_first = vn[:, sl] * w00_ref[:, sl] + b00_ref[:, sl]
        a_ref[:, sl] = (gu_ref[:, sl] * jnp.where(is_sample, s_first, s_chunk)).astype(BF16)
    pa = _bdot(a_ref[...], wa_ref[...])

    def phase_sum(r, cs):
        rows = CHUNK + (SUBLANES if r else 0)
        total = None
        for k in range(CONV_WIDTH):
            q, rk = divmod(off + k, SUBLANES)
            if rk == r:
                term = win_ref[SUBLANES * q:SUBLANES * q + rows, cs] * cw_ref[k:k + 1, cs].astype(BF16).astype(F32)
                total = term if total is None else total + term
        return total[r:r + CHUNK, :]

    for s0 in range(0, D_MODEL, CONV_STRIP):
        cs = slice(s0, s0 + CONV_STRIP)
        acc = phase_sum(0, cs)
        for r in range(1, SUBLANES):
            acc = acc + phase_sum(r, cs)
        conv_ref[:, cs] = acc + cb_ref[:, cs]
    conv = jnp.where(is_sample, convs_ref[...], conv_ref[...])
    b_ref[...] = _silu(_layer_norm(conv, lcg_ref[...], lcb_ref[...])).astype(BF16)

    pb = _bdot(b_ref[...], wb_ref[...])
    m_ref[...] = (sga_ref[...] * pa + sgb_ref[...] * pb).astype(BF16)


def _seqmix(act, conv_s, wmix, bmix, w00, b00, ln_v_g, ln_v_b, conv_w, conv_b, ln_c_g, ln_c_b, wa, wb):
    row = lambda c: (c, 0)
    const2 = lambda c: (0, 0)
    vec = pl.BlockSpec((1, D_MODEL), const2)
    hbm = pl.BlockSpec(memory_space=pltpu.MemorySpace.HBM)
    hist_per_chunk = CHUNK // HIST_BLOCK
    act_block = lambda blk: pl.BlockSpec((CHUNK, D_MODEL), lambda c: (c, blk))
    return pl.pallas_call(
        _seqmix_kernel,
        out_shape=(
            jax.ShapeDtypeStruct((N_ROWS, D_MODEL), BF16),
            jax.ShapeDtypeStruct((N_SAMPLE, D_MODEL), F32),
        ),
        grid=(N_CHUNKS,),
        in_specs=[
            act_block(ACT_GV), act_block(ACT_GU), act_block(ACT_GLU),
            pl.BlockSpec((HIST_BLOCK, D_MODEL), lambda c: (jnp.maximum(c * hist_per_chunk - 1, 0), ACT_GLU)),
            act_block(ACT_SGA), act_block(ACT_SGB),
            pl.BlockSpec((N_SAMPLE, D_MODEL), const2),
            pl.BlockSpec((A_GROUPS, CHUNK, CHUNK), lambda c: (0, 0, 0)),
            pl.BlockSpec((CHUNK, D_MODEL), const2),
            vec, vec, vec, vec,
            pl.BlockSpec((CONV_WIDTH + 1, D_MODEL), const2),
            vec, vec, vec,
            hbm, hbm,
        ],
        out_specs=(
            pl.BlockSpec((CHUNK, D_MODEL), row),
            pl.BlockSpec((N_SAMPLE, D_MODEL), const2),
        ),
        scratch_shapes=[pltpu.VMEM((WIN_ROWS, D_MODEL), F32),
                        pltpu.VMEM((CHUNK, D_MODEL), F32),
                        pltpu.VMEM((CHUNK, D_MODEL), BF16), pltpu.VMEM((CHUNK, D_MODEL), BF16),
                        pltpu.VMEM((D_MODEL, D_MODEL), BF16), pltpu.VMEM((D_MODEL, D_MODEL), BF16)] + _WEIGHT_STAGE,
        compiler_params=pltpu.CompilerParams(
            dimension_semantics=("arbitrary",), vmem_limit_bytes=VMEM_LIMIT),
        name="seqmix",
    )(act, act, act, act, act, act, conv_s, wmix, bmix, w00, b00, ln_v_g, ln_v_b, conv_w, conv_b, ln_c_g, ln_c_b,
      wa, wb)


def _route(logits):
    col = lax.broadcasted_iota(jnp.int32, logits.shape, 1).astype(F32)
    neg = jnp.float32(-jnp.inf)
    big = jnp.float32(1e9)
    lg = jnp.where(col < MOE_GROUPS, logits, neg)
    gmax = jnp.max(lg, axis=-1, keepdims=True)
    gi = jnp.min(jnp.where(lg == gmax, col, big), axis=-1, keepdims=True)
    gw = 1.0 / jnp.sum(jnp.exp(lg - gmax), axis=-1, keepdims=True)
    lo = MOE_GROUPS + gi * EXPERTS_PER_GROUP
    le = jnp.where((col >= lo) & (col < lo + EXPERTS_PER_GROUP), logits, neg)
    m1 = jnp.max(le, axis=-1, keepdims=True)
    i1 = jnp.min(jnp.where(le == m1, col, big), axis=-1, keepdims=True)
    le2 = jnp.where(col == i1, neg, le)
    m2 = jnp.max(le2, axis=-1, keepdims=True)
    i2 = jnp.min(jnp.where(le2 == m2, col, big), axis=-1, keepdims=True)
    e = jnp.exp(m2 - m1)
    w1 = gw / (1.0 + e)
    w2 = gw * e / (1.0 + e)
    eid = jnp.where(col == 0, i1 - MOE_GROUPS, jnp.where(col == 1, i2 - MOE_GROUPS, 0.0)).astype(jnp.int32)
    ew = jnp.where(col == 0, w1, jnp.where(col == 1, w2, 0.0))
    return eid, ew


def _mixout_rows(m, x, wo_ref, nf_ref, wr_ref, br_ref, tril, count_ref):
    h = x + _bdot(m, wo_ref[...])
    hn = _rms(h, nf_ref[...])
    logits = _bdot(hn.astype(BF16), wr_ref[...]) + br_ref[...]
    eid, ew = _route(logits)
    col = lax.broadcasted_iota(jnp.int32, eid.shape, 1)
    first = col == eid[:, 0:1]
    second = col == eid[:, 1:2]
    hits = first.astype(F32) + second.astype(F32)
    before = _bdot(tril, hits.astype(BF16)) + count_ref[...]
    rank0 = jnp.sum(jnp.where(first, before, 0.0), axis=-1, keepdims=True)
    rank1 = jnp.sum(jnp.where(second, before + first.astype(F32), 0.0), axis=-1, keepdims=True)
    count_ref[...] = count_ref[...] + jnp.sum(hits, axis=0, keepdims=True)
    ids = jnp.where(col == 2, rank0.astype(jnp.int32), jnp.where(col == 3, rank1.astype(jnp.int32), eid))
    return h, ids, ew


def _mixout_kernel(mp_ref, xp_ref, ms_ref, xs_ref, wo_hbm, nf_ref, wr_ref, br_ref, tril_ref,
                   h_ref, eid_ref, ew_ref, cnt_ref, wo_ref, count_ref, stage_ref, sem):
    i = pl.program_id(0)

    @pl.when(i == 0)
    def _():
        _load_weight_bf16(wo_hbm, wo_ref, stage_ref, sem)
        count_ref[...] = jnp.zeros((1, LANES), F32)

    @pl.when(i < MIX_PROMPT_STEPS)
    def _():
        h, eid, ew = _mixout_rows(mp_ref[...], xp_ref[...], wo_ref, nf_ref, wr_ref, br_ref, tril_ref[...], count_ref)
        h_ref[...] = h
        eid_ref[...] = eid
        ew_ref[...] = ew

    @pl.when(i == MIX_PROMPT_STEPS)
    def _():
        h, eid, ew = _mixout_rows(ms_ref[...], xs_ref[...], wo_ref, nf_ref, wr_ref, br_ref,
                                  tril_ref[0:N_SAMPLE, 0:N_SAMPLE], count_ref)
        for ref, val in ((h_ref, h), (eid_ref, eid), (ew_ref, ew)):
            ref[0:N_SAMPLE, :] = val
            ref[N_SAMPLE:, :] = jnp.zeros((MIX_TM - N_SAMPLE, val.shape[1]), val.dtype)

    cnt_ref[...] = jnp.broadcast_to(count_ref[...], cnt_ref.shape).astype(jnp.int32)


def _mixout(m_all, x_prompt, x_sample, wo, norm_ffn, w_route, b_route):
    last = MIX_PROMPT_STEPS - 1
    row = lambda i: (i, 0)
    prow = lambda i: (jnp.minimum(i, last), 0)
    const2 = lambda i: (0, 0)
    hbm = pl.BlockSpec(memory_space=pltpu.MemorySpace.HBM)
    tile_row = jnp.arange(MIX_TM, dtype=jnp.int32)
    tril = (tile_row[None, :] < tile_row[:, None]).astype(BF16)
    return pl.pallas_call(
        _mixout_kernel,
        out_shape=(
            jax.ShapeDtypeStruct((N_ROWS, D_MODEL), F32),
            jax.ShapeDtypeStruct((N_ROWS, LANES), jnp.int32),
            jax.ShapeDtypeStruct((N_ROWS, LANES), F32),
            jax.ShapeDtypeStruct((SUBLANES, LANES), jnp.int32),
        ),
        grid=(MIX_PROMPT_STEPS + 1,),
        in_specs=[pl.BlockSpec((MIX_TM, D_MODEL), prow), pl.BlockSpec((MIX_TM, D_MODEL), prow),
                  pl.BlockSpec((N_SAMPLE, D_MODEL), lambda i: (N_PROMPT // N_SAMPLE, 0)),
                  pl.BlockSpec((N_SAMPLE, D_MODEL), const2),
                  hbm,
                  pl.BlockSpec((1, D_MODEL), const2),
                  pl.BlockSpec((D_MODEL, LANES), const2),
                  pl.BlockSpec((1, LANES), const2),
                  pl.BlockSpec((MIX_TM, MIX_TM), const2)],
        out_specs=(pl.BlockSpec((MIX_TM, D_MODEL), row), pl.BlockSpec((MIX_TM, LANES), row),
                   pl.BlockSpec((MIX_TM, LANES), row), pl.BlockSpec((SUBLANES, LANES), const2)),
        scratch_shapes=[pltpu.VMEM((D_MODEL, D_MODEL), BF16), pltpu.VMEM((1, LANES), F32)] + _WEIGHT_STAGE,
        compiler_params=pltpu.CompilerParams(
            dimension_semantics=("arbitrary",), vmem_limit_bytes=VMEM_LIMIT),
        name="mixout",
    )(m_all, x_prompt, m_all, x_sample, wo, norm_ffn, w_route, b_route, tril)


def _experts_kernel(item_e, item_blk0, item_nb, src_hbm, dst_hbm, h_hbm, nf_ref, wg_ref, wu_ref, wd_ref,
                    y_hbm, src_s, dst_s, pend_s, xg_ref, yst_ref, xs_ref, hm_ref, wgb_ref, wub_ref, wdb_ref,
                    idx_sem, g_sem, s_sem):
    w = pl.program_id(0)
    s = pl.program_id(1)
    nb = item_nb[w]
    nxt = jnp.minimum(w + 1, MAX_ITEMS - 1)
    nb_next = jnp.where(w + 1 < MAX_ITEMS, item_nb[nxt], 0)

    def table_fetch(tbl_hbm, tbl_s, sem_idx, item):
        slot0 = pl.multiple_of(item_blk0[item] * ROW_BLOCK, ROW_BLOCK)
        return pltpu.make_async_copy(tbl_hbm.at[pl.ds(slot0, ITEM_ROWS)], tbl_s, idx_sem.at[sem_idx])

    def row_copy_in(t, u, blk):
        src = src_s[t * SUBLANES + u]
        return pltpu.make_async_copy(h_hbm.at[src >> 3, pl.ds(src & (SUBLANES - 1), 1)],
                                     xg_ref.at[t, pl.ds(u, 1)], g_sem.at[blk])

    def row_copy_out(t, u, blk):
        dst = dst_s[t * SUBLANES + u]
        return pltpu.make_async_copy(yst_ref.at[t, pl.ds(u, 1)],
                                     y_hbm.at[dst >> 3, pl.ds(dst & (SUBLANES - 1), 1)], s_sem.at[blk])

    def block_rows(blk):
        return pl.ds(pl.multiple_of(blk * ROW_BLOCK, ROW_BLOCK), ROW_BLOCK)

    def block_tiles(blk):
        return pl.ds(pl.multiple_of(blk * BLOCK_TILES, BLOCK_TILES), BLOCK_TILES)

    def block_copy_wait(sem, blk):
        pltpu.make_async_copy(xg_ref.at[block_tiles(blk)], xg_ref.at[block_tiles(blk)], sem.at[blk]).wait()

    def for_each_row(n_blocks, row_copy):
        def tile(t, carry):
            for u in range(SUBLANES):
                row_copy(t, u, lax.shift_right_logical(t, BLOCK_TILES.bit_length() - 1)).start()
            return carry
        lax.fori_loop(0, n_blocks * BLOCK_TILES, tile, 0)

    def issue_gather(n_blocks):
        for_each_row(n_blocks, row_copy_in)

    def drain_scatter():
        def drain(blk, carry):
            block_copy_wait(s_sem, blk)
            return carry
        lax.fori_loop(0, pend_s[0], drain, 0)
        pend_s[0] = 0

    @pl.when((w == 0) & (s == 0))
    def _():
        pad_tiles = pl.ds(0, BLOCK_TILES)
        xg_ref[pad_tiles] = jnp.zeros((BLOCK_TILES, SUBLANES, D_MODEL), F32)
        for row0 in (N_ROWS, DUMP_ROW0):
            cz = pltpu.make_async_copy(xg_ref.at[pad_tiles], y_hbm.at[pl.ds(row0 // SUBLANES, BLOCK_TILES)],
                                       idx_sem.at[0])
            cz.start()
            cz.wait()
        pend_s[0] = 0
        ci = table_fetch(src_hbm, src_s, 0, w)
        ci.start()
        ci.wait()
        issue_gather(nb)

    @pl.when((s == 0) & (nb > 0))
    def _():
        table_fetch(dst_hbm, dst_s, 1, w).start()

        def land(blk, carry):
            block_copy_wait(g_sem, blk)
            x = xg_ref[block_tiles(blk)].reshape(ROW_BLOCK, D_MODEL)
            xs_ref[block_rows(blk), :] = _rms(x, nf_ref[...]).astype(BF16)
            return carry
        lax.fori_loop(0, nb, land, 0)

    @pl.when((s == 1) & (nb_next > 0))
    def _():
        table_fetch(src_hbm, src_s, 0, nxt).start()

    @pl.when((s == 2) & (nb_next > 0))
    def _():
        table_fetch(src_hbm, src_s, 0, nxt).wait()
        issue_gather(nb_next)

    def over_row_blocks(fn):
        def four_blocks(j, carry):
            fn(pl.multiple_of(j * (4 * ROW_BLOCK), 4 * ROW_BLOCK), 4 * ROW_BLOCK)
            return carry
        lax.fori_loop(0, nb >> 2, four_blocks, 0)

        @pl.when((nb & 2) == 2)
        def _():
            fn(pl.multiple_of((nb >> 2) * (4 * ROW_BLOCK), 2 * ROW_BLOCK), 2 * ROW_BLOCK)

        @pl.when((nb & 1) == 1)
        def _():
            fn(pl.multiple_of((nb - 1) * ROW_BLOCK, ROW_BLOCK), ROW_BLOCK)

    @pl.when(nb > 0)
    def _():
        wgb_ref[...] = wg_ref[...].astype(BF16)
        wub_ref[...] = wu_ref[...].astype(BF16)
        wdb_ref[pl.ds(pl.multiple_of(s * EXPERT_TN, EXPERT_TN), EXPERT_TN), :] = wd_ref[...].astype(BF16)

        def gate_up(row0, nrows):
            rows = pl.ds(row0, nrows)
            x = xs_ref[rows, :]
            hm_ref[s, rows, :] = (_silu(_bdot(x, wgb_ref[...])) * _bdot(x, wub_ref[...])).astype(BF16)
        over_row_blocks(gate_up)

    @pl.when((s == EXPERT_SLICES - 1) & (nb > 0))
    def _():
        drain_scatter()

        def down(row0, nrows):
            rows = pl.ds(row0, nrows)
            hm = jnp.concatenate([hm_ref[k, rows, :] for k in range(EXPERT_SLICES)], axis=1)
            tiles = pl.ds(pl.multiple_of(row0 // SUBLANES, BLOCK_TILES), nrows // SUBLANES)
            yst_ref[tiles] = _bdot(hm, wdb_ref[...]).reshape(nrows // SUBLANES, SUBLANES, D_MODEL)
        over_row_blocks(down)

        table_fetch(dst_hbm, dst_s, 1, w).wait()

        for_each_row(nb, row_copy_out)
        pend_s[0] = nb

    @pl.when((w == MAX_ITEMS - 1) & (s == EXPERT_SLICES - 1))
    def _():
        drain_scatter()


def _experts(item_e, item_blk0, item_nb, src_row, dst_row, h_all, norm_ffn, w_gate, w_up, w_down):
    hbm = pl.BlockSpec(memory_space=pltpu.MemorySpace.HBM)

    def w_slice(w, s, inb):
        return jnp.where(inb[w] > 0, s, EXPERT_SLICES - 1)

    grid_spec = pltpu.PrefetchScalarGridSpec(
        num_scalar_prefetch=3,
        grid=(MAX_ITEMS, EXPERT_SLICES),
        in_specs=[
            hbm, hbm, hbm,
            pl.BlockSpec((1, D_MODEL), lambda w, s, ie, ib, inb: (0, 0)),
            pl.BlockSpec((None, D_MODEL, EXPERT_TN), lambda w, s, ie, ib, inb: (ie[w], 0, w_slice(w, s, inb))),
            pl.BlockSpec((None, D_MODEL, EXPERT_TN), lambda w, s, ie, ib, inb: (ie[w], 0, w_slice(w, s, inb))),
            pl.BlockSpec((None, EXPERT_TN, D_MODEL), lambda w, s, ie, ib, inb: (ie[w], w_slice(w, s, inb), 0)),
        ],
        out_specs=hbm,
        scratch_shapes=[
            pltpu.SMEM((ITEM_ROWS,), jnp.int32),
            pltpu.SMEM((ITEM_ROWS,), jnp.int32),
            pltpu.SMEM((1,), jnp.int32),
            pltpu.VMEM((ITEM_ROWS // SUBLANES, SUBLANES, D_MODEL), F32),
            pltpu.VMEM((ITEM_ROWS // SUBLANES, SUBLANES, D_MODEL), F32),
            pltpu.VMEM((ITEM_ROWS, D_MODEL), BF16),
            pltpu.VMEM((EXPERT_SLICES, ITEM_ROWS, EXPERT_TN), BF16),
            pltpu.VMEM((D_MODEL, EXPERT_TN), BF16),
            pltpu.VMEM((D_MODEL, EXPERT_TN), BF16),
            pltpu.VMEM((D_EXPERT, D_MODEL), BF16),
            pltpu.SemaphoreType.DMA((2,)),
            pltpu.SemaphoreType.DMA((ITEM_BLOCKS,)),
            pltpu.SemaphoreType.DMA((ITEM_BLOCKS,)),
        ],
    )
    return pl.pallas_call(
        _experts_kernel,
        out_shape=jax.ShapeDtypeStruct((PAIR_ROWS // SUBLANES, SUBLANES, D_MODEL), F32),
        grid_spec=grid_spec,
        compiler_params=pltpu.CompilerParams(
            dimension_semantics=("arbitrary", "arbitrary"), vmem_limit_bytes=VMEM_LIMIT),
        name="experts",
    )(item_e, item_blk0, item_nb, src_row, dst_row, h_all.reshape(N_ROWS // SUBLANES, SUBLANES, D_MODEL),
      norm_ffn, w_gate, w_up, w_down).reshape(PAIR_ROWS, D_MODEL)


def _dispatch_tables(ids, counts):
    flat_e = ids[:, 0:TOP_K].reshape(-1)
    rank = ids[:, TOP_K:2 * TOP_K].reshape(-1)
    slot = flat_e * EXPERT_SLOTS + rank
    slot_pair = jnp.full((SLOT_TABLE,), N_PAIRS, jnp.int32).at[slot].set(jnp.arange(N_PAIRS, dtype=jnp.int32))
    slot_id = jnp.arange(SLOT_TABLE, dtype=jnp.int32)
    src_row = jnp.minimum(slot_pair // TOP_K, N_ROWS - 1)
    dst_row = jnp.where(slot_pair < N_PAIRS, (slot_pair % TOP_K) * PAIR_REGION + slot_pair // TOP_K,
                        DUMP_ROW0 + slot_id % ROW_BLOCK)
    nblk = (counts + ROW_BLOCK - 1) // ROW_BLOCK
    nitem = (nblk + ITEM_BLOCKS - 1) // ITEM_BLOCKS
    upto = jnp.arange(N_EXPERTS)[None, :] <= jnp.arange(N_EXPERTS)[:, None]
    item_end = jnp.sum(jnp.where(upto, nitem[None, :], 0), axis=1)
    total = item_end[-1]
    idx = jnp.arange(MAX_ITEMS, dtype=jnp.int32)
    live = idx < total
    e_of = jnp.minimum(jnp.searchsorted(item_end, jnp.minimum(idx, total - 1), side='right'),
                       N_EXPERTS - 1).astype(jnp.int32)
    local = jnp.minimum(idx, total - 1) - (item_end - nitem)[e_of]
    item_blk0 = (e_of * EXPERT_BLOCKS + local * ITEM_BLOCKS).astype(jnp.int32)
    item_nb = jnp.where(live, jnp.clip(nblk[e_of] - local * ITEM_BLOCKS, 0, ITEM_BLOCKS), 0).astype(jnp.int32)
    return e_of, item_blk0, item_nb, src_row, dst_row


def _final_rows(h, y0, y1, ew, p, wg_ref, wp_ref, np_ref, fn_ref):
    h = h + (ew[:, 0:1] * y0 + ew[:, 1:2] * y1)
    gate = _sigmoid(_bdot(_rms(h, np_ref[...]).astype(BF16), wg_ref[...]))
    h = h + gate * _bdot(p.astype(BF16), wp_ref[...])
    return _rms(h, fn_ref[...])


def _final_kernel(h_ref, y0_ref, y1_ref, ew_ref, p_ref, hs_ref, y0s_ref, y1s_ref, ews_ref, ps_ref,
                  wg_hbm, wp_hbm, np_ref, fn_ref, op_ref, os_ref, wg_ref, wp_ref, stage_ref, sem):
    i = pl.program_id(0)

    @pl.when(i == 0)
    def _():
        _load_weight_bf16(wg_hbm, wg_ref, stage_ref, sem)
        _load_weight_bf16(wp_hbm, wp_ref, stage_ref, sem)

    @pl.when(i < FIN_PROMPT_STEPS)
    def _():
        op_ref[...] = _final_rows(h_ref[...], y0_ref[...], y1_ref[...], ew_ref[...], p_ref[...],
                                  wg_ref, wp_ref, np_ref, fn_ref)

    @pl.when(i == FIN_PROMPT_STEPS)
    def _():
        os_ref[...] = _final_rows(hs_ref[...], y0s_ref[...], y1s_ref[...], ews_ref[...], ps_ref[...],
                                  wg_ref, wp_ref, np_ref, fn_ref)


def _final(h_all, ypairs, ew, p_prompt, p_sample, w_gate, w_proj, norm_ple, final_norm):
    last = FIN_PROMPT_STEPS - 1
    prow = lambda i: (jnp.minimum(i, last), 0)
    const2 = lambda i: (0, 0)
    hbm = pl.BlockSpec(memory_space=pltpu.MemorySpace.HBM)
    sblk = N_PROMPT // N_SAMPLE
    pact = pl.BlockSpec((FIN_TM, D_MODEL), prow)
    sact = lambda off: pl.BlockSpec((N_SAMPLE, D_MODEL), lambda i: (off + sblk, 0))
    return pl.pallas_call(
        _final_kernel,
        out_shape=(jax.ShapeDtypeStruct((N_PROMPT, D_MODEL), F32), jax.ShapeDtypeStruct((N_SAMPLE, D_MODEL), F32)),
        grid=(FIN_PROMPT_STEPS + 1,),
        in_specs=[pact, pact,
                  pl.BlockSpec((FIN_TM, D_MODEL), lambda i: (jnp.minimum(i, last) + PAIR_REGION // FIN_TM, 0)),
                  pl.BlockSpec((FIN_TM, LANES), prow),
                  pl.BlockSpec((FIN_TM, PLE_DIM), prow),
                  sact(0), sact(0), sact(PAIR_REGION // N_SAMPLE),
                  pl.BlockSpec((N_SAMPLE, LANES), lambda i: (sblk, 0)),
                  pl.BlockSpec((N_SAMPLE, PLE_DIM), const2),
                  hbm, hbm,
                  pl.BlockSpec((1, D_MODEL), const2),
                  pl.BlockSpec((1, D_MODEL), const2)],
        out_specs=(pact, pl.BlockSpec((N_SAMPLE, D_MODEL), const2)),
        scratch_shapes=[pltpu.VMEM((D_MODEL, D_MODEL), BF16), pltpu.VMEM((PLE_DIM, D_MODEL), BF16)] + _WEIGHT_STAGE,
        compiler_params=pltpu.CompilerParams(
            dimension_semantics=("arbitrary",), vmem_limit_bytes=VMEM_LIMIT),
        name="final",
    )(h_all, ypairs, ypairs, ew, p_prompt, h_all, ypairs, ypairs, ew, p_sample, w_gate, w_proj, norm_ple, final_norm)


def kernel(x_prompt, x_sample, state_conv, p_prompt, p_sample, norm_mix, w_in, ln_v_g, ln_v_b, w_spatial,
           b_spatial, w_proj_a, conv_w, conv_b, ln_c_g, ln_c_b, w_proj_b, w_out, norm_ffn, w_router_group,
           b_router_group, w_router_expert, b_router_expert, w_exp_gate, w_exp_up, w_exp_down, norm_ple,
           w_ple_gate, w_ple_proj, final_norm):
    assert w_in.shape[0] == 1, "single layer"
    vec = lambda v: v.reshape(1, -1)
    xp = x_prompt.reshape(N_PROMPT, D_MODEL)
    xs = x_sample.reshape(N_SAMPLE, D_MODEL)
    xn = _xnorm(xp, xs, vec(norm_mix[0]))

    w_s, b_s = w_spatial[0], b_spatial[0]
    wmix = jnp.tril(w_s).astype(BF16)
    bmix = jnp.repeat(b_s.T, A_GROUP_DIM, axis=1)
    w00 = jnp.repeat(w_s[:, 0, 0], A_GROUP_DIM).reshape(1, D_MODEL)
    b00 = jnp.repeat(b_s[:, 0], A_GROUP_DIM).reshape(1, D_MODEL)

    act = _in_proj(xn, w_in[0])

    cw = conv_w[0]
    new_state, conv_s = _sconv(jnp.transpose(state_conv[0], (1, 0, 2)), act, cw, vec(conv_b[0]))
    new_conv_sample = jnp.transpose(new_state, (1, 0, 2))[None]
    cw_pad = jnp.concatenate([cw, jnp.zeros((1, D_MODEL), F32)], axis=0)
    m_all, vn_s = _seqmix(act, conv_s, wmix, bmix, w00, b00, vec(ln_v_g[0]), vec(ln_v_b[0]), cw_pad,
                          vec(conv_b[0]), vec(ln_c_g[0]), vec(ln_c_b[0]), w_proj_a[0], w_proj_b[0])

    w_route = jnp.zeros((D_MODEL, LANES), F32)
    w_route = w_route.at[:, :MOE_GROUPS].set(w_router_group[0])
    w_route = w_route.at[:, MOE_GROUPS:MOE_GROUPS + N_EXPERTS].set(w_router_expert[0])
    b_route = jnp.zeros((1, LANES), F32)
    b_route = b_route.at[0, :MOE_GROUPS].set(b_router_group[0])
    b_route = b_route.at[0, MOE_GROUPS:MOE_GROUPS + N_EXPERTS].set(b_router_expert[0])

    h_all, ids, ew, counts = _mixout(m_all, xp, xs, w_out[0], vec(norm_ffn[0]), w_route.astype(BF16), b_route)

    tables = _dispatch_tables(ids[:, :2 * TOP_K], counts[0, :N_EXPERTS])
    ypairs = _experts(*tables, h_all, vec(norm_ffn[0]), w_exp_gate[0], w_exp_up[0], w_exp_down[0])

    y_prompt, y_sample = _final(h_all, ypairs, ew, p_prompt[0].reshape(N_PROMPT, PLE_DIM),
                                p_sample[0].reshape(N_SAMPLE, PLE_DIM), w_ple_gate[0], w_ple_proj[0],
                                vec(norm_ple[0]), vec(final_norm))
    y_prompt = y_prompt.reshape(N_PROMPT_SEQ, SEQ, D_MODEL)
    y_sample = y_sample.reshape(N_SAMPLE, 1, D_MODEL)
    glu_cols = slice(ACT_GLU * D_MODEL, (ACT_GLU + 1) * D_MODEL)
    new_conv_prompt = jnp.stack([act[(b + 1) * SEQ - CONV_HIST:(b + 1) * SEQ, glu_cols]
                                 for b in range(N_PROMPT_SEQ)])[None]
    return (y_prompt, y_sample, new_conv_prompt, new_conv_sample, vn_s.reshape(1, N_SAMPLE, 1, D_MODEL))
```

```python
import jax
import jax.numpy as jnp
from jax import lax
from jax.experimental import pallas as pl
from jax.experimental.pallas import tpu as pltpu

F32 = jnp.float32
BF16 = jnp.bfloat16

D_MODEL = 2048
N_PROMPT_SEQ = 4
SEQ = 2048
N_PROMPT = N_PROMPT_SEQ * SEQ
N_SAMPLE = 128
N_ROWS = N_PROMPT + N_SAMPLE
CHUNK = 128
N_CHUNKS = N_ROWS // CHUNK
CHUNKS_PER_SEQ = SEQ // CHUNK
A_GROUPS = 8
A_GROUP_DIM = D_MODEL // A_GROUPS
CONV_WIDTH = 31
CONV_HIST = CONV_WIDTH - 1
HIST_BLOCK = 32
MOE_GROUPS = 4
EXPERTS_PER_GROUP = 8
N_EXPERTS = MOE_GROUPS * EXPERTS_PER_GROUP
TOP_K = 2
D_EXPERT = D_MODEL // 2
PLE_DIM = 256
EPS = 1e-6
LANES = 128

XN_TM = 512
IN_TM = 1664
IN_TN = 512
IN_COL_TILES = D_MODEL // IN_TN
ACT_BLOCKS = 5
ACT_GU, ACT_GV, ACT_GLU, ACT_SGA, ACT_SGB = range(ACT_BLOCKS)
MIX_TM = 512
MIX_PROMPT_STEPS = N_PROMPT // MIX_TM
FIN_TM = 256
FIN_PROMPT_STEPS = N_PROMPT // FIN_TM
ROW_BLOCK = 128
SUBLANES = 8
BLOCK_TILES = ROW_BLOCK // SUBLANES
N_PAIRS = N_ROWS * TOP_K
MAX_BLOCKS = N_PAIRS // ROW_BLOCK + N_EXPERTS
ITEM_BLOCKS = 8
ITEM_ROWS = ITEM_BLOCKS * ROW_BLOCK
MAX_ITEMS = (MAX_BLOCKS + (ITEM_BLOCKS - 1) * N_EXPERTS) // ITEM_BLOCKS
EXPERT_BLOCKS = N_ROWS // ROW_BLOCK + ITEM_BLOCKS
EXPERT_SLOTS = EXPERT_BLOCKS * ROW_BLOCK
SLOT_TABLE = N_EXPERTS * EXPERT_SLOTS
EXPERT_SLICES = 4
EXPERT_TN = D_EXPERT // EXPERT_SLICES
PAIR_REGION = 8448
DUMP_ROW0 = PAIR_REGION + N_ROWS
PAIR_ROWS = DUMP_ROW0 + ROW_BLOCK

VMEM_LIMIT = 56 * 1024 * 1024


def _rms(x, g):
    return x * lax.rsqrt(jnp.mean(x * x, axis=-1, keepdims=True) + EPS) * g


def _layer_norm(x, g, b):
    mu = jnp.mean(x, axis=-1, keepdims=True)
    xc = x - mu
    return xc * lax.rsqrt(jnp.mean(xc * xc, axis=-1, keepdims=True) + EPS) * g + b


def _sigmoid(x):
    return 1.0 / (1.0 + jnp.exp(-x))


def _silu(x):
    return x * _sigmoid(x)


def _gelu(x):
    return jax.nn.gelu(x, approximate=True)


def _bdot(a, b):
    return jnp.dot(a, b, preferred_element_type=F32)


STAGE_ROWS = 256


def _load_weight_bf16(w_hbm, dst_ref, stage_ref, sem):
    n_chunks = w_hbm.shape[0] // STAGE_ROWS

    def chunk_copy(j):
        return pltpu.make_async_copy(w_hbm.at[pl.ds(j * STAGE_ROWS, STAGE_ROWS)], stage_ref.at[j % 2], sem.at[j % 2])

    chunk_copy(0).start()
    for j in range(n_chunks):
        if j + 1 < n_chunks:
            chunk_copy(j + 1).start()
        chunk_copy(j).wait()
        dst_ref[pl.ds(j * STAGE_ROWS, STAGE_ROWS), :] = stage_ref[j % 2].astype(BF16)


_WEIGHT_STAGE = [pltpu.VMEM((2, STAGE_ROWS, D_MODEL), F32), pltpu.SemaphoreType.DMA((2,))]


def _xnorm_kernel(xp_ref, xs_ref, nm_ref, xn_ref):
    i = pl.program_id(0)

    @pl.when(i < N_PROMPT // XN_TM)
    def _():
        xn_ref[...] = _rms(xp_ref[...], nm_ref[...]).astype(BF16)

    @pl.when(i == N_PROMPT // XN_TM)
    def _():
        xn_ref[0:N_SAMPLE, :] = _rms(xs_ref[...], nm_ref[...]).astype(BF16)
        xn_ref[N_SAMPLE:, :] = jnp.zeros((XN_TM - N_SAMPLE, D_MODEL), BF16)


def _xnorm(x_prompt, x_sample, norm_mix):
    last = N_PROMPT // XN_TM - 1
    return pl.pallas_call(
        _xnorm_kernel,
        out_shape=jax.ShapeDtypeStruct((N_ROWS, D_MODEL), BF16),
        grid=(N_PROMPT // XN_TM + 1,),
        in_specs=[pl.BlockSpec((XN_TM, D_MODEL), lambda i: (jnp.minimum(i, last), 0)),
                  pl.BlockSpec((N_SAMPLE, D_MODEL), lambda i: (0, 0)),
                  pl.BlockSpec((1, D_MODEL), lambda i: (0, 0))],
        out_specs=pl.BlockSpec((XN_TM, D_MODEL), lambda i: (i, 0)),
        name="xnorm",
    )(x_prompt, x_sample, norm_mix)


def _in_w_col(j):
    t = IN_COL_TILES
    jj = j - 2 * t
    glu_col = 2 * t + (jj % 2) * t + jj // 2
    return jnp.where((j >= 2 * t) & (j < 4 * t), glu_col, j)


def _in_out_col(j):
    t = IN_COL_TILES
    return jnp.where(j < 2 * t, j, jnp.where(j < 4 * t, 2 * t + (j - 2 * t) // 2, j - t))


def _in_proj_kernel(xn_ref, w_ref, o_ref, val_ref):
    j = pl.program_id(1)
    t = IN_COL_TILES

    def z():
        return _bdot(xn_ref[...], w_ref[...].astype(BF16))

    @pl.when(j < 2 * t)
    def _():
        o_ref[...] = _gelu(z())

    @pl.when((j >= 2 * t) & (j < 4 * t) & (j % 2 == 0))
    def _():
        val_ref[...] = z()

    @pl.when((j >= 2 * t) & (j < 4 * t) & (j % 2 == 1))
    def _():
        o_ref[...] = val_ref[...] * _sigmoid(z())

    @pl.when(j >= 4 * t)
    def _():
        o_ref[...] = _sigmoid(z())


def _in_proj(xn, w_in):
    return pl.pallas_call(
        _in_proj_kernel,
        out_shape=jax.ShapeDtypeStruct((N_ROWS, ACT_BLOCKS * D_MODEL), F32),
        grid=(N_ROWS // IN_TM, 6 * IN_COL_TILES),
        in_specs=[
            pl.BlockSpec((IN_TM, D_MODEL), lambda i, j: (i, 0)),
            pl.BlockSpec((D_MODEL, IN_TN), lambda i, j: (0, _in_w_col(j))),
        ],
        out_specs=pl.BlockSpec((IN_TM, IN_TN), lambda i, j: (i, _in_out_col(j))),
        scratch_shapes=[pltpu.VMEM((IN_TM, IN_TN), F32)],
        compiler_params=pltpu.CompilerParams(
            dimension_semantics=("arbitrary", "arbitrary"), vmem_limit_bytes=VMEM_LIMIT),
        name="in_proj",
    )(xn, w_in)


SCONV_B = 16


def _sconv_kernel(st_ref, glu_ref, w_ref, wl_ref, cb_ref, new_ref, conv_ref):
    glu = glu_ref[...]
    new_ref[0:CONV_HIST - 1] = st_ref[1:CONV_HIST]
    new_ref[CONV_HIST - 1] = glu
    acc = st_ref[0] * w_ref[0:1, :]
    for k in range(1, CONV_HIST):
        acc = acc + st_ref[k] * w_ref[k:k + 1, :]
    conv_ref[...] = acc + glu * wl_ref[...] + cb_ref[...]


def _sconv(state, act, conv_w, conv_b):
    return pl.pallas_call(
        _sconv_kernel,
        out_shape=(
            jax.ShapeDtypeStruct((CONV_HIST, N_SAMPLE, D_MODEL), F32),
            jax.ShapeDtypeStruct((N_SAMPLE, D_MODEL), F32),
        ),
        grid=(N_SAMPLE // SCONV_B,),
        in_specs=[
            pl.BlockSpec((CONV_HIST, SCONV_B, D_MODEL), lambda i: (0, i, 0)),
            pl.BlockSpec((SCONV_B, D_MODEL), lambda i: (N_PROMPT // SCONV_B + i, ACT_GLU)),
            pl.BlockSpec((CONV_HIST, D_MODEL), lambda i: (0, 0)),
            pl.BlockSpec((1, D_MODEL), lambda i: (0, 0)),
            pl.BlockSpec((1, D_MODEL), lambda i: (0, 0)),
        ],
        out_specs=(
            pl.BlockSpec((CONV_HIST, SCONV_B, D_MODEL), lambda i: (0, i, 0)),
            pl.BlockSpec((SCONV_B, D_MODEL), lambda i: (i, 0)),
        ),
        name="sconv",
    )(state, act, conv_w[:CONV_HIST], conv_w[CONV_HIST:], conv_b)


CONV_STRIP = 256
WIN_ROWS = HIST_BLOCK + CHUNK


def _seqmix_kernel(gv_ref, gu_ref, cur_ref, prev_ref, sga_ref, sgb_ref, convs_ref, wmix_ref, bmix_ref, w00_ref,
                   b00_ref, lvg_ref, lvb_ref, cw_ref, cb_ref, lcg_ref, lcb_ref, wa_hbm, wb_hbm,
                   m_ref, vns_ref, win_ref, conv_ref, a_ref, b_ref, wa_ref, wb_ref, stage_ref, sem):
    c = pl.program_id(0)
    is_sample = c == N_CHUNKS - 1

    @pl.when(c == 0)
    def _():
        _load_weight_bf16(wa_hbm, wa_ref, stage_ref, sem)
        _load_weight_bf16(wb_hbm, wb_ref, stage_ref, sem)

    fresh = (c % CHUNKS_PER_SEQ) == 0
    win_ref[0:HIST_BLOCK, :] = jnp.where(fresh, 0.0, prev_ref[...]).astype(BF16).astype(F32)
    win_ref[HIST_BLOCK:WIN_ROWS, :] = cur_ref[...].astype(BF16).astype(F32)
    off = HIST_BLOCK - CONV_HIST

    vn = _layer_norm(gv_ref[...], lvg_ref[...], lvb_ref[...])
    vns_ref[...] = vn

    vnb = vn.astype(BF16)
    for g in range(A_GROUPS):
        sl = slice(g * A_GROUP_DIM, (g + 1) * A_GROUP_DIM)
        s_chunk = _bdot(wmix_ref[g], vnb[:, sl]) + bmix_ref[:, sl]
        s_first = vn[:, sl] * w00_ref[:, sl] + b00_ref[:, sl]
        a_ref[:, sl] = (gu_ref[:, sl] * jnp.where(is_sample, s_first, s_chunk)).astype(BF16)
    pa = _bdot(a_ref[...], wa_ref[...])

    def phase_sum(r, cs):
        rows = CHUNK + (SUBLANES if r else 0)
        total = None
        for k in range(CONV_WIDTH):
            q, rk = divmod(off + k, SUBLANES)
            if rk == r:
                term = win_ref[SUBLANES * q:SUBLANES * q + rows, cs] * cw_ref[k:k + 1, cs].astype(BF16).astype(F32)
                total = term if total is None else total + term
        return total[r:r + CHUNK, :]

    for s0 in range(0, D_MODEL, CONV_STRIP):
        cs = slice(s0, s0 + CONV_STRIP)
        acc = phase_sum(0, cs)
        for r in range(1, SUBLANES):
            acc = acc + phase_sum(r, cs)
        conv_ref[:, cs] = acc + cb_ref[:, cs]
    conv = jnp.where(is_sample, convs_ref[...], conv_ref[...])
    b_ref[...] = _silu(_layer_norm(conv, lcg_ref[...], lcb_ref[...])).astype(BF16)

    pb = _bdot(b_ref[...], wb_ref[...])
    m_ref[...] = (sga_ref[...] * pa + sgb_ref[...] * pb).astype(BF16)


def _seqmix(act, conv_s, wmix, bmix, w00, b00, ln_v_g, ln_v_b, conv_w, conv_b, ln_c_g, ln_c_b, wa, wb):
    row = lambda c: (c, 0)
    const2 = lambda c: (0, 0)
    vec = pl.BlockSpec((1, D_MODEL), const2)
    hbm = pl.BlockSpec(memory_space=pltpu.MemorySpace.HBM)
    hist_per_chunk = CHUNK // HIST_BLOCK
    act_block = lambda blk: pl.BlockSpec((CHUNK, D_MODEL), lambda c: (c, blk))
    return pl.pallas_call(
        _seqmix_kernel,
        out_shape=(
            jax.ShapeDtypeStruct((N_ROWS, D_MODEL), BF16),
            jax.ShapeDtypeStruct((N_SAMPLE, D_MODEL), F32),
        ),
        grid=(N_CHUNKS,),
        in_specs=[
            act_block(ACT_GV), act_block(ACT_GU), act_block(ACT_GLU),
            pl.BlockSpec((HIST_BLOCK, D_MODEL), lambda c: (jnp.maximum(c * hist_per_chunk - 1, 0), ACT_GLU)),
            act_block(ACT_SGA), act_block(ACT_SGB),
            pl.BlockSpec((N_SAMPLE, D_MODEL), const2),
            pl.BlockSpec((A_GROUPS, CHUNK, CHUNK), lambda c: (0, 0, 0)),
            pl.BlockSpec((CHUNK, D_MODEL), const2),
            vec, vec, vec, vec,
            pl.BlockSpec((CONV_WIDTH + 1, D_MODEL), const2),
            vec, vec, vec,
            hbm, hbm,
        ],
        out_specs=(
            pl.BlockSpec((CHUNK, D_MODEL), row),
            pl.BlockSpec((N_SAMPLE, D_MODEL), const2),
        ),
        scratch_shapes=[pltpu.VMEM((WIN_ROWS, D_MODEL), F32),
                        pltpu.VMEM((CHUNK, D_MODEL), F32),
                        pltpu.VMEM((CHUNK, D_MODEL), BF16), pltpu.VMEM((CHUNK, D_MODEL), BF16),
                        pltpu.VMEM((D_MODEL, D_MODEL), BF16), pltpu.VMEM((D_MODEL, D_MODEL), BF16)] + _WEIGHT_STAGE,
        compiler_params=pltpu.CompilerParams(
            dimension_semantics=("arbitrary",), vmem_limit_bytes=VMEM_LIMIT),
        name="seqmix",
    )(act, act, act, act, act, act, conv_s, wmix, bmix, w00, b00, ln_v_g, ln_v_b, conv_w, conv_b, ln_c_g, ln_c_b,
      wa, wb)


def _route(logits):
    col = lax.broadcasted_iota(jnp.int32, logits.shape, 1).astype(F32)
    neg = jnp.float32(-jnp.inf)
    big = jnp.float32(1e9)
    lg = jnp.where(col < MOE_GROUPS, logits, neg)
    gmax = jnp.max(lg, axis=-1, keepdims=True)
    gi = jnp.min(jnp.where(lg == gmax, col, big), axis=-1, keepdims=True)
    gw = 1.0 / jnp.sum(jnp.exp(lg - gmax), axis=-1, keepdims=True)
    lo = MOE_GROUPS + gi * EXPERTS_PER_GROUP
    le = jnp.where((col >= lo) & (col < lo + EXPERTS_PER_GROUP), logits, neg)
    m1 = jnp.max(le, axis=-1, keepdims=True)
    i1 = jnp.min(jnp.where(le == m1, col, big), axis=-1, keepdims=True)
    le2 = jnp.where(col == i1, neg, le)
    m2 = jnp.max(le2, axis=-1, keepdims=True)
    i2 = jnp.min(jnp.where(le2 == m2, col, big), axis=-1, keepdims=True)
    e = jnp.exp(m2 - m1)
    w1 = gw / (1.0 + e)
    w2 = gw * e / (1.0 + e)
    eid = jnp.where(col == 0, i1 - MOE_GROUPS, jnp.where(col == 1, i2 - MOE_GROUPS, 0.0)).astype(jnp.int32)
    ew = jnp.where(col == 0, w1, jnp.where(col == 1, w2, 0.0))
    return eid, ew


def _mixout_rows(m, x, wo_ref, nf_ref, wr_ref, br_ref, tril, count_ref):
    h = x + _bdot(m, wo_ref[...])
    hn = _rms(h, nf_ref[...])
    logits = _bdot(hn.astype(BF16), wr_ref[...]) + br_ref[...]
    eid, ew = _route(logits)
    col = lax.broadcasted_iota(jnp.int32, eid.shape, 1)
    first = col == eid[:, 0:1]
    second = col == eid[:, 1:2]
    hits = first.astype(F32) + second.astype(F32)
    before = _bdot(tril, hits.astype(BF16)) + count_ref[...]
    rank0 = jnp.sum(jnp.where(first, before, 0.0), axis=-1, keepdims=True)
    rank1 = jnp.sum(jnp.where(second, before + first.astype(F32), 0.0), axis=-1, keepdims=True)
    count_ref[...] = count_ref[...] + jnp.sum(hits, axis=0, keepdims=True)
    ids = jnp.where(col == 2, rank0.astype(jnp.int32), jnp.where(col == 3, rank1.astype(jnp.int32), eid))
    return h, ids, ew


def _mixout_kernel(mp_ref, xp_ref, ms_ref, xs_ref, wo_hbm, nf_ref, wr_ref, br_ref, tril_ref,
                   h_ref, eid_ref, ew_ref, cnt_ref, wo_ref, count_ref, stage_ref, sem):
    i = pl.program_id(0)

    @pl.when(i == 0)
    def _():
        _load_weight_bf16(wo_hbm, wo_ref, stage_ref, sem)
        count_ref[...] = jnp.zeros((1, LANES), F32)

    @pl.when(i < MIX_PROMPT_STEPS)
    def _():
        h, eid, ew = _mixout_rows(mp_ref[...], xp_ref[...], wo_ref, nf_ref, wr_ref, br_ref, tril_ref[...], count_ref)
        h_ref[...] = h
        eid_ref[...] = eid
        ew_ref[...] = ew

    @pl.when(i == MIX_PROMPT_STEPS)
    def _():
        h, eid, ew = _mixout_rows(ms_ref[...], xs_ref[...], wo_ref, nf_ref, wr_ref, br_ref,
                                  tril_ref[0:N_SAMPLE, 0:N_SAMPLE], count_ref)
        for ref, val in ((h_ref, h), (eid_ref, eid), (ew_ref, ew)):
            ref[0:N_SAMPLE, :] = val
            ref[N_SAMPLE:, :] = jnp.zeros((MIX_TM - N_SAMPLE, val.shape[1]), val.dtype)

    cnt_ref[...] = jnp.broadcast_to(count_ref[...], cnt_ref.shape).astype(jnp.int32)


def _mixout(m_all, x_prompt, x_sample, wo, norm_ffn, w_route, b_route):
    last = MIX_PROMPT_STEPS - 1
    row = lambda i: (i, 0)
    prow = lambda i: (jnp.minimum(i, last), 0)
    const2 = lambda i: (0, 0)
    hbm = pl.BlockSpec(memory_space=pltpu.MemorySpace.HBM)
    tile_row = jnp.arange(MIX_TM, dtype=jnp.int32)
    tril = (tile_row[None, :] < tile_row[:, None]).astype(BF16)
    return pl.pallas_call(
        _mixout_kernel,
        out_shape=(
            jax.ShapeDtypeStruct((N_ROWS, D_MODEL), F32),
            jax.ShapeDtypeStruct((N_ROWS, LANES), jnp.int32),
            jax.ShapeDtypeStruct((N_ROWS, LANES), F32),
            jax.ShapeDtypeStruct((SUBLANES, LANES), jnp.int32),
        ),
        grid=(MIX_PROMPT_STEPS + 1,),
        in_specs=[pl.BlockSpec((MIX_TM, D_MODEL), prow), pl.BlockSpec((MIX_TM, D_MODEL), prow),
                  pl.BlockSpec((N_SAMPLE, D_MODEL), lambda i: (N_PROMPT // N_SAMPLE, 0)),
                  pl.BlockSpec((N_SAMPLE, D_MODEL), const2),
                  hbm,
                  pl.BlockSpec((1, D_MODEL), const2),
                  pl.BlockSpec((D_MODEL, LANES), const2),
                  pl.BlockSpec((1, LANES), const2),
                  pl.BlockSpec((MIX_TM, MIX_TM), const2)],
        out_specs=(pl.BlockSpec((MIX_TM, D_MODEL), row), pl.BlockSpec((MIX_TM, LANES), row),
                   pl.BlockSpec((MIX_TM, LANES), row), pl.BlockSpec((SUBLANES, LANES), const2)),
        scratch_shapes=[pltpu.VMEM((D_MODEL, D_MODEL), BF16), pltpu.VMEM((1, LANES), F32)] + _WEIGHT_STAGE,
        compiler_params=pltpu.CompilerParams(
            dimension_semantics=("arbitrary",), vmem_limit_bytes=VMEM_LIMIT),
        name="mixout",
    )(m_all, x_prompt, m_all, x_sample, wo, norm_ffn, w_route, b_route, tril)


def _experts_kernel(item_e, item_blk0, item_nb, src_hbm, dst_hbm, h_hbm, nf_ref, wg_ref, wu_ref, wd_ref,
                    y_hbm, src_s, dst_s, pend_s, xg_ref, yst_ref, xs_ref, hm_ref, wgb_ref, wub_ref, wdb_ref,
                    idx_sem, g_sem, s_sem):
    w = pl.program_id(0)
    s = pl.program_id(1)
    nb = item_nb[w]
    nxt = jnp.minimum(w + 1, MAX_ITEMS - 1)
    nb_next = jnp.where(w + 1 < MAX_ITEMS, item_nb[nxt], 0)

    def table_fetch(tbl_hbm, tbl_s, sem_idx, item):
        slot0 = pl.multiple_of(item_blk0[item] * ROW_BLOCK, ROW_BLOCK)
        return pltpu.make_async_copy(tbl_hbm.at[pl.ds(slot0, ITEM_ROWS)], tbl_s, idx_sem.at[sem_idx])

    def row_copy_in(t, u, blk):
        src = src_s[t * SUBLANES + u]
        return pltpu.make_async_copy(h_hbm.at[src >> 3, pl.ds(src & (SUBLANES - 1), 1)],
                                     xg_ref.at[t, pl.ds(u, 1)], g_sem.at[blk])

    def row_copy_out(t, u, blk):
        dst = dst_s[t * SUBLANES + u]
        return pltpu.make_async_copy(yst_ref.at[t, pl.ds(u, 1)],
                                     y_hbm.at[dst >> 3, pl.ds(dst & (SUBLANES - 1), 1)], s_sem.at[blk])

    def block_rows(blk):
        return pl.ds(pl.multiple_of(blk * ROW_BLOCK, ROW_BLOCK), ROW_BLOCK)

    def block_tiles(blk):
        return pl.ds(pl.multiple_of(blk * BLOCK_TILES, BLOCK_TILES), BLOCK_TILES)

    def block_copy_wait(sem, blk):
        pltpu.make_async_copy(xg_ref.at[block_tiles(blk)], xg_ref.at[block_tiles(blk)], sem.at[blk]).wait()

    def for_each_row(n_blocks, row_copy):
        def tile(t, carry):
            for u in range(SUBLANES):
                row_copy(t, u, lax.shift_right_logical(t, BLOCK_TILES.bit_length() - 1)).start()
            return carry
        lax.fori_loop(0, n_blocks * BLOCK_TILES, tile, 0)

    def issue_gather(n_blocks):
        for_each_row(n_blocks, row_copy_in)

    def drain_scatter():
        def drain(blk, carry):
            block_copy_wait(s_sem, blk)
            return carry
        lax.fori_loop(0, pend_s[0], drain, 0)
        pend_s[0] = 0

    @pl.when((w == 0) & (s == 0))
    def _():
        pad_tiles = pl.ds(0, BLOCK_TILES)
        xg_ref[pad_tiles] = jnp.zeros((BLOCK_TILES, SUBLANES, D_MODEL), F32)
        for row0 in (N_ROWS, DUMP_ROW0):
            cz = pltpu.make_async_copy(xg_ref.at[pad_tiles], y_hbm.at[pl.ds(row0 // SUBLANES, BLOCK_TILES)],
                                       idx_sem.at[0])
            cz.start()
            cz.wait()
        pend_s[0] = 0
        ci = table_fetch(src_hbm, src_s, 0, w)
        ci.start()
        ci.wait()
        issue_gather(nb)

    @pl.when((s == 0) & (nb > 0))
    def _():
        table_fetch(dst_hbm, dst_s, 1, w).start()

        def land(blk, carry):
            block_copy_wait(g_sem, blk)
            x = xg_ref[block_tiles(blk)].reshape(ROW_BLOCK, D_MODEL)
            xs_ref[block_rows(blk), :] = _rms(x, nf_ref[...]).astype(BF16)
            return carry
        lax.fori_loop(0, nb, land, 0)

    @pl.when((s == 1) & (nb_next > 0))
    def _():
        table_fetch(src_hbm, src_s, 0, nxt).start()

    @pl.when((s == 2) & (nb_next > 0))
    def _():
        table_fetch(src_hbm, src_s, 0, nxt).wait()
        issue_gather(nb_next)

    def over_row_blocks(fn):
        def four_blocks(j, carry):
            fn(pl.multiple_of(j * (4 * ROW_BLOCK), 4 * ROW_BLOCK), 4 * ROW_BLOCK)
            return carry
        lax.fori_loop(0, nb >> 2, four_blocks, 0)

        @pl.when((nb & 2) == 2)
        def _():
            fn(pl.multiple_of((nb >> 2) * (4 * ROW_BLOCK), 2 * ROW_BLOCK), 2 * ROW_BLOCK)

        @pl.when((nb & 1) == 1)
        def _():
            fn(pl.multiple_of((nb - 1) * ROW_BLOCK, ROW_BLOCK), ROW_BLOCK)

    @pl.when(nb > 0)
    def _():
        wgb_ref[...] = wg_ref[...].astype(BF16)
        wub_ref[...] = wu_ref[...].astype(BF16)
        wdb_ref[pl.ds(pl.multiple_of(s * EXPERT_TN, EXPERT_TN), EXPERT_TN), :] = wd_ref[...].astype(BF16)

        def gate_up(row0, nrows):
            rows = pl.ds(row0, nrows)
            x = xs_ref[rows, :]
            hm_ref[s, rows, :] = (_silu(_bdot(x, wgb_ref[...])) * _bdot(x, wub_ref[...])).astype(BF16)
        over_row_blocks(gate_up)

    @pl.when((s == EXPERT_SLICES - 1) & (nb > 0))
    def _():
        drain_scatter()

        def down(row0, nrows):
            rows = pl.ds(row0, nrows)
            hm = jnp.concatenate([hm_ref[k, rows, :] for k in range(EXPERT_SLICES)], axis=1)
            tiles = pl.ds(pl.multiple_of(row0 // SUBLANES, BLOCK_TILES), nrows // SUBLANES)
            yst_ref[tiles] = _bdot(hm, wdb_ref[...]).reshape(nrows // SUBLANES, SUBLANES, D_MODEL)
        over_row_blocks(down)

        table_fetch(dst_hbm, dst_s, 1, w).wait()

        for_each_row(nb, row_copy_out)
        pend_s[0] = nb

    @pl.when((w == MAX_ITEMS - 1) & (s == EXPERT_SLICES - 1))
    def _():
        drain_scatter()


def _experts(item_e, item_blk0, item_nb, src_row, dst_row, h_all, norm_ffn, w_gate, w_up, w_down):
    hbm = pl.BlockSpec(memory_space=pltpu.MemorySpace.HBM)

    def w_slice(w, s, inb):
        return jnp.where(inb[w] > 0, s, EXPERT_SLICES - 1)

    grid_spec = pltpu.PrefetchScalarGridSpec(
        num_scalar_prefetch=3,
        grid=(MAX_ITEMS, EXPERT_SLICES),
        in_specs=[
            hbm, hbm, hbm,
            pl.BlockSpec((1, D_MODEL), lambda w, s, ie, ib, inb: (0, 0)),
            pl.BlockSpec((None, D_MODEL, EXPERT_TN), lambda w, s, ie, ib, inb: (ie[w], 0, w_slice(w, s, inb))),
            pl.BlockSpec((None, D_MODEL, EXPERT_TN), lambda w, s, ie, ib, inb: (ie[w], 0, w_slice(w, s, inb))),
            pl.BlockSpec((None, EXPERT_TN, D_MODEL), lambda w, s, ie, ib, inb: (ie[w], w_slice(w, s, inb), 0)),
        ],
        out_specs=hbm,
        scratch_shapes=[
            pltpu.SMEM((ITEM_ROWS,), jnp.int32),
            pltpu.SMEM((ITEM_ROWS,), jnp.int32),
            pltpu.SMEM((1,), jnp.int32),
            pltpu.VMEM((ITEM_ROWS // SUBLANES, SUBLANES, D_MODEL), F32),
            pltpu.VMEM((ITEM_ROWS // SUBLANES, SUBLANES, D_MODEL), F32),
            pltpu.VMEM((ITEM_ROWS, D_MODEL), BF16),
            pltpu.VMEM((EXPERT_SLICES, ITEM_ROWS, EXPERT_TN), BF16),
            pltpu.VMEM((D_MODEL, EXPERT_TN), BF16),
            pltpu.VMEM((D_MODEL, EXPERT_TN), BF16),
            pltpu.VMEM((D_EXPERT, D_MODEL), BF16),
            pltpu.SemaphoreType.DMA((2,)),
            pltpu.SemaphoreType.DMA((ITEM_BLOCKS,)),
            pltpu.SemaphoreType.DMA((ITEM_BLOCKS,)),
        ],
    )
    return pl.pallas_call(
        _experts_kernel,
        out_shape=jax.ShapeDtypeStruct((PAIR_ROWS // SUBLANES, SUBLANES, D_MODEL), F32),
        grid_spec=grid_spec,
        compiler_params=pltpu.CompilerParams(
            dimension_semantics=("arbitrary", "arbitrary"), vmem_limit_bytes=VMEM_LIMIT),
        name="experts",
    )(item_e, item_blk0, item_nb, src_row, dst_row, h_all.reshape(N_ROWS // SUBLANES, SUBLANES, D_MODEL),
      norm_ffn, w_gate, w_up, w_down).reshape(PAIR_ROWS, D_MODEL)


def _dispatch_tables(ids, counts):
    flat_e = ids[:, 0:TOP_K].reshape(-1)
    rank = ids[:, TOP_K:2 * TOP_K].reshape(-1)
    slot = flat_e * EXPERT_SLOTS + rank
    slot_pair = jnp.full((SLOT_TABLE,), N_PAIRS, jnp.int32).at[slot].set(jnp.arange(N_PAIRS, dtype=jnp.int32))
    slot_id = jnp.arange(SLOT_TABLE, dtype=jnp.int32)
    src_row = jnp.minimum(slot_pair // TOP_K, N_ROWS - 1)
    dst_row = jnp.where(slot_pair < N_PAIRS, (slot_pair % TOP_K) * PAIR_REGION + slot_pair // TOP_K,
                        DUMP_ROW0 + slot_id % ROW_BLOCK)
    nblk = (counts + ROW_BLOCK - 1) // ROW_BLOCK
    nitem = (nblk + ITEM_BLOCKS - 1) // ITEM_BLOCKS
    upto = jnp.arange(N_EXPERTS)[None, :] <= jnp.arange(N_EXPERTS)[:, None]
    item_end = jnp.sum(jnp.where(upto, nitem[None, :], 0), axis=1)
    total = item_end[-1]
    idx = jnp.arange(MAX_ITEMS, dtype=jnp.int32)
    live = idx < total
    at = jnp.minimum(idx, total - 1)
    e_of = jnp.minimum(jnp.sum((item_end[None, :] <= at[:, None]).astype(jnp.int32), axis=1), N_EXPERTS - 1)
    local = jnp.minimum(idx, total - 1) - (item_end - nitem)[e_of]
    item_blk0 = (e_of * EXPERT_BLOCKS + local * ITEM_BLOCKS).astype(jnp.int32)
    item_nb = jnp.where(live, jnp.clip(nblk[e_of] - local * ITEM_BLOCKS, 0, ITEM_BLOCKS), 0).astype(jnp.int32)
    return e_of, item_blk0, item_nb, src_row, dst_row


def _final_rows(h, y0, y1, ew, p, wg_ref, wp_ref, np_ref, fn_ref):
    h = h + (ew[:, 0:1] * y0 + ew[:, 1:2] * y1)
    gate = _sigmoid(_bdot(_rms(h, np_ref[...]).astype(BF16), wg_ref[...]))
    h = h + gate * _bdot(p.astype(BF16), wp_ref[...])
    return _rms(h, fn_ref[...])


def _final_kernel(h_ref, y0_ref, y1_ref, ew_ref, p_ref, hs_ref, y0s_ref, y1s_ref, ews_ref, ps_ref,
                  wg_hbm, wp_hbm, np_ref, fn_ref, op_ref, os_ref, wg_ref, wp_ref, stage_ref, sem):
    i = pl.program_id(0)

    @pl.when(i == 0)
    def _():
        _load_weight_bf16(wg_hbm, wg_ref, stage_ref, sem)
        _load_weight_bf16(wp_hbm, wp_ref, stage_ref, sem)

    @pl.when(i < FIN_PROMPT_STEPS)
    def _():
        op_ref[...] = _final_rows(h_ref[...], y0_ref[...], y1_ref[...], ew_ref[...], p_ref[...],
                                  wg_ref, wp_ref, np_ref, fn_ref)

    @pl.when(i == FIN_PROMPT_STEPS)
    def _():
        os_ref[...] = _final_rows(hs_ref[...], y0s_ref[...], y1s_ref[...], ews_ref[...], ps_ref[...],
                                  wg_ref, wp_ref, np_ref, fn_ref)


def _final(h_all, ypairs, ew, p_prompt, p_sample, w_gate, w_proj, norm_ple, final_norm):
    last = FIN_PROMPT_STEPS - 1
    prow = lambda i: (jnp.minimum(i, last), 0)
    const2 = lambda i: (0, 0)
    hbm = pl.BlockSpec(memory_space=pltpu.MemorySpace.HBM)
    sblk = N_PROMPT // N_SAMPLE
    pact = pl.BlockSpec((FIN_TM, D_MODEL), prow)
    sact = lambda off: pl.BlockSpec((N_SAMPLE, D_MODEL), lambda i: (off + sblk, 0))
    return pl.pallas_call(
        _final_kernel,
        out_shape=(jax.ShapeDtypeStruct((N_PROMPT, D_MODEL), F32), jax.ShapeDtypeStruct((N_SAMPLE, D_MODEL), F32)),
        grid=(FIN_PROMPT_STEPS + 1,),
        in_specs=[pact, pact,
                  pl.BlockSpec((FIN_TM, D_MODEL), lambda i: (jnp.minimum(i, last) + PAIR_REGION // FIN_TM, 0)),
                  pl.BlockSpec((FIN_TM, LANES), prow),
                  pl.BlockSpec((FIN_TM, PLE_DIM), prow),
                  sact(0), sact(0), sact(PAIR_REGION // N_SAMPLE),
                  pl.BlockSpec((N_SAMPLE, LANES), lambda i: (sblk, 0)),
                  pl.BlockSpec((N_SAMPLE, PLE_DIM), const2),
                  hbm, hbm,
                  pl.BlockSpec((1, D_MODEL), const2),
                  pl.BlockSpec((1, D_MODEL), const2)],
        out_specs=(pact, pl.BlockSpec((N_SAMPLE, D_MODEL), const2)),
        scratch_shapes=[pltpu.VMEM((D_MODEL, D_MODEL), BF16), pltpu.VMEM((PLE_DIM, D_MODEL), BF16)] + _WEIGHT_STAGE,
        compiler_params=pltpu.CompilerParams(
            dimension_semantics=("arbitrary",), vmem_limit_bytes=VMEM_LIMIT),
        name="final",
    )(h_all, ypairs, ypairs, ew, p_prompt, h_all, ypairs, ypairs, ew, p_sample, w_gate, w_proj, norm_ple, final_norm)


def kernel(x_prompt, x_sample, state_conv, p_prompt, p_sample, norm_mix, w_in, ln_v_g, ln_v_b, w_spatial,
           b_spatial, w_proj_a, conv_w, conv_b, ln_c_g, ln_c_b, w_proj_b, w_out, norm_ffn, w_router_group,
           b_router_group, w_router_expert, b_router_expert, w_exp_gate, w_exp_up, w_exp_down, norm_ple,
           w_ple_gate, w_ple_proj, final_norm):
    assert w_in.shape[0] == 1, "single layer"
    vec = lambda v: v.reshape(1, -1)
    xp = x_prompt.reshape(N_PROMPT, D_MODEL)
    xs = x_sample.reshape(N_SAMPLE, D_MODEL)
    xn = _xnorm(xp, xs, vec(norm_mix[0]))

    w_s, b_s = w_spatial[0], b_spatial[0]
    wmix = jnp.tril(w_s).astype(BF16)
    bmix = jnp.repeat(b_s.T, A_GROUP_DIM, axis=1)
    w00 = jnp.repeat(w_s[:, 0, 0], A_GROUP_DIM).reshape(1, D_MODEL)
    b00 = jnp.repeat(b_s[:, 0], A_GROUP_DIM).reshape(1, D_MODEL)

    act = _in_proj(xn, w_in[0])

    cw = conv_w[0]
    new_state, conv_s = _sconv(jnp.transpose(state_conv[0], (1, 0, 2)), act, cw, vec(conv_b[0]))
    new_conv_sample = jnp.transpose(new_state, (1, 0, 2))[None]
    cw_pad = jnp.concatenate([cw, jnp.zeros((1, D_MODEL), F32)], axis=0)
    m_all, vn_s = _seqmix(act, conv_s, wmix, bmix, w00, b00, vec(ln_v_g[0]), vec(ln_v_b[0]), cw_pad,
                          vec(conv_b[0]), vec(ln_c_g[0]), vec(ln_c_b[0]), w_proj_a[0], w_proj_b[0])

    route_pad = LANES - MOE_GROUPS - N_EXPERTS
    w_route = jnp.concatenate([w_router_group[0], w_router_expert[0], jnp.zeros((D_MODEL, route_pad), F32)], axis=1)
    b_route = jnp.concatenate([b_router_group[0], b_router_expert[0], jnp.zeros((route_pad,), F32)]).reshape(1, LANES)

    h_all, ids, ew, counts = _mixout(m_all, xp, xs, w_out[0], vec(norm_ffn[0]), w_route.astype(BF16), b_route)

    tables = _dispatch_tables(ids[:, :2 * TOP_K], counts[0, :N_EXPERTS])
    ypairs = _experts(*tables, h_all, vec(norm_ffn[0]), w_exp_gate[0], w_exp_up[0], w_exp_down[0])

    y_prompt, y_sample = _final(h_all, ypairs, ew, p_prompt[0].reshape(N_PROMPT, PLE_DIM),
                                p_sample[0].reshape(N_SAMPLE, PLE_DIM), w_ple_gate[0], w_ple_proj[0],
                                vec(norm_ple[0]), vec(final_norm))
    y_prompt = y_prompt.reshape(N_PROMPT_SEQ, SEQ, D_MODEL)
    y_sample = y_sample.reshape(N_SAMPLE, 1, D_MODEL)
    glu_cols = slice(ACT_GLU * D_MODEL, (ACT_GLU + 1) * D_MODEL)
    new_conv_prompt = jnp.stack([act[(b + 1) * SEQ - CONV_HIST:(b + 1) * SEQ, glu_cols]
                                 for b in range(N_PROMPT_SEQ)])[None]
    return (y_prompt, y_sample, new_conv_prompt, new_conv_sample, vn_s.reshape(1, N_SAMPLE, 1, D_MODEL))
```

```python
import jax
import jax.numpy as jnp
from jax import lax
from jax.experimental import pallas as pl
from jax.experimental.pallas import tpu as pltpu

F32 = jnp.float32
BF16 = jnp.bfloat16

D_MODEL = 2048
N_PROMPT_SEQ = 4
SEQ = 2048
N_PROMPT = N_PROMPT_SEQ * SEQ
N_SAMPLE = 128
N_ROWS = N_PROMPT + N_SAMPLE
CHUNK = 128
N_CHUNKS = N_ROWS // CHUNK
CHUNKS_PER_SEQ = SEQ // CHUNK
A_GROUPS = 8
A_GROUP_DIM = D_MODEL // A_GROUPS
CONV_WIDTH = 31
CONV_HIST = CONV_WIDTH - 1
HIST_BLOCK = 32
MOE_GROUPS = 4
EXPERTS_PER_GROUP = 8
N_EXPERTS = MOE_GROUPS * EXPERTS_PER_GROUP
TOP_K = 2
D_EXPERT = D_MODEL // 2
PLE_DIM = 256
EPS = 1e-6
LANES = 128

XN_TM = 512
IN_TM = 1664
IN_TN = 512
IN_COL_TILES = D_MODEL // IN_TN
ACT_BLOCKS = 5
ACT_GU, ACT_GV, ACT_GLU, ACT_SGA, ACT_SGB = range(ACT_BLOCKS)
MIX_TM = 512
MIX_PROMPT_STEPS = N_PROMPT // MIX_TM
FIN_TM = 256
FIN_PROMPT_STEPS = N_PROMPT // FIN_TM
ROW_BLOCK = 128
SUBLANES = 8
BLOCK_TILES = ROW_BLOCK // SUBLANES
N_PAIRS = N_ROWS * TOP_K
MAX_BLOCKS = N_PAIRS // ROW_BLOCK + N_EXPERTS
ITEM_BLOCKS = 8
ITEM_ROWS = ITEM_BLOCKS * ROW_BLOCK
MAX_ITEMS = (MAX_BLOCKS + (ITEM_BLOCKS - 1) * N_EXPERTS) // ITEM_BLOCKS
EXPERT_BLOCKS = N_ROWS // ROW_BLOCK + ITEM_BLOCKS
EXPERT_SLOTS = EXPERT_BLOCKS * ROW_BLOCK
SLOT_TABLE = N_EXPERTS * EXPERT_SLOTS
EXPERT_SLICES = 4
EXPERT_TN = D_EXPERT // EXPERT_SLICES
PAIR_REGION = 8448
DUMP_ROW0 = PAIR_REGION + N_ROWS
PAIR_ROWS = DUMP_ROW0 + ROW_BLOCK

VMEM_LIMIT = 56 * 1024 * 1024


def _rms(x, g):
    return x * lax.rsqrt(jnp.mean(x * x, axis=-1, keepdims=True) + EPS) * g


def _layer_norm(x, g, b):
    mu = jnp.mean(x, axis=-1, keepdims=True)
    xc = x - mu
    return xc * lax.rsqrt(jnp.mean(xc * xc, axis=-1, keepdims=True) + EPS) * g + b


def _sigmoid(x):
    return 1.0 / (1.0 + jnp.exp(-x))


def _silu(x):
    return x * _sigmoid(x)


def _gelu(x):
    return jax.nn.gelu(x, approximate=True)


def _bdot(a, b):
    return jnp.dot(a, b, preferred_element_type=F32)


STAGE_ROWS = 256


def _load_weight_bf16(w_hbm, dst_ref, stage_ref, sem):
    n_chunks = w_hbm.shape[0] // STAGE_ROWS

    def chunk_copy(j):
        return pltpu.make_async_copy(w_hbm.at[pl.ds(j * STAGE_ROWS, STAGE_ROWS)], stage_ref.at[j % 2], sem.at[j % 2])

    chunk_copy(0).start()
    for j in range(n_chunks):
        if j + 1 < n_chunks:
            chunk_copy(j + 1).start()
        chunk_copy(j).wait()
        dst_ref[pl.ds(j * STAGE_ROWS, STAGE_ROWS), :] = stage_ref[j % 2].astype(BF16)


_WEIGHT_STAGE = [pltpu.VMEM((2, STAGE_ROWS, D_MODEL), F32), pltpu.SemaphoreType.DMA((2,))]


def _xnorm_kernel(xp_ref, xs_ref, nm_ref, xn_ref):
    i = pl.program_id(0)

    @pl.when(i < N_PROMPT // XN_TM)
    def _():
        xn_ref[...] = _rms(xp_ref[...], nm_ref[...]).astype(BF16)

    @pl.when(i == N_PROMPT // XN_TM)
    def _():
        xn_ref[0:N_SAMPLE, :] = _rms(xs_ref[...], nm_ref[...]).astype(BF16)
        xn_ref[N_SAMPLE:, :] = jnp.zeros((XN_TM - N_SAMPLE, D_MODEL), BF16)


def _xnorm(x_prompt, x_sample, norm_mix):
    last = N_PROMPT // XN_TM - 1
    return pl.pallas_call(
        _xnorm_kernel,
        out_shape=jax.ShapeDtypeStruct((N_ROWS, D_MODEL), BF16),
        grid=(N_PROMPT // XN_TM + 1,),
        in_specs=[pl.BlockSpec((XN_TM, D_MODEL), lambda i: (jnp.minimum(i, last), 0)),
                  pl.BlockSpec((N_SAMPLE, D_MODEL), lambda i: (0, 0)),
                  pl.BlockSpec((1, D_MODEL), lambda i: (0, 0))],
        out_specs=pl.BlockSpec((XN_TM, D_MODEL), lambda i: (i, 0)),
        name="xnorm",
    )(x_prompt, x_sample, norm_mix)


def _in_w_col(j):
    t = IN_COL_TILES
    jj = j - 2 * t
    glu_col = 2 * t + (jj % 2) * t + jj // 2
    return jnp.where((j >= 2 * t) & (j < 4 * t), glu_col, j)


def _in_out_col(j):
    t = IN_COL_TILES
    return jnp.where(j < 2 * t, j, jnp.where(j < 4 * t, 2 * t + (j - 2 * t) // 2, j - t))


def _in_proj_kernel(xn_ref, w_ref, o_ref, val_ref):
    j = pl.program_id(1)
    t = IN_COL_TILES

    def z():
        return _bdot(xn_ref[...], w_ref[...].astype(BF16))

    @pl.when(j < 2 * t)
    def _():
        o_ref[...] = _gelu(z())

    @pl.when((j >= 2 * t) & (j < 4 * t) & (j % 2 == 0))
    def _():
        val_ref[...] = z()

    @pl.when((j >= 2 * t) & (j < 4 * t) & (j % 2 == 1))
    def _():
        o_ref[...] = val_ref[...] * _sigmoid(z())

    @pl.when(j >= 4 * t)
    def _():
        o_ref[...] = _sigmoid(z())


def _in_proj(xn, w_in):
    return pl.pallas_call(
        _in_proj_kernel,
        out_shape=jax.ShapeDtypeStruct((N_ROWS, ACT_BLOCKS * D_MODEL), F32),
        grid=(N_ROWS // IN_TM, 6 * IN_COL_TILES),
        in_specs=[
            pl.BlockSpec((IN_TM, D_MODEL), lambda i, j: (i, 0)),
            pl.BlockSpec((D_MODEL, IN_TN), lambda i, j: (0, _in_w_col(j))),
        ],
        out_specs=pl.BlockSpec((IN_TM, IN_TN), lambda i, j: (i, _in_out_col(j))),
        scratch_shapes=[pltpu.VMEM((IN_TM, IN_TN), F32)],
        compiler_params=pltpu.CompilerParams(
            dimension_semantics=("arbitrary", "arbitrary"), vmem_limit_bytes=VMEM_LIMIT),
        name="in_proj",
    )(xn, w_in)


SCONV_B = 16


def _sconv_kernel(st_ref, glu_ref, w_ref, wl_ref, cb_ref, new_ref, conv_ref):
    glu = glu_ref[...]
    new_ref[0:CONV_HIST - 1] = st_ref[1:CONV_HIST]
    new_ref[CONV_HIST - 1] = glu
    acc = st_ref[0] * w_ref[0:1, :]
    for k in range(1, CONV_HIST):
        acc = acc + st_ref[k] * w_ref[k:k + 1, :]
    conv_ref[...] = acc + glu * wl_ref[...] + cb_ref[...]


def _sconv(state, act, conv_w, conv_b):
    return pl.pallas_call(
        _sconv_kernel,
        out_shape=(
            jax.ShapeDtypeStruct((CONV_HIST, N_SAMPLE, D_MODEL), F32),
            jax.ShapeDtypeStruct((N_SAMPLE, D_MODEL), F32),
        ),
        grid=(N_SAMPLE // SCONV_B,),
        in_specs=[
            pl.BlockSpec((CONV_HIST, SCONV_B, D_MODEL), lambda i: (0, i, 0)),
            pl.BlockSpec((SCONV_B, D_MODEL), lambda i: (N_PROMPT // SCONV_B + i, ACT_GLU)),
            pl.BlockSpec((CONV_HIST, D_MODEL), lambda i: (0, 0)),
            pl.BlockSpec((1, D_MODEL), lambda i: (0, 0)),
            pl.BlockSpec((1, D_MODEL), lambda i: (0, 0)),
        ],
        out_specs=(
            pl.BlockSpec((CONV_HIST, SCONV_B, D_MODEL), lambda i: (0, i, 0)),
            pl.BlockSpec((SCONV_B, D_MODEL), lambda i: (i, 0)),
        ),
        name="sconv",
    )(state, act, conv_w[:CONV_HIST], conv_w[CONV_HIST:], conv_b)


CONV_STRIP = 256
WIN_ROWS = HIST_BLOCK + CHUNK


def _seqmix_kernel(gv_ref, gu_ref, cur_ref, prev_ref, sga_ref, sgb_ref, convs_ref, wmix_ref, bmix_ref, w00_ref,
                   b00_ref, lvg_ref, lvb_ref, cw_ref, cb_ref, lcg_ref, lcb_ref, wa_hbm, wb_hbm,
                   m_ref, vns_ref, win_ref, conv_ref, a_ref, b_ref, wa_ref, wb_ref, stage_ref, sem):
    c = pl.program_id(0)
    is_sample = c == N_CHUNKS - 1

    @pl.when(c == 0)
    def _():
        _load_weight_bf16(wa_hbm, wa_ref, stage_ref, sem)
        _load_weight_bf16(wb_hbm, wb_ref, stage_ref, sem)

    fresh = (c % CHUNKS_PER_SEQ) == 0
    win_ref[0:HIST_BLOCK, :] = jnp.where(fresh, 0.0, prev_ref[...]).astype(BF16).astype(F32)
    win_ref[HIST_BLOCK:WIN_ROWS, :] = cur_ref[...].astype(BF16).astype(F32)
    off = HIST_BLOCK - CONV_HIST

    vn = _layer_norm(gv_ref[...], lvg_ref[...], lvb_ref[...])
    vns_ref[...] = vn

    vnb = vn.astype(BF16)
    for g in range(A_GROUPS):
        sl = slice(g * A_GROUP_DIM, (g + 1) * A_GROUP_DIM)
        s_chunk = _bdot(wmix_ref[g], vnb[:, sl]) + bmix_ref[:, sl]
        s_first = vn[:, sl] * w00_ref[:, sl] + b00_ref[:, sl]
        a_ref[:, sl] = (gu_ref[:, sl] * jnp.where(is_sample, s_first, s_chunk)).astype(BF16)
    pa = _bdot(a_ref[...], wa_ref[...])

    def phase_sum(r, cs):
        rows = CHUNK + (SUBLANES if r else 0)
        total = None
        for k in range(CONV_WIDTH):
            q, rk = divmod(off + k, SUBLANES)
            if rk == r:
                term = win_ref[SUBLANES * q:SUBLANES * q + rows, cs] * cw_ref[k:k + 1, cs].astype(BF16).astype(F32)
                total = term if total is None else total + term
        return total[r:r + CHUNK, :]

    for s0 in range(0, D_MODEL, CONV_STRIP):
        cs = slice(s0, s0 + CONV_STRIP)
        acc = phase_sum(0, cs)
        for r in range(1, SUBLANES):
            acc = acc + phase_sum(r, cs)
        conv_ref[:, cs] = acc + cb_ref[:, cs]
    conv = jnp.where(is_sample, convs_ref[...], conv_ref[...])
    b_ref[...] = _silu(_layer_norm(conv, lcg_ref[...], lcb_ref[...])).astype(BF16)

    pb = _bdot(b_ref[...], wb_ref[...])
    m_ref[...] = (sga_ref[...] * pa + sgb_ref[...] * pb).astype(BF16)


def _seqmix(act, conv_s, wmix, bmix, w00, b00, ln_v_g, ln_v_b, conv_w, conv_b, ln_c_g, ln_c_b, wa, wb):
    row = lambda c: (c, 0)
    const2 = lambda c: (0, 0)
    vec = pl.BlockSpec((1, D_MODEL), const2)
    hbm = pl.BlockSpec(memory_space=pltpu.MemorySpace.HBM)
    hist_per_chunk = CHUNK // HIST_BLOCK
    act_block = lambda blk: pl.BlockSpec((CHUNK, D_MODEL), lambda c: (c, blk))
    return pl.pallas_call(
        _seqmix_kernel,
        out_shape=(
            jax.ShapeDtypeStruct((N_ROWS, D_MODEL), BF16),
            jax.ShapeDtypeStruct((N_SAMPLE, D_MODEL), F32),
        ),
        grid=(N_CHUNKS,),
        in_specs=[
            act_block(ACT_GV), act_block(ACT_GU), act_block(ACT_GLU),
            pl.BlockSpec((HIST_BLOCK, D_MODEL), lambda c: (jnp.maximum(c * hist_per_chunk - 1, 0), ACT_GLU)),
            act_block(ACT_SGA), act_block(ACT_SGB),
            pl.BlockSpec((N_SAMPLE, D_MODEL), const2),
            pl.BlockSpec((A_GROUPS, CHUNK, CHUNK), lambda c: (0, 0, 0)),
            pl.BlockSpec((CHUNK, D_MODEL), const2),
            vec, vec, vec, vec,
            pl.BlockSpec((CONV_WIDTH + 1, D_MODEL), const2),
            vec, vec, vec,
            hbm, hbm,
        ],
        out_specs=(
            pl.BlockSpec((CHUNK, D_MODEL), row),
            pl.BlockSpec((N_SAMPLE, D_MODEL), const2),
        ),
        scratch_shapes=[pltpu.VMEM((WIN_ROWS, D_MODEL), F32),
                        pltpu.VMEM((CHUNK, D_MODEL), F32),
                        pltpu.VMEM((CHUNK, D_MODEL), BF16), pltpu.VMEM((CHUNK, D_MODEL), BF16),
                        pltpu.VMEM((D_MODEL, D_MODEL), BF16), pltpu.VMEM((D_MODEL, D_MODEL), BF16)] + _WEIGHT_STAGE,
        compiler_params=pltpu.CompilerParams(
            dimension_semantics=("arbitrary",), vmem_limit_bytes=VMEM_LIMIT),
        name="seqmix",
    )(act, act, act, act, act, act, conv_s, wmix, bmix, w00, b00, ln_v_g, ln_v_b, conv_w, conv_b, ln_c_g, ln_c_b,
      wa, wb)


def _route(logits):
    col = lax.broadcasted_iota(jnp.int32, logits.shape, 1).astype(F32)
    neg = jnp.float32(-jnp.inf)
    big = jnp.float32(1e9)
    lg = jnp.where(col < MOE_GROUPS, logits, neg)
    gmax = jnp.max(lg, axis=-1, keepdims=True)
    gi = jnp.min(jnp.where(lg == gmax, col, big), axis=-1, keepdims=True)
    gw = 1.0 / jnp.sum(jnp.exp(lg - gmax), axis=-1, keepdims=True)
    lo = MOE_GROUPS + gi * EXPERTS_PER_GROUP
    le = jnp.where((col >= lo) & (col < lo + EXPERTS_PER_GROUP), logits, neg)
    m1 = jnp.max(le, axis=-1, keepdims=True)
    i1 = jnp.min(jnp.where(le == m1, col, big), axis=-1, keepdims=True)
    le2 = jnp.where(col == i1, neg, le)
    m2 = jnp.max(le2, axis=-1, keepdims=True)
    i2 = jnp.min(jnp.where(le2 == m2, col, big), axis=-1, keepdims=True)
    e = jnp.exp(m2 - m1)
    w1 = gw / (1.0 + e)
    w2 = gw * e / (1.0 + e)
    eid = jnp.where(col == 0, i1 - MOE_GROUPS, jnp.where(col == 1, i2 - MOE_GROUPS, 0.0)).astype(jnp.int32)
    ew = jnp.where(col == 0, w1, jnp.where(col == 1, w2, 0.0))
    return eid, ew


def _mixout_rows(m, x, wo_ref, nf_ref, wr_ref, br_ref, tril, count_ref):
    h = x + _bdot(m, wo_ref[...])
    hn = _rms(h, nf_ref[...])
    logits = _bdot(hn.astype(BF16), wr_ref[...]) + br_ref[...]
    eid, ew = _route(logits)
    col = lax.broadcasted_iota(jnp.int32, eid.shape, 1)
    first = col == eid[:, 0:1]
    second = col == eid[:, 1:2]
    hits = first.astype(F32) + second.astype(F32)
    before = _bdot(tril, hits.astype(BF16)) + count_ref[...]
    rank0 = jnp.sum(jnp.where(first, before, 0.0), axis=-1, keepdims=True)
    rank1 = jnp.sum(jnp.where(second, before + first.astype(F32), 0.0), axis=-1, keepdims=True)
    count_ref[...] = count_ref[...] + jnp.sum(hits, axis=0, keepdims=True)
    ids = jnp.where(col == 2, rank0.astype(jnp.int32), jnp.where(col == 3, rank1.astype(jnp.int32), eid))
    return h, ids, ew


def _mixout_kernel(mp_ref, xp_ref, ms_ref, xs_ref, wo_hbm, nf_ref, wr_ref, br_ref, tril_ref,
                   h_ref, eid_ref, ew_ref, cnt_ref, wo_ref, count_ref, stage_ref, sem):
    i = pl.program_id(0)

    @pl.when(i == 0)
    def _():
        _load_weight_bf16(wo_hbm, wo_ref, stage_ref, sem)
        count_ref[...] = jnp.zeros((1, LANES), F32)

    @pl.when(i < MIX_PROMPT_STEPS)
    def _():
        h, eid, ew = _mixout_rows(mp_ref[...], xp_ref[...], wo_ref, nf_ref, wr_ref, br_ref, tril_ref[...], count_ref)
        h_ref[...] = h
        eid_ref[...] = eid
        ew_ref[...] = ew

    @pl.when(i == MIX_PROMPT_STEPS)
    def _():
        h, eid, ew = _mixout_rows(ms_ref[...], xs_ref[...], wo_ref, nf_ref, wr_ref, br_ref,
                                  tril_ref[0:N_SAMPLE, 0:N_SAMPLE], count_ref)
        for ref, val in ((h_ref, h), (eid_ref, eid), (ew_ref, ew)):
            ref[0:N_SAMPLE, :] = val
            ref[N_SAMPLE:, :] = jnp.zeros((MIX_TM - N_SAMPLE, val.shape[1]), val.dtype)

    cnt_ref[...] = jnp.broadcast_to(count_ref[...], cnt_ref.shape).astype(jnp.int32)


def _mixout(m_all, x_prompt, x_sample, wo, norm_ffn, w_route, b_route):
    last = MIX_PROMPT_STEPS - 1
    row = lambda i: (i, 0)
    prow = lambda i: (jnp.minimum(i, last), 0)
    const2 = lambda i: (0, 0)
    hbm = pl.BlockSpec(memory_space=pltpu.MemorySpace.HBM)
    tile_row = jnp.arange(MIX_TM, dtype=jnp.int32)
    tril = (tile_row[None, :] < tile_row[:, None]).astype(BF16)
    return pl.pallas_call(
        _mixout_kernel,
        out_shape=(
            jax.ShapeDtypeStruct((N_ROWS, D_MODEL), F32),
            jax.ShapeDtypeStruct((N_ROWS, LANES), jnp.int32),
            jax.ShapeDtypeStruct((N_ROWS, LANES), F32),
            jax.ShapeDtypeStruct((SUBLANES, LANES), jnp.int32),
        ),
        grid=(MIX_PROMPT_STEPS + 1,),
        in_specs=[pl.BlockSpec((MIX_TM, D_MODEL), prow), pl.BlockSpec((MIX_TM, D_MODEL), prow),
                  pl.BlockSpec((N_SAMPLE, D_MODEL), lambda i: (N_PROMPT // N_SAMPLE, 0)),
                  pl.BlockSpec((N_SAMPLE, D_MODEL), const2),
                  hbm,
                  pl.BlockSpec((1, D_MODEL), const2),
                  pl.BlockSpec((D_MODEL, LANES), const2),
                  pl.BlockSpec((1, LANES), const2),
                  pl.BlockSpec((MIX_TM, MIX_TM), const2)],
        out_specs=(pl.BlockSpec((MIX_TM, D_MODEL), row), pl.BlockSpec((MIX_TM, LANES), row),
                   pl.BlockSpec((MIX_TM, LANES), row), pl.BlockSpec((SUBLANES, LANES), const2)),
        scratch_shapes=[pltpu.VMEM((D_MODEL, D_MODEL), BF16), pltpu.VMEM((1, LANES), F32)] + _WEIGHT_STAGE,
        compiler_params=pltpu.CompilerParams(
            dimension_semantics=("arbitrary",), vmem_limit_bytes=VMEM_LIMIT),
        name="mixout",
    )(m_all, x_prompt, m_all, x_sample, wo, norm_ffn, w_route, b_route, tril)


def _experts_kernel(item_e, item_blk0, item_nb, src_hbm, dst_hbm, h_hbm, nf_ref, wg_ref, wu_ref, wd_ref,
                    y_hbm, src_s, dst_s, pend_s, xg_ref, yst_ref, xs_ref, hm_ref, wgb_ref, wub_ref, wdb_ref,
                    idx_sem, g_sem, s_sem):
    w = pl.program_id(0)
    s = pl.program_id(1)
    nb = item_nb[w]
    nxt = jnp.minimum(w + 1, MAX_ITEMS - 1)
    nb_next = jnp.where(w + 1 < MAX_ITEMS, item_nb[nxt], 0)

    def table_fetch(tbl_hbm, tbl_s, sem_idx, item):
        slot0 = pl.multiple_of(item_blk0[item] * ROW_BLOCK, ROW_BLOCK)
        return pltpu.make_async_copy(tbl_hbm.at[pl.ds(slot0, ITEM_ROWS)], tbl_s, idx_sem.at[sem_idx])

    def row_copy_in(t, u, blk):
        src = src_s[t * SUBLANES + u]
        return pltpu.make_async_copy(h_hbm.at[src >> 3, pl.ds(src & (SUBLANES - 1), 1)],
                                     xg_ref.at[t, pl.ds(u, 1)], g_sem.at[blk])

    def row_copy_out(t, u, blk):
        dst = dst_s[t * SUBLANES + u]
        return pltpu.make_async_copy(yst_ref.at[t, pl.ds(u, 1)],
                                     y_hbm.at[dst >> 3, pl.ds(dst & (SUBLANES - 1), 1)], s_sem.at[blk])

    def block_rows(blk):
        return pl.ds(pl.multiple_of(blk * ROW_BLOCK, ROW_BLOCK), ROW_BLOCK)

    def block_tiles(blk):
        return pl.ds(pl.multiple_of(blk * BLOCK_TILES, BLOCK_TILES), BLOCK_TILES)

    def block_copy_wait(sem, blk):
        pltpu.make_async_copy(xg_ref.at[block_tiles(blk)], xg_ref.at[block_tiles(blk)], sem.at[blk]).wait()

    def for_each_row(n_blocks, row_copy):
        def tile(t, carry):
            for u in range(SUBLANES):
                row_copy(t, u, lax.shift_right_logical(t, BLOCK_TILES.bit_length() - 1)).start()
            return carry
        lax.fori_loop(0, n_blocks * BLOCK_TILES, tile, 0)

    def issue_gather(n_blocks):
        for_each_row(n_blocks, row_copy_in)

    def drain_scatter():
        def drain(blk, carry):
            block_copy_wait(s_sem, blk)
            return carry
        lax.fori_loop(0, pend_s[0], drain, 0)
        pend_s[0] = 0

    @pl.when((w == 0) & (s == 0))
    def _():
        pad_tiles = pl.ds(0, BLOCK_TILES)
        xg_ref[pad_tiles] = jnp.zeros((BLOCK_TILES, SUBLANES, D_MODEL), F32)
        for row0 in (N_ROWS, DUMP_ROW0):
            cz = pltpu.make_async_copy(xg_ref.at[pad_tiles], y_hbm.at[pl.ds(row0 // SUBLANES, BLOCK_TILES)],
                                       idx_sem.at[0])
            cz.start()
            cz.wait()
        pend_s[0] = 0
        ci = table_fetch(src_hbm, src_s, 0, w)
        ci.start()
        ci.wait()
        issue_gather(nb)

    @pl.when((s == 0) & (nb > 0))
    def _():
        table_fetch(dst_hbm, dst_s, 1, w).start()

        def land(blk, carry):
            block_copy_wait(g_sem, blk)
            x = xg_ref[block_tiles(blk)].reshape(ROW_BLOCK, D_MODEL)
            xs_ref[block_rows(blk), :] = _rms(x, nf_ref[...]).astype(BF16)
            return carry
        lax.fori_loop(0, nb, land, 0)

    @pl.when((s == 1) & (nb_next > 0))
    def _():
        table_fetch(src_hbm, src_s, 0, nxt).start()

    @pl.when((s == 2) & (nb_next > 0))
    def _():
        table_fetch(src_hbm, src_s, 0, nxt).wait()
        issue_gather(nb_next)

    def over_row_blocks(fn):
        def four_blocks(j, carry):
            fn(pl.multiple_of(j * (4 * ROW_BLOCK), 4 * ROW_BLOCK), 4 * ROW_BLOCK)
            return carry
        lax.fori_loop(0, nb >> 2, four_blocks, 0)

        @pl.when((nb & 2) == 2)
        def _():
            fn(pl.multiple_of((nb >> 2) * (4 * ROW_BLOCK), 2 * ROW_BLOCK), 2 * ROW_BLOCK)

        @pl.when((nb & 1) == 1)
        def _():
            fn(pl.multiple_of((nb - 1) * ROW_BLOCK, ROW_BLOCK), ROW_BLOCK)

    @pl.when(nb > 0)
    def _():
        wgb_ref[...] = wg_ref[...].astype(BF16)
        wub_ref[...] = wu_ref[...].astype(BF16)
        wdb_ref[pl.ds(pl.multiple_of(s * EXPERT_TN, EXPERT_TN), EXPERT_TN), :] = wd_ref[...].astype(BF16)

        def gate_up(row0, nrows):
            rows = pl.ds(row0, nrows)
            x = xs_ref[rows, :]
            hm_ref[s, rows, :] = (_silu(_bdot(x, wgb_ref[...])) * _bdot(x, wub_ref[...])).astype(BF16)
        over_row_blocks(gate_up)

    @pl.when((s == EXPERT_SLICES - 1) & (nb > 0))
    def _():
        drain_scatter()

        def down(row0, nrows):
            rows = pl.ds(row0, nrows)
            hm = jnp.concatenate([hm_ref[k, rows, :] for k in range(EXPERT_SLICES)], axis=1)
            tiles = pl.ds(pl.multiple_of(row0 // SUBLANES, BLOCK_TILES), nrows // SUBLANES)
            yst_ref[tiles] = _bdot(hm, wdb_ref[...]).reshape(nrows // SUBLANES, SUBLANES, D_MODEL)
        over_row_blocks(down)

        table_fetch(dst_hbm, dst_s, 1, w).wait()

        for_each_row(nb, row_copy_out)
        pend_s[0] = nb

    @pl.when((w == pl.num_programs(0) - 1) & (s == EXPERT_SLICES - 1))
    def _():
        drain_scatter()


def _experts(n_items, item_e, item_blk0, item_nb, src_row, dst_row, h_all, norm_ffn, w_gate, w_up, w_down):
    hbm = pl.BlockSpec(memory_space=pltpu.MemorySpace.HBM)

    def w_slice(w, s, inb):
        return jnp.where(inb[w] > 0, s, EXPERT_SLICES - 1)

    grid_spec = pltpu.PrefetchScalarGridSpec(
        num_scalar_prefetch=3,
        grid=(n_items, EXPERT_SLICES),
        in_specs=[
            hbm, hbm, hbm,
            pl.BlockSpec((1, D_MODEL), lambda w, s, ie, ib, inb: (0, 0)),
            pl.BlockSpec((None, D_MODEL, EXPERT_TN), lambda w, s, ie, ib, inb: (ie[w], 0, w_slice(w, s, inb))),
            pl.BlockSpec((None, D_MODEL, EXPERT_TN), lambda w, s, ie, ib, inb: (ie[w], 0, w_slice(w, s, inb))),
            pl.BlockSpec((None, EXPERT_TN, D_MODEL), lambda w, s, ie, ib, inb: (ie[w], w_slice(w, s, inb), 0)),
        ],
        out_specs=hbm,
        scratch_shapes=[
            pltpu.SMEM((ITEM_ROWS,), jnp.int32),
            pltpu.SMEM((ITEM_ROWS,), jnp.int32),
            pltpu.SMEM((1,), jnp.int32),
            pltpu.VMEM((ITEM_ROWS // SUBLANES, SUBLANES, D_MODEL), F32),
            pltpu.VMEM((ITEM_ROWS // SUBLANES, SUBLANES, D_MODEL), F32),
            pltpu.VMEM((ITEM_ROWS, D_MODEL), BF16),
            pltpu.VMEM((EXPERT_SLICES, ITEM_ROWS, EXPERT_TN), BF16),
            pltpu.VMEM((D_MODEL, EXPERT_TN), BF16),
            pltpu.VMEM((D_MODEL, EXPERT_TN), BF16),
            pltpu.VMEM((D_EXPERT, D_MODEL), BF16),
            pltpu.SemaphoreType.DMA((2,)),
            pltpu.SemaphoreType.DMA((ITEM_BLOCKS,)),
            pltpu.SemaphoreType.DMA((ITEM_BLOCKS,)),
        ],
    )
    return pl.pallas_call(
        _experts_kernel,
        out_shape=jax.ShapeDtypeStruct((PAIR_ROWS // SUBLANES, SUBLANES, D_MODEL), F32),
        grid_spec=grid_spec,
        compiler_params=pltpu.CompilerParams(
            dimension_semantics=("arbitrary", "arbitrary"), vmem_limit_bytes=VMEM_LIMIT),
        name="experts",
    )(item_e, item_blk0, item_nb, src_row, dst_row, h_all.reshape(N_ROWS // SUBLANES, SUBLANES, D_MODEL),
      norm_ffn, w_gate, w_up, w_down).reshape(PAIR_ROWS, D_MODEL)


def _dispatch_tables(ids, counts):
    flat_e = ids[:, 0:TOP_K].reshape(-1)
    rank = ids[:, TOP_K:2 * TOP_K].reshape(-1)
    slot = flat_e * EXPERT_SLOTS + rank
    slot_pair = jnp.full((SLOT_TABLE,), N_PAIRS, jnp.int32).at[slot].set(jnp.arange(N_PAIRS, dtype=jnp.int32))
    slot_id = jnp.arange(SLOT_TABLE, dtype=jnp.int32)
    src_row = jnp.minimum(slot_pair // TOP_K, N_ROWS - 1)
    dst_row = jnp.where(slot_pair < N_PAIRS, (slot_pair % TOP_K) * PAIR_REGION + slot_pair // TOP_K,
                        DUMP_ROW0 + slot_id % ROW_BLOCK)
    nblk = (counts + ROW_BLOCK - 1) // ROW_BLOCK
    nitem = (nblk + ITEM_BLOCKS - 1) // ITEM_BLOCKS
    upto = jnp.arange(N_EXPERTS)[None, :] <= jnp.arange(N_EXPERTS)[:, None]
    item_end = jnp.sum(jnp.where(upto, nitem[None, :], 0), axis=1)
    total = item_end[-1]
    idx = jnp.arange(MAX_ITEMS, dtype=jnp.int32)
    live = idx < total
    at = jnp.minimum(idx, total - 1)
    e_of = jnp.minimum(jnp.sum((item_end[None, :] <= at[:, None]).astype(jnp.int32), axis=1), N_EXPERTS - 1)
    local = jnp.minimum(idx, total - 1) - (item_end - nitem)[e_of]
    item_blk0 = (e_of * EXPERT_BLOCKS + local * ITEM_BLOCKS).astype(jnp.int32)
    item_nb = jnp.where(live, jnp.clip(nblk[e_of] - local * ITEM_BLOCKS, 0, ITEM_BLOCKS), 0).astype(jnp.int32)
    return total.astype(jnp.int32), e_of, item_blk0, item_nb, src_row, dst_row


def _final_rows(h, y0, y1, ew, p, wg_ref, wp_ref, np_ref, fn_ref):
    h = h + (ew[:, 0:1] * y0 + ew[:, 1:2] * y1)
    gate = _sigmoid(_bdot(_rms(h, np_ref[...]).astype(BF16), wg_ref[...]))
    h = h + gate * _bdot(p.astype(BF16), wp_ref[...])
    return _rms(h, fn_ref[...])


def _final_kernel(h_ref, y0_ref, y1_ref, ew_ref, p_ref, hs_ref, y0s_ref, y1s_ref, ews_ref, ps_ref,
                  wg_hbm, wp_hbm, np_ref, fn_ref, op_ref, os_ref, wg_ref, wp_ref, stage_ref, sem):
    i = pl.program_id(0)

    @pl.when(i == 0)
    def _():
        _load_weight_bf16(wg_hbm, wg_ref, stage_ref, sem)
        _load_weight_bf16(wp_hbm, wp_ref, stage_ref, sem)

    @pl.when(i < FIN_PROMPT_STEPS)
    def _():
        op_ref[...] = _final_rows(h_ref[...], y0_ref[...], y1_ref[...], ew_ref[...], p_ref[...],
                                  wg_ref, wp_ref, np_ref, fn_ref)

    @pl.when(i == FIN_PROMPT_STEPS)
    def _():
        os_ref[...] = _final_rows(hs_ref[...], y0s_ref[...], y1s_ref[...], ews_ref[...], ps_ref[...],
                                  wg_ref, wp_ref, np_ref, fn_ref)


def _final(h_all, ypairs, ew, p_prompt, p_sample, w_gate, w_proj, norm_ple, final_norm):
    last = FIN_PROMPT_STEPS - 1
    prow = lambda i: (jnp.minimum(i, last), 0)
    const2 = lambda i: (0, 0)
    hbm = pl.BlockSpec(memory_space=pltpu.MemorySpace.HBM)
    sblk = N_PROMPT // N_SAMPLE
    pact = pl.BlockSpec((FIN_TM, D_MODEL), prow)
    sact = lambda off: pl.BlockSpec((N_SAMPLE, D_MODEL), lambda i: (off + sblk, 0))
    return pl.pallas_call(
        _final_kernel,
        out_shape=(jax.ShapeDtypeStruct((N_PROMPT, D_MODEL), F32), jax.ShapeDtypeStruct((N_SAMPLE, D_MODEL), F32)),
        grid=(FIN_PROMPT_STEPS + 1,),
        in_specs=[pact, pact,
                  pl.BlockSpec((FIN_TM, D_MODEL), lambda i: (jnp.minimum(i, last) + PAIR_REGION // FIN_TM, 0)),
                  pl.BlockSpec((FIN_TM, LANES), prow),
                  pl.BlockSpec((FIN_TM, PLE_DIM), prow),
                  sact(0), sact(0), sact(PAIR_REGION // N_SAMPLE),
                  pl.BlockSpec((N_SAMPLE, LANES), lambda i: (sblk, 0)),
                  pl.BlockSpec((N_SAMPLE, PLE_DIM), const2),
                  hbm, hbm,
                  pl.BlockSpec((1, D_MODEL), const2),
                  pl.BlockSpec((1, D_MODEL), const2)],
        out_specs=(pact, pl.BlockSpec((N_SAMPLE, D_MODEL), const2)),
        scratch_shapes=[pltpu.VMEM((D_MODEL, D_MODEL), BF16), pltpu.VMEM((PLE_DIM, D_MODEL), BF16)] + _WEIGHT_STAGE,
        compiler_params=pltpu.CompilerParams(
            dimension_semantics=("arbitrary",), vmem_limit_bytes=VMEM_LIMIT),
        name="final",
    )(h_all, ypairs, ypairs, ew, p_prompt, h_all, ypairs, ypairs, ew, p_sample, w_gate, w_proj, norm_ple, final_norm)


def kernel(x_prompt, x_sample, state_conv, p_prompt, p_sample, norm_mix, w_in, ln_v_g, ln_v_b, w_spatial,
           b_spatial, w_proj_a, conv_w, conv_b, ln_c_g, ln_c_b, w_proj_b, w_out, norm_ffn, w_router_group,
           b_router_group, w_router_expert, b_router_expert, w_exp_gate, w_exp_up, w_exp_down, norm_ple,
           w_ple_gate, w_ple_proj, final_norm):
    assert w_in.shape[0] == 1, "single layer"
    vec = lambda v: v.reshape(1, -1)
    xp = x_prompt.reshape(N_PROMPT, D_MODEL)
    xs = x_sample.reshape(N_SAMPLE, D_MODEL)
    xn = _xnorm(xp, xs, vec(norm_mix[0]))

    w_s, b_s = w_spatial[0], b_spatial[0]
    wmix = jnp.tril(w_s).astype(BF16)
    bmix = jnp.repeat(b_s.T, A_GROUP_DIM, axis=1)
    w00 = jnp.repeat(w_s[:, 0, 0], A_GROUP_DIM).reshape(1, D_MODEL)
    b00 = jnp.repeat(b_s[:, 0], A_GROUP_DIM).reshape(1, D_MODEL)

    act = _in_proj(xn, w_in[0])

    cw = conv_w[0]
    new_state, conv_s = _sconv(jnp.transpose(state_conv[0], (1, 0, 2)), act, cw, vec(conv_b[0]))
    new_conv_sample = jnp.transpose(new_state, (1, 0, 2))[None]
    cw_pad = jnp.concatenate([cw, jnp.zeros((1, D_MODEL), F32)], axis=0)
    m_all, vn_s = _seqmix(act, conv_s, wmix, bmix, w00, b00, vec(ln_v_g[0]), vec(ln_v_b[0]), cw_pad,
                          vec(conv_b[0]), vec(ln_c_g[0]), vec(ln_c_b[0]), w_proj_a[0], w_proj_b[0])

    route_pad = LANES - MOE_GROUPS - N_EXPERTS
    w_route = jnp.concatenate([w_router_group[0], w_router_expert[0], jnp.zeros((D_MODEL, route_pad), F32)], axis=1)
    b_route = jnp.concatenate([b_router_group[0], b_router_expert[0], jnp.zeros((route_pad,), F32)]).reshape(1, LANES)

    h_all, ids, ew, counts = _mixout(m_all, xp, xs, w_out[0], vec(norm_ffn[0]), w_route.astype(BF16), b_route)

    tables = _dispatch_tables(ids[:, :2 * TOP_K], counts[0, :N_EXPERTS])
    ypairs = _experts(*tables, h_all, vec(norm_ffn[0]), w_exp_gate[0], w_exp_up[0], w_exp_down[0])

    y_prompt, y_sample = _final(h_all, ypairs, ew, p_prompt[0].reshape(N_PROMPT, PLE_DIM),
                                p_sample[0].reshape(N_SAMPLE, PLE_DIM), w_ple_gate[0], w_ple_proj[0],
                                vec(norm_ple[0]), vec(final_norm))
    y_prompt = y_prompt.reshape(N_PROMPT_SEQ, SEQ, D_MODEL)
    y_sample = y_sample.reshape(N_SAMPLE, 1, D_MODEL)
    glu_cols = slice(ACT_GLU * D_MODEL, (ACT_GLU + 1) * D_MODEL)
    new_conv_prompt = jnp.stack([act[(b + 1) * SEQ - CONV_HIST:(b + 1) * SEQ, glu_cols]
                                 for b in range(N_PROMPT_SEQ)])[None]
    return (y_prompt, y_sample, new_conv_prompt, new_conv_sample, vn_s.reshape(1, N_SAMPLE, 1, D_MODEL))
```

```python
import jax
import jax.numpy as jnp
from jax import lax
from jax.experimental import pallas as pl
from jax.experimental.pallas import tpu as pltpu

F32 = jnp.float32
BF16 = jnp.bfloat16

D_MODEL = 2048
N_PROMPT_SEQ = 4
SEQ = 2048
N_PROMPT = N_PROMPT_SEQ * SEQ
N_SAMPLE = 128
N_ROWS = N_PROMPT + N_SAMPLE
CHUNK = 128
N_CHUNKS = N_ROWS // CHUNK
CHUNKS_PER_SEQ = SEQ // CHUNK
A_GROUPS = 8
A_GROUP_DIM = D_MODEL // A_GROUPS
CONV_WIDTH = 31
CONV_HIST = CONV_WIDTH - 1
HIST_BLOCK = 32
MOE_GROUPS = 4
EXPERTS_PER_GROUP = 8
N_EXPERTS = MOE_GROUPS * EXPERTS_PER_GROUP
TOP_K = 2
D_EXPERT = D_MODEL // 2
PLE_DIM = 256
EPS = 1e-6
LANES = 128

XN_TM = 512
IN_TM = 1664
IN_TN = 512
IN_COL_TILES = D_MODEL // IN_TN
ACT_BLOCKS = 5
ACT_GU, ACT_GV, ACT_GLU, ACT_SGA, ACT_SGB = range(ACT_BLOCKS)
MIX_TM = 512
MIX_PROMPT_STEPS = N_PROMPT // MIX_TM
FIN_TM = 256
FIN_PROMPT_STEPS = N_PROMPT // FIN_TM
ROW_BLOCK = 128
SUBLANES = 8
BLOCK_TILES = ROW_BLOCK // SUBLANES
N_PAIRS = N_ROWS * TOP_K
N_KEYS = N_EXPERTS * SUBLANES
MAX_BLOCKS = N_PAIRS * SUBLANES // ROW_BLOCK + N_EXPERTS
ITEM_BLOCKS = 8
ITEM_ROWS = ITEM_BLOCKS * ROW_BLOCK
MAX_ITEMS = (MAX_BLOCKS + (ITEM_BLOCKS - 1) * N_EXPERTS) // ITEM_BLOCKS
EXPERT_BLOCKS = N_ROWS // ROW_BLOCK + ITEM_BLOCKS
EXPERT_SLOTS = EXPERT_BLOCKS * ROW_BLOCK
SLOT_TABLE = N_EXPERTS * EXPERT_SLOTS
EXPERT_SLICES = 4
EXPERT_TN = D_EXPERT // EXPERT_SLICES
PAIR_REGION = 8448
DUMP_ROW0 = PAIR_REGION + N_ROWS
PAIR_ROWS = DUMP_ROW0 + ITEM_ROWS

VMEM_LIMIT = 56 * 1024 * 1024


def _rms(x, g):
    return x * lax.rsqrt(jnp.mean(x * x, axis=-1, keepdims=True) + EPS) * g


def _layer_norm(x, g, b):
    mu = jnp.mean(x, axis=-1, keepdims=True)
    xc = x - mu
    return xc * lax.rsqrt(jnp.mean(xc * xc, axis=-1, keepdims=True) + EPS) * g + b


def _sigmoid(x):
    return 1.0 / (1.0 + jnp.exp(-x))


def _silu(x):
    return x * _sigmoid(x)


def _gelu(x):
    return jax.nn.gelu(x, approximate=True)


def _bdot(a, b):
    return jnp.dot(a, b, preferred_element_type=F32)


STAGE_ROWS = 256


def _load_weight_bf16(w_hbm, dst_ref, stage_ref, sem):
    n_chunks = w_hbm.shape[0] // STAGE_ROWS

    def chunk_copy(j):
        return pltpu.make_async_copy(w_hbm.at[pl.ds(j * STAGE_ROWS, STAGE_ROWS)], stage_ref.at[j % 2], sem.at[j % 2])

    chunk_copy(0).start()
    for j in range(n_chunks):
        if j + 1 < n_chunks:
            chunk_copy(j + 1).start()
        chunk_copy(j).wait()
        dst_ref[pl.ds(j * STAGE_ROWS, STAGE_ROWS), :] = stage_ref[j % 2].astype(BF16)


_WEIGHT_STAGE = [pltpu.VMEM((2, STAGE_ROWS, D_MODEL), F32), pltpu.SemaphoreType.DMA((2,))]


def _xnorm_kernel(xp_ref, xs_ref, nm_ref, xn_ref):
    i = pl.program_id(0)

    @pl.when(i < N_PROMPT // XN_TM)
    def _():
        xn_ref[...] = _rms(xp_ref[...], nm_ref[...]).astype(BF16)

    @pl.when(i == N_PROMPT // XN_TM)
    def _():
        xn_ref[0:N_SAMPLE, :] = _rms(xs_ref[...], nm_ref[...]).astype(BF16)
        xn_ref[N_SAMPLE:, :] = jnp.zeros((XN_TM - N_SAMPLE, D_MODEL), BF16)


def _xnorm(x_prompt, x_sample, norm_mix):
    last = N_PROMPT // XN_TM - 1
    return pl.pallas_call(
        _xnorm_kernel,
        out_shape=jax.ShapeDtypeStruct((N_ROWS, D_MODEL), BF16),
        grid=(N_PROMPT // XN_TM + 1,),
        in_specs=[pl.BlockSpec((XN_TM, D_MODEL), lambda i: (jnp.minimum(i, last), 0)),
                  pl.BlockSpec((N_SAMPLE, D_MODEL), lambda i: (0, 0)),
                  pl.BlockSpec((1, D_MODEL), lambda i: (0, 0))],
        out_specs=pl.BlockSpec((XN_TM, D_MODEL), lambda i: (i, 0)),
        name="xnorm",
    )(x_prompt, x_sample, norm_mix)


def _in_w_col(j):
    t = IN_COL_TILES
    jj = j - 2 * t
    glu_col = 2 * t + (jj % 2) * t + jj // 2
    return jnp.where((j >= 2 * t) & (j < 4 * t), glu_col, j)


def _in_out_col(j):
    t = IN_COL_TILES
    return jnp.where(j < 2 * t, j, jnp.where(j < 4 * t, 2 * t + (j - 2 * t) // 2, j - t))


def _in_proj_kernel(xn_ref, w_ref, o_ref, val_ref):
    j = pl.program_id(1)
    t = IN_COL_TILES

    def z():
        return _bdot(xn_ref[...], w_ref[...].astype(BF16))

    @pl.when(j < 2 * t)
    def _():
        o_ref[...] = _gelu(z())

    @pl.when((j >= 2 * t) & (j < 4 * t) & (j % 2 == 0))
    def _():
        val_ref[...] = z()

    @pl.when((j >= 2 * t) & (j < 4 * t) & (j % 2 == 1))
    def _():
        o_ref[...] = val_ref[...] * _sigmoid(z())

    @pl.when(j >= 4 * t)
    def _():
        o_ref[...] = _sigmoid(z())


def _in_proj(xn, w_in):
    return pl.pallas_call(
        _in_proj_kernel,
        out_shape=jax.ShapeDtypeStruct((N_ROWS, ACT_BLOCKS * D_MODEL), F32),
        grid=(N_ROWS // IN_TM, 6 * IN_COL_TILES),
        in_specs=[
            pl.BlockSpec((IN_TM, D_MODEL), lambda i, j: (i, 0)),
            pl.BlockSpec((D_MODEL, IN_TN), lambda i, j: (0, _in_w_col(j))),
        ],
        out_specs=pl.BlockSpec((IN_TM, IN_TN), lambda i, j: (i, _in_out_col(j))),
        scratch_shapes=[pltpu.VMEM((IN_TM, IN_TN), F32)],
        compiler_params=pltpu.CompilerParams(
            dimension_semantics=("arbitrary", "arbitrary"), vmem_limit_bytes=VMEM_LIMIT),
        name="in_proj",
    )(xn, w_in)


SCONV_B = 16


def _sconv_kernel(st_ref, glu_ref, w_ref, wl_ref, cb_ref, new_ref, conv_ref):
    glu = glu_ref[...]
    new_ref[0:CONV_HIST - 1] = st_ref[1:CONV_HIST]
    new_ref[CONV_HIST - 1] = glu
    acc = st_ref[0] * w_ref[0:1, :]
    for k in range(1, CONV_HIST):
        acc = acc + st_ref[k] * w_ref[k:k + 1, :]
    conv_ref[...] = acc + glu * wl_ref[...] + cb_ref[...]


def _sconv(state, act, conv_w, conv_b):
    return pl.pallas_call(
        _sconv_kernel,
        out_shape=(
            jax.ShapeDtypeStruct((CONV_HIST, N_SAMPLE, D_MODEL), F32),
            jax.ShapeDtypeStruct((N_SAMPLE, D_MODEL), F32),
        ),
        grid=(N_SAMPLE // SCONV_B,),
        in_specs=[
            pl.BlockSpec((CONV_HIST, SCONV_B, D_MODEL), lambda i: (0, i, 0)),
            pl.BlockSpec((SCONV_B, D_MODEL), lambda i: (N_PROMPT // SCONV_B + i, ACT_GLU)),
            pl.BlockSpec((CONV_HIST, D_MODEL), lambda i: (0, 0)),
            pl.BlockSpec((1, D_MODEL), lambda i: (0, 0)),
            pl.BlockSpec((1, D_MODEL), lambda i: (0, 0)),
        ],
        out_specs=(
            pl.BlockSpec((CONV_HIST, SCONV_B, D_MODEL), lambda i: (0, i, 0)),
            pl.BlockSpec((SCONV_B, D_MODEL), lambda i: (i, 0)),
        ),
        name="sconv",
    )(state, act, conv_w[:CONV_HIST], conv_w[CONV_HIST:], conv_b)


CONV_STRIP = 256
WIN_ROWS = HIST_BLOCK + CHUNK


def _seqmix_kernel(gv_ref, gu_ref, cur_ref, prev_ref, sga_ref, sgb_ref, convs_ref, wmix_ref, bmix_ref, w00_ref,
                   b00_ref, lvg_ref, lvb_ref, cw_ref, cb_ref, lcg_ref, lcb_ref, wa_hbm, wb_hbm,
                   m_ref, vns_ref, win_ref, conv_ref, a_ref, b_ref, wa_ref, wb_ref, stage_ref, sem):
    c = pl.program_id(0)
    is_sample = c == N_CHUNKS - 1

    @pl.when(c == 0)
    def _():
        _load_weight_bf16(wa_hbm, wa_ref, stage_ref, sem)
        _load_weight_bf16(wb_hbm, wb_ref, stage_ref, sem)

    fresh = (c % CHUNKS_PER_SEQ) == 0
    win_ref[0:HIST_BLOCK, :] = jnp.where(fresh, 0.0, prev_ref[...]).astype(BF16).astype(F32)
    win_ref[HIST_BLOCK:WIN_ROWS, :] = cur_ref[...].astype(BF16).astype(F32)
    off = HIST_BLOCK - CONV_HIST

    vn = _layer_norm(gv_ref[...], lvg_ref[...], lvb_ref[...])
    vns_ref[...] = vn

    vnb = vn.astype(BF16)
    for g in range(A_GROUPS):
        sl = slice(g * A_GROUP_DIM, (g + 1) * A_GROUP_DIM)
        s_chunk = _bdot(wmix_ref[g], vnb[:, sl]) + bmix_ref[:, sl]
        s_first = vn[:, sl] * w00_ref[:, sl] + b00_ref[:, sl]
        a_ref[:, sl] = (gu_ref[:, sl] * jnp.where(is_sample, s_first, s_chunk)).astype(BF16)
    pa = _bdot(a_ref[...], wa_ref[...])

    def phase_sum(r, cs):
        rows = CHUNK + (SUBLANES if r else 0)
        total = None
        for k in range(CONV_WIDTH):
            q, rk = divmod(off + k, SUBLANES)
            if rk == r:
                term = win_ref[SUBLANES * q:SUBLANES * q + rows, cs] * cw_ref[k:k + 1, cs].astype(BF16).astype(F32)
                total = term if total is None else total + term
        return total[r:r + CHUNK, :]

    for s0 in range(0, D_MODEL, CONV_STRIP):
        cs = slice(s0, s0 + CONV_STRIP)
        acc = phase_sum(0, cs)
        for r in range(1, SUBLANES):
            acc = acc + phase_sum(r, cs)
        conv_ref[:, cs] = acc + cb_ref[:, cs]
    conv = jnp.where(is_sample, convs_ref[...], conv_ref[...])
    b_ref[...] = _silu(_layer_norm(conv, lcg_ref[...], lcb_ref[...])).astype(BF16)

    pb = _bdot(b_ref[...], wb_ref[...])
    m_ref[...] = (sga_ref[...] * pa + sgb_ref[...] * pb).astype(BF16)


def _seqmix(act, conv_s, wmix, bmix, w00, b00, ln_v_g, ln_v_b, conv_w, conv_b, ln_c_g, ln_c_b, wa, wb):
    row = lambda c: (c, 0)
    const2 = lambda c: (0, 0)
    vec = pl.BlockSpec((1, D_MODEL), const2)
    hbm = pl.BlockSpec(memory_space=pltpu.MemorySpace.HBM)
    hist_per_chunk = CHUNK // HIST_BLOCK
    act_block = lambda blk: pl.BlockSpec((CHUNK, D_MODEL), lambda c: (c, blk))
    return pl.pallas_call(
        _seqmix_kernel,
        out_shape=(
            jax.ShapeDtypeStruct((N_ROWS, D_MODEL), BF16),
            jax.ShapeDtypeStruct((N_SAMPLE, D_MODEL), F32),
        ),
        grid=(N_CHUNKS,),
        in_specs=[
            act_block(ACT_GV), act_block(ACT_GU), act_block(ACT_GLU),
            pl.BlockSpec((HIST_BLOCK, D_MODEL), lambda c: (jnp.maximum(c * hist_per_chunk - 1, 0), ACT_GLU)),
            act_block(ACT_SGA), act_block(ACT_SGB),
            pl.BlockSpec((N_SAMPLE, D_MODEL), const2),
            pl.BlockSpec((A_GROUPS, CHUNK, CHUNK), lambda c: (0, 0, 0)),
            pl.BlockSpec((CHUNK, D_MODEL), const2),
            vec, vec, vec, vec,
            pl.BlockSpec((CONV_WIDTH + 1, D_MODEL), const2),
            vec, vec, vec,
            hbm, hbm,
        ],
        out_specs=(
            pl.BlockSpec((CHUNK, D_MODEL), row),
            pl.BlockSpec((N_SAMPLE, D_MODEL), const2),
        ),
        scratch_shapes=[pltpu.VMEM((WIN_ROWS, D_MODEL), F32),
                        pltpu.VMEM((CHUNK, D_MODEL), F32),
                        pltpu.VMEM((CHUNK, D_MODEL), BF16), pltpu.VMEM((CHUNK, D_MODEL), BF16),
                        pltpu.VMEM((D_MODEL, D_MODEL), BF16), pltpu.VMEM((D_MODEL, D_MODEL), BF16)] + _WEIGHT_STAGE,
        compiler_params=pltpu.CompilerParams(
            dimension_semantics=("arbitrary",), vmem_limit_bytes=VMEM_LIMIT),
        name="seqmix",
    )(act, act, act, act, act, act, conv_s, wmix, bmix, w00, b00, ln_v_g, ln_v_b, conv_w, conv_b, ln_c_g, ln_c_b,
      wa, wb)


def _route(logits):
    col = lax.broadcasted_iota(jnp.int32, logits.shape, 1).astype(F32)
    neg = jnp.float32(-jnp.inf)
    big = jnp.float32(1e9)
    lg = jnp.where(col < MOE_GROUPS, logits, neg)
    gmax = jnp.max(lg, axis=-1, keepdims=True)
    gi = jnp.min(jnp.where(lg == gmax, col, big), axis=-1, keepdims=True)
    gw = 1.0 / jnp.sum(jnp.exp(lg - gmax), axis=-1, keepdims=True)
    lo = MOE_GROUPS + gi * EXPERTS_PER_GROUP
    le = jnp.where((col >= lo) & (col < lo + EXPERTS_PER_GROUP), logits, neg)
    m1 = jnp.max(le, axis=-1, keepdims=True)
    i1 = jnp.min(jnp.where(le == m1, col, big), axis=-1, keepdims=True)
    le2 = jnp.where(col == i1, neg, le)
    m2 = jnp.max(le2, axis=-1, keepdims=True)
    i2 = jnp.min(jnp.where(le2 == m2, col, big), axis=-1, keepdims=True)
    e = jnp.exp(m2 - m1)
    w1 = gw / (1.0 + e)
    w2 = gw * e / (1.0 + e)
    eid = jnp.where(col == 0, i1 - MOE_GROUPS, jnp.where(col == 1, i2 - MOE_GROUPS, 0.0)).astype(jnp.int32)
    ew = jnp.where(col == 0, w1, jnp.where(col == 1, w2, 0.0))
    return eid, ew


def _mixout_rows(m, x, wo_ref, nf_ref, wr_ref, br_ref, tril, count_ref):
    h = x + _bdot(m, wo_ref[...])
    hn = _rms(h, nf_ref[...])
    logits = _bdot(hn.astype(BF16), wr_ref[...]) + br_ref[...]
    eid, ew = _route(logits)
    keys = lax.broadcasted_iota(jnp.int32, (eid.shape[0], N_KEYS), 1)
    phase = lax.broadcasted_iota(jnp.int32, (eid.shape[0], N_KEYS), 0) % SUBLANES
    first = keys == eid[:, 0:1] * SUBLANES + phase
    second = keys == eid[:, 1:2] * SUBLANES + phase
    hits = first.astype(F32) + second.astype(F32)
    before = _bdot(tril, hits.astype(BF16)) + count_ref[...]
    rank0 = jnp.sum(jnp.where(first, before, 0.0), axis=-1, keepdims=True)
    rank1 = jnp.sum(jnp.where(second, before + first.astype(F32), 0.0), axis=-1, keepdims=True)
    count_ref[...] = count_ref[...] + jnp.sum(hits, axis=0, keepdims=True)
    col = lax.broadcasted_iota(jnp.int32, eid.shape, 1)
    ids = jnp.where(col == 2, rank0.astype(jnp.int32), jnp.where(col == 3, rank1.astype(jnp.int32), eid))
    return h, ids, ew


def _mixout_kernel(mp_ref, xp_ref, ms_ref, xs_ref, wo_hbm, nf_ref, wr_ref, br_ref, tril_ref,
                   h_ref, eid_ref, ew_ref, cnt_ref, wo_ref, count_ref, stage_ref, sem):
    i = pl.program_id(0)

    @pl.when(i == 0)
    def _():
        _load_weight_bf16(wo_hbm, wo_ref, stage_ref, sem)
        count_ref[...] = jnp.zeros((1, N_KEYS), F32)

    @pl.when(i < MIX_PROMPT_STEPS)
    def _():
        h, eid, ew = _mixout_rows(mp_ref[...], xp_ref[...], wo_ref, nf_ref, wr_ref, br_ref, tril_ref[...], count_ref)
        h_ref[...] = h
        eid_ref[...] = eid
        ew_ref[...] = ew

    @pl.when(i == MIX_PROMPT_STEPS)
    def _():
        h, eid, ew = _mixout_rows(ms_ref[...], xs_ref[...], wo_ref, nf_ref, wr_ref, br_ref,
                                  tril_ref[0:N_SAMPLE, 0:N_SAMPLE], count_ref)
        for ref, val in ((h_ref, h), (eid_ref, eid), (ew_ref, ew)):
            ref[0:N_SAMPLE, :] = val
            ref[N_SAMPLE:, :] = jnp.zeros((MIX_TM - N_SAMPLE, val.shape[1]), val.dtype)

    cnt_ref[...] = jnp.broadcast_to(count_ref[...], cnt_ref.shape).astype(jnp.int32)


def _mixout(m_all, x_prompt, x_sample, wo, norm_ffn, w_route, b_route):
    last = MIX_PROMPT_STEPS - 1
    row = lambda i: (i, 0)
    prow = lambda i: (jnp.minimum(i, last), 0)
    const2 = lambda i: (0, 0)
    hbm = pl.BlockSpec(memory_space=pltpu.MemorySpace.HBM)
    tile_row = jnp.arange(MIX_TM, dtype=jnp.int32)
    tril = (tile_row[None, :] < tile_row[:, None]).astype(BF16)
    return pl.pallas_call(
        _mixout_kernel,
        out_shape=(
            jax.ShapeDtypeStruct((N_ROWS, D_MODEL), F32),
            jax.ShapeDtypeStruct((N_ROWS, LANES), jnp.int32),
            jax.ShapeDtypeStruct((N_ROWS, LANES), F32),
            jax.ShapeDtypeStruct((SUBLANES, N_KEYS), jnp.int32),
        ),
        grid=(MIX_PROMPT_STEPS + 1,),
        in_specs=[pl.BlockSpec((MIX_TM, D_MODEL), prow), pl.BlockSpec((MIX_TM, D_MODEL), prow),
                  pl.BlockSpec((N_SAMPLE, D_MODEL), lambda i: (N_PROMPT // N_SAMPLE, 0)),
                  pl.BlockSpec((N_SAMPLE, D_MODEL), const2),
                  hbm,
                  pl.BlockSpec((1, D_MODEL), const2),
                  pl.BlockSpec((D_MODEL, LANES), const2),
                  pl.BlockSpec((1, LANES), const2),
                  pl.BlockSpec((MIX_TM, MIX_TM), const2)],
        out_specs=(pl.BlockSpec((MIX_TM, D_MODEL), row), pl.BlockSpec((MIX_TM, LANES), row),
                   pl.BlockSpec((MIX_TM, LANES), row), pl.BlockSpec((SUBLANES, N_KEYS), const2)),
        scratch_shapes=[pltpu.VMEM((D_MODEL, D_MODEL), BF16), pltpu.VMEM((1, N_KEYS), F32)] + _WEIGHT_STAGE,
        compiler_params=pltpu.CompilerParams(
            dimension_semantics=("arbitrary",), vmem_limit_bytes=VMEM_LIMIT),
        name="mixout",
    )(m_all, x_prompt, m_all, x_sample, wo, norm_ffn, w_route, b_route, tril)


def _experts_kernel(item_e, item_blk0, item_nb, src_hbm, dst_hbm, h_hbm, nf_ref, wg_ref, wu_ref, wd_ref,
                    y_hbm, src_s, dst_s, pend_s, xg_ref, yst_ref, xs_ref, hm_ref, wgb_ref, wub_ref, wdb_ref,
                    idx_sem, g_sem, s_sem):
    w = pl.program_id(0)
    s = pl.program_id(1)
    nb = item_nb[w]
    nxt = jnp.minimum(w + 1, MAX_ITEMS - 1)
    nb_next = jnp.where(w + 1 < MAX_ITEMS, item_nb[nxt], 0)

    def table_fetch(tbl_hbm, tbl_s, sem_idx, item):
        slot0 = pl.multiple_of(item_blk0[item] * ROW_BLOCK, ROW_BLOCK)
        return pltpu.make_async_copy(tbl_hbm.at[pl.ds(slot0, ITEM_ROWS)], tbl_s, idx_sem.at[sem_idx])

    def row_copy_in(t, u, blk):
        return pltpu.make_async_copy(h_hbm.at[src_s[t * SUBLANES + u], pl.ds(u, 1)],
                                     xg_ref.at[t, pl.ds(u, 1)], g_sem.at[blk])

    def row_copy_out(t, u, blk):
        return pltpu.make_async_copy(yst_ref.at[t, pl.ds(u, 1)],
                                     y_hbm.at[dst_s[t * SUBLANES + u], pl.ds(u, 1)], s_sem.at[blk])

    def block_rows(blk):
        return pl.ds(pl.multiple_of(blk * ROW_BLOCK, ROW_BLOCK), ROW_BLOCK)

    def block_tiles(blk):
        return pl.ds(pl.multiple_of(blk * BLOCK_TILES, BLOCK_TILES), BLOCK_TILES)

    def block_copy_wait(sem, blk):
        pltpu.make_async_copy(xg_ref.at[block_tiles(blk)], xg_ref.at[block_tiles(blk)], sem.at[blk]).wait()

    def for_each_row(n_blocks, row_copy):
        def tile(t, carry):
            for u in range(SUBLANES):
                row_copy(t, u, lax.shift_right_logical(t, BLOCK_TILES.bit_length() - 1)).start()
            return carry
        lax.fori_loop(0, n_blocks * BLOCK_TILES, tile, 0)

    def issue_gather(n_blocks):
        for_each_row(n_blocks, row_copy_in)

    def drain_scatter():
        def drain(blk, carry):
            block_copy_wait(s_sem, blk)
            return carry
        lax.fori_loop(0, pend_s[0], drain, 0)
        pend_s[0] = 0

    @pl.when((w == 0) & (s == 0))
    def _():
        pad_tiles = pl.ds(0, BLOCK_TILES)
        xg_ref[pad_tiles] = jnp.zeros((BLOCK_TILES, SUBLANES, D_MODEL), F32)
        for row0 in (N_ROWS, *range(DUMP_ROW0, PAIR_ROWS, ROW_BLOCK)):
            cz = pltpu.make_async_copy(xg_ref.at[pad_tiles], y_hbm.at[pl.ds(row0 // SUBLANES, BLOCK_TILES)],
                                       idx_sem.at[0])
            cz.start()
            cz.wait()
        pend_s[0] = 0
        ci = table_fetch(src_hbm, src_s, 0, w)
        ci.start()
        ci.wait()
        issue_gather(nb)

    @pl.when((s == 0) & (nb > 0))
    def _():
        table_fetch(dst_hbm, dst_s, 1, w).start()

        def land(blk, carry):
            block_copy_wait(g_sem, blk)
            x = xg_ref[block_tiles(blk)].reshape(ROW_BLOCK, D_MODEL)
            xs_ref[block_rows(blk), :] = _rms(x, nf_ref[...]).astype(BF16)
            return carry
        lax.fori_loop(0, nb, land, 0)

    @pl.when((s == 1) & (nb_next > 0))
    def _():
        table_fetch(src_hbm, src_s, 0, nxt).start()

    @pl.when((s == 2) & (nb_next > 0))
    def _():
        table_fetch(src_hbm, src_s, 0, nxt).wait()
        issue_gather(nb_next)

    def over_row_blocks(fn):
        def four_blocks(j, carry):
            fn(pl.multiple_of(j * (4 * ROW_BLOCK), 4 * ROW_BLOCK), 4 * ROW_BLOCK)
            return carry
        lax.fori_loop(0, nb >> 2, four_blocks, 0)

        @pl.when((nb & 2) == 2)
        def _():
            fn(pl.multiple_of((nb >> 2) * (4 * ROW_BLOCK), 2 * ROW_BLOCK), 2 * ROW_BLOCK)

        @pl.when((nb & 1) == 1)
        def _():
            fn(pl.multiple_of((nb - 1) * ROW_BLOCK, ROW_BLOCK), ROW_BLOCK)

    @pl.when(nb > 0)
    def _():
        wgb_ref[...] = wg_ref[...].astype(BF16)
        wub_ref[...] = wu_ref[...].astype(BF16)
        wdb_ref[pl.ds(pl.multiple_of(s * EXPERT_TN, EXPERT_TN), EXPERT_TN), :] = wd_ref[...].astype(BF16)

        def gate_up(row0, nrows):
            rows = pl.ds(row0, nrows)
            x = xs_ref[rows, :]
            hm_ref[s, rows, :] = (_silu(_bdot(x, wgb_ref[...])) * _bdot(x, wub_ref[...])).astype(BF16)
        over_row_blocks(gate_up)

    @pl.when((s == EXPERT_SLICES - 1) & (nb > 0))
    def _():
        drain_scatter()

        def down(row0, nrows):
            rows = pl.ds(row0, nrows)
            hm = jnp.concatenate([hm_ref[k, rows, :] for k in range(EXPERT_SLICES)], axis=1)
            tiles = pl.ds(pl.multiple_of(row0 // SUBLANES, BLOCK_TILES), nrows // SUBLANES)
            yst_ref[tiles] = _bdot(hm, wdb_ref[...]).reshape(nrows // SUBLANES, SUBLANES, D_MODEL)
        over_row_blocks(down)

        table_fetch(dst_hbm, dst_s, 1, w).wait()

        for_each_row(nb, row_copy_out)
        pend_s[0] = nb

    @pl.when((w == pl.num_programs(0) - 1) & (s == EXPERT_SLICES - 1))
    def _():
        drain_scatter()


def _experts(n_items, item_e, item_blk0, item_nb, src_row, dst_row, h_all, norm_ffn, w_gate, w_up, w_down):
    hbm = pl.BlockSpec(memory_space=pltpu.MemorySpace.HBM)

    def w_slice(w, s, inb):
        return jnp.where(inb[w] > 0, s, EXPERT_SLICES - 1)

    grid_spec = pltpu.PrefetchScalarGridSpec(
        num_scalar_prefetch=3,
        grid=(n_items, EXPERT_SLICES),
        in_specs=[
            hbm, hbm, hbm,
            pl.BlockSpec((1, D_MODEL), lambda w, s, ie, ib, inb: (0, 0)),
            pl.BlockSpec((None, D_MODEL, EXPERT_TN), lambda w, s, ie, ib, inb: (ie[w], 0, w_slice(w, s, inb))),
            pl.BlockSpec((None, D_MODEL, EXPERT_TN), lambda w, s, ie, ib, inb: (ie[w], 0, w_slice(w, s, inb))),
            pl.BlockSpec((None, EXPERT_TN, D_MODEL), lambda w, s, ie, ib, inb: (ie[w], w_slice(w, s, inb), 0)),
        ],
        out_specs=hbm,
        scratch_shapes=[
            pltpu.SMEM((ITEM_ROWS,), jnp.int32),
            pltpu.SMEM((ITEM_ROWS,), jnp.int32),
            pltpu.SMEM((1,), jnp.int32),
            pltpu.VMEM((ITEM_ROWS // SUBLANES, SUBLANES, D_MODEL), F32),
            pltpu.VMEM((ITEM_ROWS // SUBLANES, SUBLANES, D_MODEL), F32),
            pltpu.VMEM((ITEM_ROWS, D_MODEL), BF16),
            pltpu.VMEM((EXPERT_SLICES, ITEM_ROWS, EXPERT_TN), BF16),
            pltpu.VMEM((D_MODEL, EXPERT_TN), BF16),
            pltpu.VMEM((D_MODEL, EXPERT_TN), BF16),
            pltpu.VMEM((D_EXPERT, D_MODEL), BF16),
            pltpu.SemaphoreType.DMA((2,)),
            pltpu.SemaphoreType.DMA((ITEM_BLOCKS,)),
            pltpu.SemaphoreType.DMA((ITEM_BLOCKS,)),
        ],
    )
    return pl.pallas_call(
        _experts_kernel,
        out_shape=jax.ShapeDtypeStruct((PAIR_ROWS // SUBLANES, SUBLANES, D_MODEL), F32),
        grid_spec=grid_spec,
        compiler_params=pltpu.CompilerParams(
            dimension_semantics=("arbitrary", "arbitrary"), vmem_limit_bytes=VMEM_LIMIT),
        name="experts",
    )(item_e, item_blk0, item_nb, src_row, dst_row, h_all.reshape(N_ROWS // SUBLANES, SUBLANES, D_MODEL),
      norm_ffn, w_gate, w_up, w_down).reshape(PAIR_ROWS, D_MODEL)


def _dispatch_tables(ids, counts):
    flat_e = ids[:, 0:TOP_K].reshape(-1)
    rank = ids[:, TOP_K:2 * TOP_K].reshape(-1)
    pair = jnp.arange(N_PAIRS, dtype=jnp.int32)
    slot = flat_e * EXPERT_SLOTS + rank * SUBLANES + (pair // TOP_K) % SUBLANES
    slot_pair = jnp.full((SLOT_TABLE,), N_PAIRS, jnp.int32).at[slot].set(pair)
    slot_id = jnp.arange(SLOT_TABLE, dtype=jnp.int32)
    live_slot = slot_pair < N_PAIRS
    src_row = jnp.where(live_slot, slot_pair // TOP_K, 0) // SUBLANES
    dst_row = jnp.where(live_slot, (slot_pair % TOP_K) * PAIR_REGION + slot_pair // TOP_K,
                        DUMP_ROW0 + slot_id % ITEM_ROWS) // SUBLANES
    nblk = (jnp.max(counts, axis=1) * SUBLANES + ROW_BLOCK - 1) // ROW_BLOCK
    nitem = (nblk + ITEM_BLOCKS - 1) // ITEM_BLOCKS
    upto = jnp.arange(N_EXPERTS)[None, :] <= jnp.arange(N_EXPERTS)[:, None]
    item_end = jnp.sum(jnp.where(upto, nitem[None, :], 0), axis=1)
    total = item_end[-1]
    idx = jnp.arange(MAX_ITEMS, dtype=jnp.int32)
    live = idx < total
    at = jnp.minimum(idx, total - 1)
    e_of = jnp.minimum(jnp.sum((item_end[None, :] <= at[:, None]).astype(jnp.int32), axis=1), N_EXPERTS - 1)
    local = jnp.minimum(idx, total - 1) - (item_end - nitem)[e_of]
    item_blk0 = (e_of * EXPERT_BLOCKS + local * ITEM_BLOCKS).astype(jnp.int32)
    item_nb = jnp.where(live, jnp.clip(nblk[e_of] - local * ITEM_BLOCKS, 0, ITEM_BLOCKS), 0).astype(jnp.int32)
    return total.astype(jnp.int32), e_of, item_blk0, item_nb, src_row, dst_row


def _final_rows(h, y0, y1, ew, p, wg_ref, wp_ref, np_ref, fn_ref):
    h = h + (ew[:, 0:1] * y0 + ew[:, 1:2] * y1)
    gate = _sigmoid(_bdot(_rms(h, np_ref[...]).astype(BF16), wg_ref[...]))
    h = h + gate * _bdot(p.astype(BF16), wp_ref[...])
    return _rms(h, fn_ref[...])


def _final_kernel(h_ref, y0_ref, y1_ref, ew_ref, p_ref, hs_ref, y0s_ref, y1s_ref, ews_ref, ps_ref,
                  wg_hbm, wp_hbm, np_ref, fn_ref, op_ref, os_ref, wg_ref, wp_ref, stage_ref, sem):
    i = pl.program_id(0)

    @pl.when(i == 0)
    def _():
        _load_weight_bf16(wg_hbm, wg_ref, stage_ref, sem)
        _load_weight_bf16(wp_hbm, wp_ref, stage_ref, sem)

    @pl.when(i < FIN_PROMPT_STEPS)
    def _():
        op_ref[...] = _final_rows(h_ref[...], y0_ref[...], y1_ref[...], ew_ref[...], p_ref[...],
                                  wg_ref, wp_ref, np_ref, fn_ref)

    @pl.when(i == FIN_PROMPT_STEPS)
    def _():
        os_ref[...] = _final_rows(hs_ref[...], y0s_ref[...], y1s_ref[...], ews_ref[...], ps_ref[...],
                                  wg_ref, wp_ref, np_ref, fn_ref)


def _final(h_all, ypairs, ew, p_prompt, p_sample, w_gate, w_proj, norm_ple, final_norm):
    last = FIN_PROMPT_STEPS - 1
    prow = lambda i: (jnp.minimum(i, last), 0)
    const2 = lambda i: (0, 0)
    hbm = pl.BlockSpec(memory_space=pltpu.MemorySpace.HBM)
    sblk = N_PROMPT // N_SAMPLE
    pact = pl.BlockSpec((FIN_TM, D_MODEL), prow)
    sact = lambda off: pl.BlockSpec((N_SAMPLE, D_MODEL), lambda i: (off + sblk, 0))
    return pl.pallas_call(
        _final_kernel,
        out_shape=(jax.ShapeDtypeStruct((N_PROMPT, D_MODEL), F32), jax.ShapeDtypeStruct((N_SAMPLE, D_MODEL), F32)),
        grid=(FIN_PROMPT_STEPS + 1,),
        in_specs=[pact, pact,
                  pl.BlockSpec((FIN_TM, D_MODEL), lambda i: (jnp.minimum(i, last) + PAIR_REGION // FIN_TM, 0)),
                  pl.BlockSpec((FIN_TM, LANES), prow),
                  pl.BlockSpec((FIN_TM, PLE_DIM), prow),
                  sact(0), sact(0), sact(PAIR_REGION // N_SAMPLE),
                  pl.BlockSpec((N_SAMPLE, LANES), lambda i: (sblk, 0)),
                  pl.BlockSpec((N_SAMPLE, PLE_DIM), const2),
                  hbm, hbm,
                  pl.BlockSpec((1, D_MODEL), const2),
                  pl.BlockSpec((1, D_MODEL), const2)],
        out_specs=(pact, pl.BlockSpec((N_SAMPLE, D_MODEL), const2)),
        scratch_shapes=[pltpu.VMEM((D_MODEL, D_MODEL), BF16), pltpu.VMEM((PLE_DIM, D_MODEL), BF16)] + _WEIGHT_STAGE,
        compiler_params=pltpu.CompilerParams(
            dimension_semantics=("arbitrary",), vmem_limit_bytes=VMEM_LIMIT),
        name="final",
    )(h_all, ypairs, ypairs, ew, p_prompt, h_all, ypairs, ypairs, ew, p_sample, w_gate, w_proj, norm_ple, final_norm)


def kernel(x_prompt, x_sample, state_conv, p_prompt, p_sample, norm_mix, w_in, ln_v_g, ln_v_b, w_spatial,
           b_spatial, w_proj_a, conv_w, conv_b, ln_c_g, ln_c_b, w_proj_b, w_out, norm_ffn, w_router_group,
           b_router_group, w_router_expert, b_router_expert, w_exp_gate, w_exp_up, w_exp_down, norm_ple,
           w_ple_gate, w_ple_proj, final_norm):
    assert w_in.shape[0] == 1, "single layer"
    vec = lambda v: v.reshape(1, -1)
    xp = x_prompt.reshape(N_PROMPT, D_MODEL)
    xs = x_sample.reshape(N_SAMPLE, D_MODEL)
    xn = _xnorm(xp, xs, vec(norm_mix[0]))

    w_s, b_s = w_spatial[0], b_spatial[0]
    wmix = jnp.tril(w_s).astype(BF16)
    bmix = jnp.repeat(b_s.T, A_GROUP_DIM, axis=1)
    w00 = jnp.repeat(w_s[:, 0, 0], A_GROUP_DIM).reshape(1, D_MODEL)
    b00 = jnp.repeat(b_s[:, 0], A_GROUP_DIM).reshape(1, D_MODEL)

    act = _in_proj(xn, w_in[0])

    cw = conv_w[0]
    new_state, conv_s = _sconv(jnp.transpose(state_conv[0], (1, 0, 2)), act, cw, vec(conv_b[0]))
    new_conv_sample = jnp.transpose(new_state, (1, 0, 2))[None]
    cw_pad = jnp.concatenate([cw, jnp.zeros((1, D_MODEL), F32)], axis=0)
    m_all, vn_s = _seqmix(act, conv_s, wmix, bmix, w00, b00, vec(ln_v_g[0]), vec(ln_v_b[0]), cw_pad,
                          vec(conv_b[0]), vec(ln_c_g[0]), vec(ln_c_b[0]), w_proj_a[0], w_proj_b[0])

    route_pad = LANES - MOE_GROUPS - N_EXPERTS
    w_route = jnp.concatenate([w_router_group[0], w_router_expert[0], jnp.zeros((D_MODEL, route_pad), F32)], axis=1)
    b_route = jnp.concatenate([b_router_group[0], b_router_expert[0], jnp.zeros((route_pad,), F32)]).reshape(1, LANES)

    h_all, ids, ew, counts = _mixout(m_all, xp, xs, w_out[0], vec(norm_ffn[0]), w_route.astype(BF16), b_route)

    tables = _dispatch_tables(ids[:, :2 * TOP_K], counts[0].reshape(N_EXPERTS, SUBLANES))
    ypairs = _experts(*tables, h_all, vec(norm_ffn[0]), w_exp_gate[0], w_exp_up[0], w_exp_down[0])

    y_prompt, y_sample = _final(h_all, ypairs, ew, p_prompt[0].reshape(N_PROMPT, PLE_DIM),
                                p_sample[0].reshape(N_SAMPLE, PLE_DIM), w_ple_gate[0], w_ple_proj[0],
                                vec(norm_ple[0]), vec(final_norm))
    y_prompt = y_prompt.reshape(N_PROMPT_SEQ, SEQ, D_MODEL)
    y_sample = y_sample.reshape(N_SAMPLE, 1, D_MODEL)
    glu_cols = slice(ACT_GLU * D_MODEL, (ACT_GLU + 1) * D_MODEL)
    new_conv_prompt = jnp.stack([act[(b + 1) * SEQ - CONV_HIST:(b + 1) * SEQ, glu_cols]
                                 for b in range(N_PROMPT_SEQ)])[None]
    return (y_prompt, y_sample, new_conv_prompt, new_conv_sample, vn_s.reshape(1, N_SAMPLE, 1, D_MODEL))
```

```python
import jax
import jax.numpy as jnp
from jax import lax
from jax.experimental import pallas as pl
from jax.experimental.pallas import tpu as pltpu

F32 = jnp.float32
BF16 = jnp.bfloat16

D_MODEL = 2048
N_PROMPT_SEQ = 4
SEQ = 2048
N_PROMPT = N_PROMPT_SEQ * SEQ
N_SAMPLE = 128
N_ROWS = N_PROMPT + N_SAMPLE
CHUNK = 128
N_CHUNKS = N_ROWS // CHUNK
CHUNKS_PER_SEQ = SEQ // CHUNK
A_GROUPS = 8
A_GROUP_DIM = D_MODEL // A_GROUPS
CONV_WIDTH = 31
CONV_HIST = CONV_WIDTH - 1
HIST_BLOCK = 32
MOE_GROUPS = 4
EXPERTS_PER_GROUP = 8
N_EXPERTS = MOE_GROUPS * EXPERTS_PER_GROUP
TOP_K = 2
D_EXPERT = D_MODEL // 2
PLE_DIM = 256
EPS = 1e-6
LANES = 128

XN_TM = 1024
IN_TM = 1664
IN_TN = 512
IN_COL_TILES = D_MODEL // IN_TN
ACT_BLOCKS = 5
ACT_GU, ACT_GV, ACT_GLU, ACT_SGA, ACT_SGB = range(ACT_BLOCKS)
MIX_TM = 512
MIX_PROMPT_STEPS = N_PROMPT // MIX_TM
FIN_TM = 256
FIN_PROMPT_STEPS = N_PROMPT // FIN_TM
ROW_BLOCK = 128
SUBLANES = 8
BLOCK_TILES = ROW_BLOCK // SUBLANES
N_PAIRS = N_ROWS * TOP_K
MAX_BLOCKS = N_PAIRS // ROW_BLOCK + N_EXPERTS
ITEM_BLOCKS = 8
ITEM_ROWS = ITEM_BLOCKS * ROW_BLOCK
MAX_ITEMS = (MAX_BLOCKS + (ITEM_BLOCKS - 1) * N_EXPERTS) // ITEM_BLOCKS
EXPERT_BLOCKS = N_ROWS // ROW_BLOCK + ITEM_BLOCKS
EXPERT_SLOTS = EXPERT_BLOCKS * ROW_BLOCK
SLOT_TABLE = N_EXPERTS * EXPERT_SLOTS
EXPERT_SLICES = 4
EXPERT_TN = D_EXPERT // EXPERT_SLICES
PAIR_REGION = -(-N_ROWS // FIN_TM) * FIN_TM
DUMP_ROW0 = PAIR_REGION + N_ROWS
PAIR_ROWS = DUMP_ROW0 + ROW_BLOCK

VMEM_LIMIT = 56 * 1024 * 1024


def _rms(x, g):
    return x * lax.rsqrt(jnp.mean(x * x, axis=-1, keepdims=True) + EPS) * g


def _layer_norm(x, g, b):
    mu = jnp.mean(x, axis=-1, keepdims=True)
    xc = x - mu
    return xc * lax.rsqrt(jnp.mean(xc * xc, axis=-1, keepdims=True) + EPS) * g + b


def _sigmoid(x):
    return 1.0 / (1.0 + jnp.exp(-x))


def _silu(x):
    return x * _sigmoid(x)


def _gelu(x):
    return jax.nn.gelu(x, approximate=True)


def _bdot(a, b):
    return jnp.dot(a, b, preferred_element_type=F32)


STAGE_ROWS = 256


def _load_weight_bf16(w_hbm, dst_ref, stage_ref, sem):
    n_chunks = w_hbm.shape[0] // STAGE_ROWS

    def chunk_copy(j):
        return pltpu.make_async_copy(w_hbm.at[pl.ds(j * STAGE_ROWS, STAGE_ROWS)], stage_ref.at[j % 2], sem.at[j % 2])

    chunk_copy(0).start()
    for j in range(n_chunks):
        if j + 1 < n_chunks:
            chunk_copy(j + 1).start()
        chunk_copy(j).wait()
        dst_ref[pl.ds(j * STAGE_ROWS, STAGE_ROWS), :] = stage_ref[j % 2].astype(BF16)


_WEIGHT_STAGE = [pltpu.VMEM((2, STAGE_ROWS, D_MODEL), F32), pltpu.SemaphoreType.DMA((2,))]


def _xnorm_kernel(xp_ref, xs_ref, nm_ref, xn_ref):
    i = pl.program_id(0)

    @pl.when(i < N_PROMPT // XN_TM)
    def _():
        xn_ref[...] = _rms(xp_ref[...], nm_ref[...]).astype(BF16)

    @pl.when(i == N_PROMPT // XN_TM)
    def _():
        xn_ref[0:N_SAMPLE, :] = _rms(xs_ref[...], nm_ref[...]).astype(BF16)
        xn_ref[N_SAMPLE:, :] = jnp.zeros((XN_TM - N_SAMPLE, D_MODEL), BF16)


def _xnorm(x_prompt, x_sample, norm_mix):
    last = N_PROMPT // XN_TM - 1
    return pl.pallas_call(
        _xnorm_kernel,
        out_shape=jax.ShapeDtypeStruct((N_ROWS, D_MODEL), BF16),
        grid=(N_PROMPT // XN_TM + 1,),
        in_specs=[pl.BlockSpec((XN_TM, D_MODEL), lambda i: (jnp.minimum(i, last), 0)),
                  pl.BlockSpec((N_SAMPLE, D_MODEL), lambda i: (0, 0)),
                  pl.BlockSpec((1, D_MODEL), lambda i: (0, 0))],
        out_specs=pl.BlockSpec((XN_TM, D_MODEL), lambda i: (i, 0)),
        name="xnorm",
    )(x_prompt, x_sample, norm_mix)


def _in_w_col(j):
    t = IN_COL_TILES
    jj = j - 2 * t
    glu_col = 2 * t + (jj % 2) * t + jj // 2
    return jnp.where((j >= 2 * t) & (j < 4 * t), glu_col, j)


def _in_out_col(j):
    t = IN_COL_TILES
    return jnp.where(j < 2 * t, j, jnp.where(j < 4 * t, 2 * t + (j - 2 * t) // 2, j - t))


def _in_proj_kernel(xn_ref, w_ref, o_ref, val_ref):
    j = pl.program_id(1)
    t = IN_COL_TILES

    def z():
        return _bdot(xn_ref[...], w_ref[...].astype(BF16))

    @pl.when(j < 2 * t)
    def _():
        o_ref[...] = _gelu(z())

    @pl.when((j >= 2 * t) & (j < 4 * t) & (j % 2 == 0))
    def _():
        val_ref[...] = z()

    @pl.when((j >= 2 * t) & (j < 4 * t) & (j % 2 == 1))
    def _():
        o_ref[...] = val_ref[...] * _sigmoid(z())

    @pl.when(j >= 4 * t)
    def _():
        o_ref[...] = _sigmoid(z())


def _in_proj(xn, w_in):
    return pl.pallas_call(
        _in_proj_kernel,
        out_shape=jax.ShapeDtypeStruct((N_ROWS, ACT_BLOCKS * D_MODEL), F32),
        grid=(N_ROWS // IN_TM, 6 * IN_COL_TILES),
        in_specs=[
            pl.BlockSpec((IN_TM, D_MODEL), lambda i, j: (i, 0)),
            pl.BlockSpec((D_MODEL, IN_TN), lambda i, j: (0, _in_w_col(j))),
        ],
        out_specs=pl.BlockSpec((IN_TM, IN_TN), lambda i, j: (i, _in_out_col(j))),
        scratch_shapes=[pltpu.VMEM((IN_TM, IN_TN), F32)],
        compiler_params=pltpu.CompilerParams(
            dimension_semantics=("arbitrary", "arbitrary"), vmem_limit_bytes=VMEM_LIMIT),
        name="in_proj",
    )(xn, w_in)


SCONV_B = 32


def _sconv_kernel(st_ref, glu_ref, w_ref, wl_ref, cb_ref, new_ref, conv_ref):
    glu = glu_ref[...]
    new_ref[0:CONV_HIST - 1] = st_ref[1:CONV_HIST]
    new_ref[CONV_HIST - 1] = glu
    acc = st_ref[0] * w_ref[0:1, :]
    for k in range(1, CONV_HIST):
        acc = acc + st_ref[k] * w_ref[k:k + 1, :]
    conv_ref[...] = acc + glu * wl_ref[...] + cb_ref[...]


def _sconv(state, act, conv_w, conv_b):
    return pl.pallas_call(
        _sconv_kernel,
        out_shape=(
            jax.ShapeDtypeStruct((CONV_HIST, N_SAMPLE, D_MODEL), F32),
            jax.ShapeDtypeStruct((N_SAMPLE, D_MODEL), F32),
        ),
        grid=(N_SAMPLE // SCONV_B,),
        in_specs=[
            pl.BlockSpec((CONV_HIST, SCONV_B, D_MODEL), lambda i: (0, i, 0)),
            pl.BlockSpec((SCONV_B, D_MODEL), lambda i: (N_PROMPT // SCONV_B + i, ACT_GLU)),
            pl.BlockSpec((CONV_HIST, D_MODEL), lambda i: (0, 0)),
            pl.BlockSpec((1, D_MODEL), lambda i: (0, 0)),
            pl.BlockSpec((1, D_MODEL), lambda i: (0, 0)),
        ],
        out_specs=(
            pl.BlockSpec((CONV_HIST, SCONV_B, D_MODEL), lambda i: (0, i, 0)),
            pl.BlockSpec((SCONV_B, D_MODEL), lambda i: (i, 0)),
        ),
        name="sconv",
    )(state, act, conv_w[:CONV_HIST], conv_w[CONV_HIST:], conv_b)


CONV_STRIP = 256
WIN_ROWS = HIST_BLOCK + CHUNK


def _seqmix_kernel(gv_ref, gu_ref, cur_ref, prev_ref, sga_ref, sgb_ref, convs_ref, wmix_ref, bmix_ref, w00_ref,
                   b00_ref, lvg_ref, lvb_ref, cw_ref, cb_ref, lcg_ref, lcb_ref, wa_hbm, wb_hbm,
                   m_ref, vns_ref, win_ref, conv_ref, a_ref, b_ref, wa_ref, wb_ref, stage_ref, sem):
    c = pl.program_id(0)
    is_sample = c == N_CHUNKS - 1

    @pl.when(c == 0)
    def _():
        _load_weight_bf16(wa_hbm, wa_ref, stage_ref, sem)
        _load_weight_bf16(wb_hbm, wb_ref, stage_ref, sem)

    fresh = (c % CHUNKS_PER_SEQ) == 0
    win_ref[0:HIST_BLOCK, :] = jnp.where(fresh, 0.0, prev_ref[...]).astype(BF16).astype(F32)
    win_ref[HIST_BLOCK:WIN_ROWS, :] = cur_ref[...].astype(BF16).astype(F32)
    off = HIST_BLOCK - CONV_HIST

    vn = _layer_norm(gv_ref[...], lvg_ref[...], lvb_ref[...])
    vnb = vn.astype(BF16)
    for g in range(A_GROUPS):
        sl = slice(g * A_GROUP_DIM, (g + 1) * A_GROUP_DIM)
        s_chunk = _bdot(wmix_ref[g], vnb[:, sl]) + bmix_ref[:, sl]
        a_ref[:, sl] = (gu_ref[:, sl] * s_chunk).astype(BF16)

    @pl.when(is_sample)
    def _():
        vns_ref[...] = vn
        a_ref[...] = (gu_ref[...] * (vn * w00_ref[...] + b00_ref[...])).astype(BF16)

    pa = _bdot(a_ref[...], wa_ref[...])

    def phase_sum(r, cs):
        rows = CHUNK + (SUBLANES if r else 0)
        total = None
        for k in range(CONV_WIDTH):
            q, rk = divmod(off + k, SUBLANES)
            if rk == r:
                term = win_ref[SUBLANES * q:SUBLANES * q + rows, cs] * cw_ref[k:k + 1, cs].astype(BF16).astype(F32)
                total = term if total is None else total + term
        return total[r:r + CHUNK, :]

    for s0 in range(0, D_MODEL, CONV_STRIP):
        cs = slice(s0, s0 + CONV_STRIP)
        acc = phase_sum(0, cs)
        for r in range(1, SUBLANES):
            acc = acc + phase_sum(r, cs)
        conv_ref[:, cs] = acc + cb_ref[:, cs]

    @pl.when(is_sample)
    def _():
        conv_ref[...] = convs_ref[...]

    b_ref[...] = _silu(_layer_norm(conv_ref[...], lcg_ref[...], lcb_ref[...])).astype(BF16)

    pb = _bdot(b_ref[...], wb_ref[...])
    m_ref[...] = (sga_ref[...] * pa + sgb_ref[...] * pb).astype(BF16)


def _seqmix(act, conv_s, wmix, bmix, w00, b00, ln_v_g, ln_v_b, conv_w, conv_b, ln_c_g, ln_c_b, wa, wb):
    row = lambda c: (c, 0)
    const2 = lambda c: (0, 0)
    vec = pl.BlockSpec((1, D_MODEL), const2)
    hbm = pl.BlockSpec(memory_space=pltpu.MemorySpace.HBM)
    hist_per_chunk = CHUNK // HIST_BLOCK
    act_block = lambda blk: pl.BlockSpec((CHUNK, D_MODEL), lambda c: (c, blk))
    return pl.pallas_call(
        _seqmix_kernel,
        out_shape=(
            jax.ShapeDtypeStruct((N_ROWS, D_MODEL), BF16),
            jax.ShapeDtypeStruct((N_SAMPLE, D_MODEL), F32),
        ),
        grid=(N_CHUNKS,),
        in_specs=[
            act_block(ACT_GV), act_block(ACT_GU), act_block(ACT_GLU),
            pl.BlockSpec((HIST_BLOCK, D_MODEL), lambda c: (jnp.maximum(c * hist_per_chunk - 1, 0), ACT_GLU)),
            act_block(ACT_SGA), act_block(ACT_SGB),
            pl.BlockSpec((N_SAMPLE, D_MODEL), const2),
            pl.BlockSpec((A_GROUPS, CHUNK, CHUNK), lambda c: (0, 0, 0)),
            pl.BlockSpec((CHUNK, D_MODEL), const2),
            vec, vec, vec, vec,
            pl.BlockSpec((CONV_WIDTH + 1, D_MODEL), const2),
            vec, vec, vec,
            hbm, hbm,
        ],
        out_specs=(
            pl.BlockSpec((CHUNK, D_MODEL), row),
            pl.BlockSpec((N_SAMPLE, D_MODEL), const2),
        ),
        scratch_shapes=[pltpu.VMEM((WIN_ROWS, D_MODEL), F32),
                        pltpu.VMEM((CHUNK, D_MODEL), F32),
                        pltpu.VMEM((CHUNK, D_MODEL), BF16), pltpu.VMEM((CHUNK, D_MODEL), BF16),
                        pltpu.VMEM((D_MODEL, D_MODEL), BF16), pltpu.VMEM((D_MODEL, D_MODEL), BF16)] + _WEIGHT_STAGE,
        compiler_params=pltpu.CompilerParams(
            dimension_semantics=("arbitrary",), vmem_limit_bytes=VMEM_LIMIT),
        name="seqmix",
    )(act, act, act, act, act, act, conv_s, wmix, bmix, w00, b00, ln_v_g, ln_v_b, conv_w, conv_b, ln_c_g, ln_c_b,
      wa, wb)


def _route(logits):
    col = lax.broadcasted_iota(jnp.int32, logits.shape, 1).astype(F32)
    neg = jnp.float32(-jnp.inf)
    big = jnp.float32(1e9)
    lg = jnp.where(col < MOE_GROUPS, logits, neg)
    gmax = jnp.max(lg, axis=-1, keepdims=True)
    gi = jnp.min(jnp.where(lg == gmax, col, big), axis=-1, keepdims=True)
    gw = 1.0 / jnp.sum(jnp.exp(lg - gmax), axis=-1, keepdims=True)
    lo = MOE_GROUPS + gi * EXPERTS_PER_GROUP
    le = jnp.where((col >= lo) & (col < lo + EXPERTS_PER_GROUP), logits, neg)
    m1 = jnp.max(le, axis=-1, keepdims=True)
    i1 = jnp.min(jnp.where(le == m1, col, big), axis=-1, keepdims=True)
    le2 = jnp.where(col == i1, neg, le)
    m2 = jnp.max(le2, axis=-1, keepdims=True)
    i2 = jnp.min(jnp.where(le2 == m2, col, big), axis=-1, keepdims=True)
    e = jnp.exp(m2 - m1)
    w1 = gw / (1.0 + e)
    w2 = gw * e / (1.0 + e)
    eid = jnp.where(col == 0, i1 - MOE_GROUPS, jnp.where(col == 1, i2 - MOE_GROUPS, 0.0)).astype(jnp.int32)
    ew = jnp.where(col == 0, w1, jnp.where(col == 1, w2, 0.0))
    return eid, ew


def _mixout_rows(m, x, wo_ref, nf_ref, wr_ref, br_ref, tril, count_ref):
    h = x + _bdot(m, wo_ref[...])
    hn = _rms(h, nf_ref[...])
    logits = _bdot(hn.astype(BF16), wr_ref[...]) + br_ref[...]
    eid, ew = _route(logits)
    col = lax.broadcasted_iota(jnp.int32, eid.shape, 1)
    first = col == eid[:, 0:1]
    second = col == eid[:, 1:2]
    hits = first.astype(F32) + second.astype(F32)
    before = _bdot(tril, hits.astype(BF16)) + count_ref[...]
    rank0 = jnp.sum(jnp.where(first, before, 0.0), axis=-1, keepdims=True)
    rank1 = jnp.sum(jnp.where(second, before + first.astype(F32), 0.0), axis=-1, keepdims=True)
    count_ref[...] = count_ref[...] + jnp.sum(hits, axis=0, keepdims=True)
    ids = jnp.where(col == 2, rank0.astype(jnp.int32), jnp.where(col == 3, rank1.astype(jnp.int32), eid))
    return h, ids, ew


def _mixout_kernel(mp_ref, xp_ref, ms_ref, xs_ref, wo_hbm, nf_ref, wr_ref, br_ref, tril_ref,
                   h_ref, eid_ref, ew_ref, cnt_ref, wo_ref, count_ref, stage_ref, sem):
    i = pl.program_id(0)

    @pl.when(i == 0)
    def _():
        _load_weight_bf16(wo_hbm, wo_ref, stage_ref, sem)
        count_ref[...] = jnp.zeros((1, LANES), F32)

    @pl.when(i < MIX_PROMPT_STEPS)
    def _():
        h, eid, ew = _mixout_rows(mp_ref[...], xp_ref[...], wo_ref, nf_ref, wr_ref, br_ref, tril_ref[...], count_ref)
        h_ref[...] = h
        eid_ref[...] = eid
        ew_ref[...] = ew

    @pl.when(i == MIX_PROMPT_STEPS)
    def _():
        h, eid, ew = _mixout_rows(ms_ref[...], xs_ref[...], wo_ref, nf_ref, wr_ref, br_ref,
                                  tril_ref[0:N_SAMPLE, 0:N_SAMPLE], count_ref)
        for ref, val in ((h_ref, h), (eid_ref, eid), (ew_ref, ew)):
            ref[0:N_SAMPLE, :] = val
            ref[N_SAMPLE:, :] = jnp.zeros((MIX_TM - N_SAMPLE, val.shape[1]), val.dtype)

    cnt_ref[...] = jnp.broadcast_to(count_ref[...], cnt_ref.shape).astype(jnp.int32)


def _mixout(m_all, x_prompt, x_sample, wo, norm_ffn, w_route, b_route):
    last = MIX_PROMPT_STEPS - 1
    row = lambda i: (i, 0)
    prow = lambda i: (jnp.minimum(i, last), 0)
    const2 = lambda i: (0, 0)
    hbm = pl.BlockSpec(memory_space=pltpu.MemorySpace.HBM)
    tile_row = jnp.arange(MIX_TM, dtype=jnp.int32)
    tril = (tile_row[None, :] < tile_row[:, None]).astype(BF16)
    return pl.pallas_call(
        _mixout_kernel,
        out_shape=(
            jax.ShapeDtypeStruct((N_ROWS, D_MODEL), F32),
            jax.ShapeDtypeStruct((N_ROWS, LANES), jnp.int32),
            jax.ShapeDtypeStruct((N_ROWS, LANES), F32),
            jax.ShapeDtypeStruct((SUBLANES, LANES), jnp.int32),
        ),
        grid=(MIX_PROMPT_STEPS + 1,),
        in_specs=[pl.BlockSpec((MIX_TM, D_MODEL), prow), pl.BlockSpec((MIX_TM, D_MODEL), prow),
                  pl.BlockSpec((N_SAMPLE, D_MODEL), lambda i: (N_PROMPT // N_SAMPLE, 0)),
                  pl.BlockSpec((N_SAMPLE, D_MODEL), const2),
                  hbm,
                  pl.BlockSpec((1, D_MODEL), const2),
                  pl.BlockSpec((D_MODEL, LANES), const2),
                  pl.BlockSpec((1, LANES), const2),
                  pl.BlockSpec((MIX_TM, MIX_TM), const2)],
        out_specs=(pl.BlockSpec((MIX_TM, D_MODEL), row), pl.BlockSpec((MIX_TM, LANES), row),
                   pl.BlockSpec((MIX_TM, LANES), row), pl.BlockSpec((SUBLANES, LANES), const2)),
        scratch_shapes=[pltpu.VMEM((D_MODEL, D_MODEL), BF16), pltpu.VMEM((1, LANES), F32)] + _WEIGHT_STAGE,
        compiler_params=pltpu.CompilerParams(
            dimension_semantics=("arbitrary",), vmem_limit_bytes=VMEM_LIMIT),
        name="mixout",
    )(m_all, x_prompt, m_all, x_sample, wo, norm_ffn, w_route, b_route, tril)


def _experts_kernel(item_e, item_blk0, item_nb, src_hbm, dst_hbm, h_hbm, nf_ref, wg_ref, wu_ref, wd_ref,
                    y_hbm, src_s, dst_s, pend_s, xg_ref, yst_ref, xs_ref, hm_ref, wgb_ref, wub_ref, wdb_ref,
                    idx_sem, g_sem, s_sem):
    w = pl.program_id(0)
    s = pl.program_id(1)
    nb = item_nb[w]
    nxt = jnp.minimum(w + 1, MAX_ITEMS - 1)
    nb_next = jnp.where(w + 1 < MAX_ITEMS, item_nb[nxt], 0)

    def table_fetch(tbl_hbm, tbl_s, sem_idx, item):
        slot0 = pl.multiple_of(item_blk0[item] * ROW_BLOCK, ROW_BLOCK)
        return pltpu.make_async_copy(tbl_hbm.at[pl.ds(slot0, ITEM_ROWS)], tbl_s, idx_sem.at[sem_idx])

    def row_copy_in(t, u, blk):
        src = src_s[t * SUBLANES + u]
        return pltpu.make_async_copy(h_hbm.at[src >> 3, pl.ds(src & (SUBLANES - 1), 1)],
                                     xg_ref.at[t, pl.ds(u, 1)], g_sem.at[blk])

    def row_copy_out(t, u, blk):
        dst = dst_s[t * SUBLANES + u]
        return pltpu.make_async_copy(yst_ref.at[t, pl.ds(u, 1)],
                                     y_hbm.at[dst >> 3, pl.ds(dst & (SUBLANES - 1), 1)], s_sem.at[blk])

    def block_rows(blk):
        return pl.ds(pl.multiple_of(blk * ROW_BLOCK, ROW_BLOCK), ROW_BLOCK)

    def block_tiles(blk):
        return pl.ds(pl.multiple_of(blk * BLOCK_TILES, BLOCK_TILES), BLOCK_TILES)

    def block_copy_wait(sem, blk):
        pltpu.make_async_copy(xg_ref.at[block_tiles(blk)], xg_ref.at[block_tiles(blk)], sem.at[blk]).wait()

    def for_each_row(n_blocks, row_copy):
        def tile(t, carry):
            for u in range(SUBLANES):
                row_copy(t, u, lax.shift_right_logical(t, BLOCK_TILES.bit_length() - 1)).start()
            return carry
        lax.fori_loop(0, n_blocks * BLOCK_TILES, tile, 0)

    def issue_gather(n_blocks):
        for_each_row(n_blocks, row_copy_in)

    def drain_scatter():
        def drain(blk, carry):
            block_copy_wait(s_sem, blk)
            return carry
        lax.fori_loop(0, pend_s[0], drain, 0)
        pend_s[0] = 0

    @pl.when((w == 0) & (s == 0))
    def _():
        pad_tiles = pl.ds(0, BLOCK_TILES)
        xg_ref[pad_tiles] = jnp.zeros((BLOCK_TILES, SUBLANES, D_MODEL), F32)
        for row0 in (N_ROWS, DUMP_ROW0):
            cz = pltpu.make_async_copy(xg_ref.at[pad_tiles], y_hbm.at[pl.ds(row0 // SUBLANES, BLOCK_TILES)],
                                       idx_sem.at[0])
            cz.start()
            cz.wait()
        pend_s[0] = 0
        ci = table_fetch(src_hbm, src_s, 0, w)
        ci.start()
        ci.wait()
        issue_gather(nb)

    @pl.when((s == 0) & (nb > 0))
    def _():
        table_fetch(dst_hbm, dst_s, 1, w).start()

        def land(blk, carry):
            block_copy_wait(g_sem, blk)
            x = xg_ref[block_tiles(blk)].reshape(ROW_BLOCK, D_MODEL)
            xs_ref[block_rows(blk), :] = _rms(x, nf_ref[...]).astype(BF16)
            return carry
        lax.fori_loop(0, nb, land, 0)

    @pl.when((s == 1) & (nb_next > 0))
    def _():
        table_fetch(src_hbm, src_s, 0, nxt).start()

    @pl.when((s == 2) & (nb_next > 0))
    def _():
        table_fetch(src_hbm, src_s, 0, nxt).wait()
        issue_gather(nb_next)

    def over_row_blocks(fn):
        def four_blocks(j, carry):
            fn(pl.multiple_of(j * (4 * ROW_BLOCK), 4 * ROW_BLOCK), 4 * ROW_BLOCK)
            return carry
        lax.fori_loop(0, nb >> 2, four_blocks, 0)

        @pl.when((nb & 2) == 2)
        def _():
            fn(pl.multiple_of((nb >> 2) * (4 * ROW_BLOCK), 2 * ROW_BLOCK), 2 * ROW_BLOCK)

        @pl.when((nb & 1) == 1)
        def _():
            fn(pl.multiple_of((nb - 1) * ROW_BLOCK, ROW_BLOCK), ROW_BLOCK)

    @pl.when(nb > 0)
    def _():
        wgb_ref[...] = wg_ref[...].astype(BF16)
        wub_ref[...] = wu_ref[...].astype(BF16)
        wdb_ref[pl.ds(pl.multiple_of(s * EXPERT_TN, EXPERT_TN), EXPERT_TN), :] = wd_ref[...].astype(BF16)

        def gate_up(row0, nrows):
            rows = pl.ds(row0, nrows)
            x = xs_ref[rows, :]
            hm_ref[s, rows, :] = (_silu(_bdot(x, wgb_ref[...])) * _bdot(x, wub_ref[...])).astype(BF16)
        over_row_blocks(gate_up)

    @pl.when((s == EXPERT_SLICES - 1) & (nb > 0))
    def _():
        drain_scatter()

        def down(row0, nrows):
            rows = pl.ds(row0, nrows)
            hm = jnp.concatenate([hm_ref[k, rows, :] for k in range(EXPERT_SLICES)], axis=1)
            tiles = pl.ds(pl.multiple_of(row0 // SUBLANES, BLOCK_TILES), nrows // SUBLANES)
            yst_ref[tiles] = _bdot(hm, wdb_ref[...]).reshape(nrows // SUBLANES, SUBLANES, D_MODEL)
        over_row_blocks(down)

        table_fetch(dst_hbm, dst_s, 1, w).wait()

        for_each_row(nb, row_copy_out)
        pend_s[0] = nb

    @pl.when((w == pl.num_programs(0) - 1) & (s == EXPERT_SLICES - 1))
    def _():
        drain_scatter()


def _experts(n_items, item_e, item_blk0, item_nb, src_row, dst_row, h_all, norm_ffn, w_gate, w_up, w_down):
    hbm = pl.BlockSpec(memory_space=pltpu.MemorySpace.HBM)

    def w_slice(w, s, inb):
        return jnp.where(inb[w] > 0, s, EXPERT_SLICES - 1)

    grid_spec = pltpu.PrefetchScalarGridSpec(
        num_scalar_prefetch=3,
        grid=(n_items, EXPERT_SLICES),
        in_specs=[
            hbm, hbm, hbm,
            pl.BlockSpec((1, D_MODEL), lambda w, s, ie, ib, inb: (0, 0)),
            pl.BlockSpec((None, D_MODEL, EXPERT_TN), lambda w, s, ie, ib, inb: (ie[w], 0, w_slice(w, s, inb))),
            pl.BlockSpec((None, D_MODEL, EXPERT_TN), lambda w, s, ie, ib, inb: (ie[w], 0, w_slice(w, s, inb))),
            pl.BlockSpec((None, EXPERT_TN, D_MODEL), lambda w, s, ie, ib, inb: (ie[w], w_slice(w, s, inb), 0)),
        ],
        out_specs=hbm,
        scratch_shapes=[
            pltpu.SMEM((ITEM_ROWS,), jnp.int32),
            pltpu.SMEM((ITEM_ROWS,), jnp.int32),
            pltpu.SMEM((1,), jnp.int32),
            pltpu.VMEM((ITEM_ROWS // SUBLANES, SUBLANES, D_MODEL), F32),
            pltpu.VMEM((ITEM_ROWS // SUBLANES, SUBLANES, D_MODEL), F32),
            pltpu.VMEM((ITEM_ROWS, D_MODEL), BF16),
            pltpu.VMEM((EXPERT_SLICES, ITEM_ROWS, EXPERT_TN), BF16),
            pltpu.VMEM((D_MODEL, EXPERT_TN), BF16),
            pltpu.VMEM((D_MODEL, EXPERT_TN), BF16),
            pltpu.VMEM((D_EXPERT, D_MODEL), BF16),
            pltpu.SemaphoreType.DMA((2,)),
            pltpu.SemaphoreType.DMA((ITEM_BLOCKS,)),
            pltpu.SemaphoreType.DMA((ITEM_BLOCKS,)),
        ],
    )
    return pl.pallas_call(
        _experts_kernel,
        out_shape=jax.ShapeDtypeStruct((PAIR_ROWS // SUBLANES, SUBLANES, D_MODEL), F32),
        grid_spec=grid_spec,
        compiler_params=pltpu.CompilerParams(
            dimension_semantics=("arbitrary", "arbitrary"), vmem_limit_bytes=VMEM_LIMIT),
        name="experts",
    )(item_e, item_blk0, item_nb, src_row, dst_row, h_all.reshape(N_ROWS // SUBLANES, SUBLANES, D_MODEL),
      norm_ffn, w_gate, w_up, w_down).reshape(PAIR_ROWS, D_MODEL)


def _dispatch_tables(ids, counts):
    flat_e = ids[:, 0:TOP_K].reshape(-1)
    rank = ids[:, TOP_K:2 * TOP_K].reshape(-1)
    slot = flat_e * EXPERT_SLOTS + rank
    slot_pair = jnp.full((SLOT_TABLE,), N_PAIRS, jnp.int32).at[slot].set(jnp.arange(N_PAIRS, dtype=jnp.int32))
    slot_id = jnp.arange(SLOT_TABLE, dtype=jnp.int32)
    src_row = jnp.minimum(slot_pair // TOP_K, N_ROWS - 1)
    dst_row = jnp.where(slot_pair < N_PAIRS, (slot_pair % TOP_K) * PAIR_REGION + slot_pair // TOP_K,
                        DUMP_ROW0 + slot_id % ROW_BLOCK)
    nblk = (counts + ROW_BLOCK - 1) // ROW_BLOCK
    nitem = (nblk + ITEM_BLOCKS - 1) // ITEM_BLOCKS
    upto = jnp.arange(N_EXPERTS)[None, :] <= jnp.arange(N_EXPERTS)[:, None]
    item_end = jnp.sum(jnp.where(upto, nitem[None, :], 0), axis=1)
    total = item_end[-1]
    idx = jnp.arange(MAX_ITEMS, dtype=jnp.int32)
    live = idx < total
    at = jnp.minimum(idx, total - 1)
    e_of = jnp.minimum(jnp.sum((item_end[None, :] <= at[:, None]).astype(jnp.int32), axis=1), N_EXPERTS - 1)
    local = jnp.minimum(idx, total - 1) - (item_end - nitem)[e_of]
    item_blk0 = (e_of * EXPERT_BLOCKS + local * ITEM_BLOCKS).astype(jnp.int32)
    item_nb = jnp.where(live, jnp.clip(nblk[e_of] - local * ITEM_BLOCKS, 0, ITEM_BLOCKS), 0).astype(jnp.int32)
    return total.astype(jnp.int32), e_of, item_blk0, item_nb, src_row, dst_row


def _final_rows(h, y0, y1, ew, p, wg_ref, wp_ref, np_ref, fn_ref):
    h = h + (ew[:, 0:1] * y0 + ew[:, 1:2] * y1)
    gate = _sigmoid(_bdot(_rms(h, np_ref[...]).astype(BF16), wg_ref[...]))
    h = h + gate * _bdot(p.astype(BF16), wp_ref[...])
    return _rms(h, fn_ref[...])


def _final_kernel(h_ref, y0_ref, y1_ref, ew_ref, p_ref, hs_ref, y0s_ref, y1s_ref, ews_ref, ps_ref,
                  wg_hbm, wp_hbm, np_ref, fn_ref, op_ref, os_ref, wg_ref, wp_ref, stage_ref, sem):
    i = pl.program_id(0)

    @pl.when(i == 0)
    def _():
        _load_weight_bf16(wg_hbm, wg_ref, stage_ref, sem)
        _load_weight_bf16(wp_hbm, wp_ref, stage_ref, sem)

    @pl.when(i < FIN_PROMPT_STEPS)
    def _():
        op_ref[...] = _final_rows(h_ref[...], y0_ref[...], y1_ref[...], ew_ref[...], p_ref[...],
                                  wg_ref, wp_ref, np_ref, fn_ref)

    @pl.when(i == FIN_PROMPT_STEPS)
    def _():
        os_ref[...] = _final_rows(hs_ref[...], y0s_ref[...], y1s_ref[...], ews_ref[...], ps_ref[...],
                                  wg_ref, wp_ref, np_ref, fn_ref)


def _final(h_all, ypairs, ew, p_prompt, p_sample, w_gate, w_proj, norm_ple, final_norm):
    last = FIN_PROMPT_STEPS - 1
    prow = lambda i: (jnp.minimum(i, last), 0)
    const2 = lambda i: (0, 0)
    hbm = pl.BlockSpec(memory_space=pltpu.MemorySpace.HBM)
    sblk = N_PROMPT // N_SAMPLE
    pact = pl.BlockSpec((FIN_TM, D_MODEL), prow)
    sact = lambda off: pl.BlockSpec((N_SAMPLE, D_MODEL), lambda i: (off + sblk, 0))
    return pl.pallas_call(
        _final_kernel,
        out_shape=(jax.ShapeDtypeStruct((N_PROMPT, D_MODEL), F32), jax.ShapeDtypeStruct((N_SAMPLE, D_MODEL), F32)),
        grid=(FIN_PROMPT_STEPS + 1,),
        in_specs=[pact, pact,
                  pl.BlockSpec((FIN_TM, D_MODEL), lambda i: (jnp.minimum(i, last) + PAIR_REGION // FIN_TM, 0)),
                  pl.BlockSpec((FIN_TM, LANES), prow),
                  pl.BlockSpec((FIN_TM, PLE_DIM), prow),
                  sact(0), sact(0), sact(PAIR_REGION // N_SAMPLE),
                  pl.BlockSpec((N_SAMPLE, LANES), lambda i: (sblk, 0)),
                  pl.BlockSpec((N_SAMPLE, PLE_DIM), const2),
                  hbm, hbm,
                  pl.BlockSpec((1, D_MODEL), const2),
                  pl.BlockSpec((1, D_MODEL), const2)],
        out_specs=(pact, pl.BlockSpec((N_SAMPLE, D_MODEL), const2)),
        scratch_shapes=[pltpu.VMEM((D_MODEL, D_MODEL), BF16), pltpu.VMEM((PLE_DIM, D_MODEL), BF16)] + _WEIGHT_STAGE,
        compiler_params=pltpu.CompilerParams(
            dimension_semantics=("arbitrary",), vmem_limit_bytes=VMEM_LIMIT),
        name="final",
    )(h_all, ypairs, ypairs, ew, p_prompt, h_all, ypairs, ypairs, ew, p_sample, w_gate, w_proj, norm_ple, final_norm)


def kernel(x_prompt, x_sample, state_conv, p_prompt, p_sample, norm_mix, w_in, ln_v_g, ln_v_b, w_spatial,
           b_spatial, w_proj_a, conv_w, conv_b, ln_c_g, ln_c_b, w_proj_b, w_out, norm_ffn, w_router_group,
           b_router_group, w_router_expert, b_router_expert, w_exp_gate, w_exp_up, w_exp_down, norm_ple,
           w_ple_gate, w_ple_proj, final_norm):
    assert w_in.shape[0] == 1, "single layer"
    vec = lambda v: v.reshape(1, -1)
    xp = x_prompt.reshape(N_PROMPT, D_MODEL)
    xs = x_sample.reshape(N_SAMPLE, D_MODEL)
    xn = _xnorm(xp, xs, vec(norm_mix[0]))

    w_s, b_s = w_spatial[0], b_spatial[0]
    wmix = jnp.tril(w_s).astype(BF16)
    bmix = jnp.repeat(b_s.T, A_GROUP_DIM, axis=1)
    w00 = jnp.repeat(w_s[:, 0, 0], A_GROUP_DIM).reshape(1, D_MODEL)
    b00 = jnp.repeat(b_s[:, 0], A_GROUP_DIM).reshape(1, D_MODEL)

    act = _in_proj(xn, w_in[0])

    cw = conv_w[0]
    new_state, conv_s = _sconv(jnp.transpose(state_conv[0], (1, 0, 2)), act, cw, vec(conv_b[0]))
    new_conv_sample = jnp.transpose(new_state, (1, 0, 2))[None]
    cw_pad = jnp.concatenate([cw, jnp.zeros((1, D_MODEL), F32)], axis=0)
    m_all, vn_s = _seqmix(act, conv_s, wmix, bmix, w00, b00, vec(ln_v_g[0]), vec(ln_v_b[0]), cw_pad,
                          vec(conv_b[0]), vec(ln_c_g[0]), vec(ln_c_b[0]), w_proj_a[0], w_proj_b[0])

    route_pad = LANES - MOE_GROUPS - N_EXPERTS
    w_route = jnp.concatenate([w_router_group[0], w_router_expert[0], jnp.zeros((D_MODEL, route_pad), F32)], axis=1)
    b_route = jnp.concatenate([b_router_group[0], b_router_expert[0], jnp.zeros((route_pad,), F32)]).reshape(1, LANES)

    h_all, ids, ew, counts = _mixout(m_all, xp, xs, w_out[0], vec(norm_ffn[0]), w_route.astype(BF16), b_route)

    tables = _dispatch_tables(ids[:, :2 * TOP_K], counts[0, :N_EXPERTS])
    ypairs = _experts(*tables, h_all, vec(norm_ffn[0]), w_exp_gate[0], w_exp_up[0], w_exp_down[0])

    y_prompt, y_sample = _final(h_all, ypairs, ew, p_prompt[0].reshape(N_PROMPT, PLE_DIM),
                                p_sample[0].reshape(N_SAMPLE, PLE_DIM), w_ple_gate[0], w_ple_proj[0],
                                vec(norm_ple[0]), vec(final_norm))
    y_prompt = y_prompt.reshape(N_PROMPT_SEQ, SEQ, D_MODEL)
    y_sample = y_sample.reshape(N_SAMPLE, 1, D_MODEL)
    glu_cols = slice(ACT_GLU * D_MODEL, (ACT_GLU + 1) * D_MODEL)
    new_conv_prompt = jnp.stack([act[(b + 1) * SEQ - CONV_HIST:(b + 1) * SEQ, glu_cols]
                                 for b in range(N_PROMPT_SEQ)])[None]
    return (y_prompt, y_sample, new_conv_prompt, new_conv_sample, vn_s.reshape(1, N_SAMPLE, 1, D_MODEL))
```
